```python
import math
import jax
import jax.numpy as jnp
from jax import lax
import numpy as np

D_MODEL = 4096
BATCH = 4
SEQ = 2048
DEPTH = 2
DEC_BATCH = 8
DEC_SEQ = 1
PAST_LEN = 16384
PAGE_SIZE = 128

HEAD_DIM = 128
MIX_W = D_MODEL
A_V_HEADS = (3 * MIX_W // 8) // HEAD_DIM
A_K_HEADS = A_V_HEADS // 2
A_KEY_W = A_K_HEADS * HEAD_DIM
A_VAL_W = A_V_HEADS * HEAD_DIM
A_QKV = 2 * A_KEY_W + A_VAL_W
A_CONV_W = 4
A_CHUNK = 64
C_HEADS = A_V_HEADS
C_W = C_HEADS * HEAD_DIM
B_CH = MIX_W - A_VAL_W - C_W
B_CONV_W = 31
Q_BLOCK = 128
D_FF = 256 * ((8 * D_MODEL // 3 + 255) // 256)
FFN_CONV_W = 3
P_IN = A_QKV + A_VAL_W + 2 * A_V_HEADS + 2 * B_CH + 3 * C_W + C_HEADS
FORGET_BIAS_INIT = 2.0
EPS = 1e-6

kernel_name = 'hybrid_gdn_conformer_fox_step'


def _rmsnorm(x, g):
    xf = x.astype(jnp.float32)
    y = xf * lax.rsqrt(jnp.mean(xf * xf, axis=-1, keepdims=True) + EPS)
    return (y * g.astype(jnp.float32)).astype(x.dtype)


def _layernorm(x, g, b):
    xf = x.astype(jnp.float32)
    xc = xf - jnp.mean(xf, axis=-1, keepdims=True)
    y = xc * lax.rsqrt(jnp.mean(xc * xc, axis=-1, keepdims=True) + EPS)
    return (y * g.astype(jnp.float32) + b.astype(jnp.float32)).astype(x.dtype)


def _l2norm(x):
    return x * lax.rsqrt(jnp.sum(x * x, axis=-1, keepdims=True) + EPS)


def _causal_dwconv(x, prev, w):
    width = w.shape[0]
    t = x.shape[1]
    xc = jnp.concatenate([prev.astype(x.dtype), x], axis=1)
    w = w.astype(x.dtype)
    y = xc[:, 0:t] * w[0]
    for i in range(1, width):
        y = y + xc[:, i:i + t] * w[i]
    return y, xc[:, xc.shape[1] - (width - 1):]


def _in_splits():
    sizes = [A_QKV, A_VAL_W, A_V_HEADS, A_V_HEADS, 2 * B_CH, 3 * C_W, C_HEADS]
    return [int(s) for s in np.cumsum(sizes)[:-1]]


def _gated_delta_rule(q, k, v, g, beta, s0):
    f32 = jnp.float32
    bn, t, h, _ = q.shape
    dv = v.shape[-1]
    c = A_CHUNK if t >= A_CHUNK else t
    n = -(-t // c)
    pad = n * c - t
    q, k, v, g, beta = (a.astype(f32) for a in (q, k, v, g, beta))
    if pad:
        def _pad(a):
            return jnp.pad(a, [(0, 0), (0, pad)] + [(0, 0)] * (a.ndim - 2))
        q, k, v, g, beta = (_pad(a) for a in (q, k, v, g, beta))

    def _chunks(a):
        a = a.reshape((bn, n, c) + a.shape[2:])
        return jnp.swapaxes(jnp.moveaxis(a, 1, 0), 2, 3)

    q, k, v, g, beta = (_chunks(a) for a in (q, k, v, g, beta))
    gc = jnp.cumsum(g, axis=-1)
    idx = jnp.arange(c)
    lower = idx[:, None] >= idx[None, :]
    strict = idx[:, None] > idx[None, :]
    decay = jnp.exp(jnp.where(lower, gc[..., :, None] - gc[..., None, :], -jnp.inf))
    kb = k * beta[..., None]
    vb = v * beta[..., None]
    a_mat = jnp.where(strict, jnp.einsum('nbhid,nbhjd->nbhij', kb, k) * decay, 0.0)
    l_mat = a_mat + jnp.eye(c, dtype=f32)
    u = lax.linalg.triangular_solve(l_mat, vb, left_side=True, lower=True, unit_diagonal=True)
    w = lax.linalg.triangular_solve(l_mat, kb * jnp.exp(gc)[..., None], left_side=True, lower=True, unit_diagonal=True)

    def _step(s, inp):
        qc, kc, uc, wc, gcc, dc = inp
        attn = jnp.einsum('bhid,bhjd->bhij', qc, kc) * dc
        v_new = uc - jnp.einsum('bhck,bhkv->bhcv', wc, s)
        o = (jnp.einsum('bhck,bhkv->bhcv', qc * jnp.exp(gcc)[..., None], s)
             + jnp.einsum('bhij,bhjv->bhiv', attn, v_new))
        g_last = gcc[..., -1]
        s = (s * jnp.exp(g_last)[..., None, None]
             + jnp.einsum('bhck,bhcv->bhkv', kc * jnp.exp(g_last[..., None] - gcc)[..., None], v_new))
        return s, o

    s_final, o = lax.scan(_step, s0.astype(f32), (q, k, u, w, gc, decay))
    o = jnp.transpose(o, (1, 0, 3, 2, 4)).reshape(bn, n * c, h, dv)[:, :t]
    return o, s_final


def _gdn_mix(qkv, z, b, a, conv_prev, s_prev, conv_w, a_log, dt_bias, norm_w):
    f32 = jnp.float32
    bn, t, _ = qkv.shape
    out_dtype = qkv.dtype
    qkv, conv_new = _causal_dwconv(qkv, conv_prev, conv_w)
    qkv = jax.nn.silu(qkv)
    q, k, v = jnp.split(qkv, [A_KEY_W, 2 * A_KEY_W], axis=-1)
    rep = A_V_HEADS // A_K_HEADS
    q = jnp.repeat(_l2norm(q.astype(f32).reshape(bn, t, A_K_HEADS, HEAD_DIM)), rep, axis=2) * HEAD_DIM ** -0.5
    k = jnp.repeat(_l2norm(k.astype(f32).reshape(bn, t, A_K_HEADS, HEAD_DIM)), rep, axis=2)
    v = v.reshape(bn, t, A_V_HEADS, HEAD_DIM)
    beta = jax.nn.sigmoid(b.astype(f32))
    g = -jnp.exp(a_log.astype(f32)) * jax.nn.softplus(a.astype(f32) + dt_bias.astype(f32))
    o, s_new = _gated_delta_rule(q, k, v, g, beta, s_prev)
    o = _rmsnorm(o, norm_w) * jax.nn.silu(z.astype(f32).reshape(bn, t, A_V_HEADS, HEAD_DIM))
    return o.reshape(bn, t, A_VAL_W).astype(out_dtype), s_new.astype(s_prev.dtype), conv_new


def _conformer_mix(glu_in, conv_prev, dw_w, dw_b, ln_g, ln_b):
    val, gate = jnp.split(glu_in, 2, axis=-1)
    u = val * jax.nn.sigmoid(gate)
    y, conv_new = _causal_dwconv(u, conv_prev, dw_w)
    y = _layernorm(y + dw_b.astype(y.dtype), ln_g, ln_b)
    return jax.nn.silu(y), conv_new


def _fox_block(q, c_q, q_pos, k, v, c_k, k_pos):
    s = jnp.einsum('bqhd,bkhd->bhqk', q, k, preferred_element_type=jnp.float32) * HEAD_DIM ** -0.5
    s = s + (jnp.swapaxes(c_q, 1, 2)[..., :, None] - jnp.swapaxes(c_k, 1, 2)[..., None, :])
    s = jnp.where(k_pos[None, :] <= q_pos[:, None], s, -jnp.inf)
    p = jax.nn.softmax(s, axis=-1)
    return jnp.einsum('bhqk,bkhd->bqhd', p.astype(v.dtype), v)


def _fox_prompt(q, k, v, logf):
    bn, t, h, d = q.shape
    c = jnp.cumsum(logf, axis=1)
    blk = Q_BLOCK if t % Q_BLOCK == 0 else t
    nb = t // blk
    pos = jnp.arange(t)
    qb = jnp.moveaxis(q.reshape(bn, nb, blk, h, d), 1, 0)
    cb = jnp.moveaxis(c.reshape(bn, nb, blk, h), 1, 0)
    pb = pos.reshape(nb, blk)
    o = lax.map(lambda xs: _fox_block(xs[0], xs[1], xs[2], k, v, c, pos), (qb, cb, pb))
    return jnp.moveaxis(o, 0, 1).reshape(bn, t, h * d)


def _fox_sample(q, k, v, logf, k_pool, v_pool, logf_pool, page_table):
    bn, t, h, d = q.shape
    k_past = k_pool[page_table].reshape(bn, -1, h, d).astype(k.dtype)
    v_past = v_pool[page_table].reshape(bn, -1, h, d).astype(v.dtype)
    lf_past = logf_pool[page_table].reshape(bn, -1, h).astype(jnp.float32)
    p_len = k_past.shape[1]
    c_past = jnp.cumsum(lf_past, axis=1)
    c_new = c_past[:, -1:] + jnp.cumsum(logf, axis=1)
    k_all = jnp.concatenate([k_past, k], axis=1)
    v_all = jnp.concatenate([v_past, v], axis=1)
    c_all = jnp.concatenate([c_past, c_new], axis=1)
    q_pos = p_len + jnp.arange(t)
    k_pos = jnp.arange(p_len + t)
    o = _fox_block(q, c_new, q_pos, k_all, v_all, c_all, k_pos)
    return o.reshape(bn, t, h * d)


def _layer(x, p, gdn_conv_prev, gdn_s_prev, conf_conv_prev, ffn_conv_prev, past):
    bn, t, _ = x.shape
    h = _rmsnorm(x, p['norm_mix'])
    proj = jnp.einsum('btd,dp->btp', h, p['w_in'])
    qkv_a, z_a, b_a, a_a, glu_b, qkv_c, f_c = jnp.split(proj, _in_splits(), axis=-1)
    o_a, gdn_s_new, gdn_conv_new = _gdn_mix(qkv_a, z_a, b_a, a_a, gdn_conv_prev, gdn_s_prev, p['gdn_conv_w'],
                                           p['gdn_a_log'], p['gdn_dt_bias'], p['gdn_norm_w'])
    o_b, conf_conv_new = _conformer_mix(glu_b, conf_conv_prev, p['conf_dw_w'], p['conf_dw_b'],
                                        p['conf_ln_g'], p['conf_ln_b'])
    q_c, k_c, v_c = (a.reshape(bn, t, C_HEADS, HEAD_DIM) for a in jnp.split(qkv_c, 3, axis=-1))
    logf = jax.nn.log_sigmoid((f_c + p['fox_b_f']).astype(jnp.float32))
    if past is None:
        o_c = _fox_prompt(q_c, k_c, v_c, logf)
    else:
        o_c = _fox_sample(q_c, k_c, v_c, logf, *past)
    x = x + jnp.einsum('btm,md->btd', jnp.concatenate([o_a, o_b, o_c], axis=-1), p['w_out'])
    h = _rmsnorm(x, p['norm_ffn'])
    up = jnp.einsum('btd,df->btf', h, p['w_up'])
    up, ffn_conv_new = _causal_dwconv(up, ffn_conv_prev, p['ffn_conv_w'])
    gate, val = jnp.split(up, 2, axis=-1)
    x = x + jnp.einsum('btf,fd->btd', jax.nn.silu(gate) * val, p['w_down'])
    return x, (k_c, v_c, logf.astype(x.dtype), gdn_s_new, gdn_conv_new, conf_conv_new, ffn_conv_new)


def _stack(outs, i):
    return jnp.stack([o[i] for o in outs], axis=0)


def setup_inputs(seed: int = 0) -> dict:
    key = jax.random.key(seed)
    ks = jax.random.split(key, 27)
    f32 = jnp.float32
    n_pages = PAST_LEN // PAGE_SIZE
    n_pool = (5 * DEC_BATCH * n_pages) // 4

    def nrm(k, shape, scale):
        return jax.random.normal(k, shape, f32) * scale

    dt = jnp.exp(jax.random.uniform(ks[14], (DEPTH, A_V_HEADS), f32, math.log(1e-3), math.log(1e-1)))
    page_table = jax.random.permutation(ks[5], n_pool)[:DEC_BATCH * n_pages]
    return {
        'x_prompt': nrm(ks[0], (BATCH, SEQ, D_MODEL), 1.0),
        'x_sample': nrm(ks[1], (DEC_BATCH, DEC_SEQ, D_MODEL), 1.0),
        'cache_k': nrm(ks[2], (DEPTH, n_pool, PAGE_SIZE, C_HEADS, HEAD_DIM), 1.0),
        'cache_v': nrm(ks[3], (DEPTH, n_pool, PAGE_SIZE, C_HEADS, HEAD_DIM), 1.0),
        'cache_logf': jax.nn.log_sigmoid(FORGET_BIAS_INIT + nrm(ks[4], (DEPTH, n_pool, PAGE_SIZE, C_HEADS), 1.0)),
        'page_table': page_table.reshape(DEC_BATCH, n_pages).astype(jnp.int32),
        'state_gdn': nrm(ks[6], (DEPTH, DEC_BATCH, A_V_HEADS, HEAD_DIM, HEAD_DIM), 0.1),
        'state_gdn_conv': nrm(ks[7], (DEPTH, DEC_BATCH, A_CONV_W - 1, A_QKV), 1.0),
        'state_conf_conv': nrm(ks[8], (DEPTH, DEC_BATCH, B_CONV_W - 1, B_CH), 1.0),
        'state_ffn_conv': nrm(ks[9], (DEPTH, DEC_BATCH, FFN_CONV_W - 1, 2 * D_FF), 1.0),
        'norm_mix': 1.0 + nrm(ks[10], (DEPTH, D_MODEL), 0.02),
        'w_in': nrm(ks[11], (DEPTH, D_MODEL, P_IN), D_MODEL ** -0.5),
        'gdn_conv_w': nrm(ks[12], (DEPTH, A_CONV_W, A_QKV), A_CONV_W ** -0.5),
        'gdn_a_log': jnp.log(jax.random.uniform(ks[13], (DEPTH, A_V_HEADS), f32, 1.0, 16.0)),
        'gdn_dt_bias': dt + jnp.log(-jnp.expm1(-dt)),
        'gdn_norm_w': 1.0 + nrm(ks[15], (DEPTH, HEAD_DIM), 0.02),
        'conf_dw_w': nrm(ks[16], (DEPTH, B_CONV_W, B_CH), B_CONV_W ** -0.5),
        'conf_dw_b': nrm(ks[17], (DEPTH, B_CH), 0.02),
        'conf_ln_g': 1.0 + nrm(ks[18], (DEPTH, B_CH), 0.02),
        'conf_ln_b': nrm(ks[19], (DEPTH, B_CH), 0.02),
        'fox_b_f': FORGET_BIAS_INIT + nrm(ks[20], (DEPTH, C_HEADS), 0.1),
        'w_out': nrm(ks[21], (DEPTH, MIX_W, D_MODEL), MIX_W ** -0.5),
        'norm_ffn': 1.0 + nrm(ks[22], (DEPTH, D_MODEL), 0.02),
        'ffn_conv_w': nrm(ks[23], (DEPTH, FFN_CONV_W, 2 * D_FF), FFN_CONV_W ** -0.5),
        'w_up': nrm(ks[24], (DEPTH, D_MODEL, 2 * D_FF), D_MODEL ** -0.5),
        'w_down': nrm(ks[25], (DEPTH, D_FF, D_MODEL), D_FF ** -0.5),
        'norm_final': 1.0 + nrm(ks[26], (D_MODEL,), 0.02),
    }


def reference(x_prompt, x_sample, cache_k, cache_v, cache_logf, page_table, state_gdn, state_gdn_conv,
              state_conf_conv, state_ffn_conv, norm_mix, w_in, gdn_conv_w, gdn_a_log, gdn_dt_bias, gdn_norm_w,
              conf_dw_w, conf_dw_b, conf_ln_g, conf_ln_b, fox_b_f, w_out, norm_ffn, ffn_conv_w, w_up, w_down,
              norm_final):
    yp = x_prompt
    ys = x_sample
    bp = x_prompt.shape[0]
    outs_p = []
    outs_s = []
    for l in range(DEPTH):
        p = {'norm_mix': norm_mix[l], 'w_in': w_in[l], 'gdn_conv_w': gdn_conv_w[l], 'gdn_a_log': gdn_a_log[l],
             'gdn_dt_bias': gdn_dt_bias[l], 'gdn_norm_w': gdn_norm_w[l], 'conf_dw_w': conf_dw_w[l],
             'conf_dw_b': conf_dw_b[l], 'conf_ln_g': conf_ln_g[l], 'conf_ln_b': conf_ln_b[l],
             'fox_b_f': fox_b_f[l], 'w_out': w_out[l], 'norm_ffn': norm_ffn[l], 'ffn_conv_w': ffn_conv_w[l],
             'w_up': w_up[l], 'w_down': w_down[l]}
        dt_p = yp.dtype
        yp, st_p = _layer(yp, p,
                          jnp.zeros((bp, A_CONV_W - 1, A_QKV), dt_p),
                          jnp.zeros((bp, A_V_HEADS, HEAD_DIM, HEAD_DIM), dt_p),
                          jnp.zeros((bp, B_CONV_W - 1, B_CH), dt_p),
                          jnp.zeros((bp, FFN_CONV_W - 1, 2 * D_FF), dt_p),
                          None)
        ys, st_s = _layer(ys, p, state_gdn_conv[l], state_gdn[l], state_conf_conv[l], state_ffn_conv[l],
                          (cache_k[l], cache_v[l], cache_logf[l], page_table))
        outs_p.append(st_p)
        outs_s.append(st_s)
    y_prompt = _rmsnorm(yp, norm_final)
    y_sample = _rmsnorm(ys, norm_final)
    k_rows_prompt = _stack(outs_p, 0)
    v_rows_prompt = _stack(outs_p, 1)
    logf_rows_prompt = _stack(outs_p, 2)
    gdn_state_prompt = _stack(outs_p, 3)
    gdn_conv_prompt = _stack(outs_p, 4)
    conf_conv_prompt = _stack(outs_p, 5)
    ffn_conv_prompt = _stack(outs_p, 6)
    k_rows_sample = _stack(outs_s, 0)
    v_rows_sample = _stack(outs_s, 1)
    logf_rows_sample = _stack(outs_s, 2)
    gdn_state_sample = _stack(outs_s, 3)
    gdn_conv_sample = _stack(outs_s, 4)
    conf_conv_sample = _stack(outs_s, 5)
    ffn_conv_sample = _stack(outs_s, 6)
    return (y_prompt, y_sample, k_rows_prompt, v_rows_prompt, logf_rows_prompt, gdn_state_prompt, gdn_conv_prompt,
            conf_conv_prompt, ffn_conv_prompt, k_rows_sample, v_rows_sample, logf_rows_sample, gdn_state_sample,
            gdn_conv_sample, conf_conv_sample, ffn_conv_sample)
```

```python
import functools
import math

import jax
import jax.numpy as jnp
from jax import lax
from jax.experimental import pallas as pl
from jax.experimental.pallas import tpu as pltpu

EPS = 1e-6
HEAD_DIM = 128
GDN_CHUNK = 64
SUBLANES = 8
ROW_PAD = 16
VMEM_LIMIT = 56 * 2**20

F32 = jnp.float32
BF16 = jnp.bfloat16
_HP = lax.Precision.HIGHEST


def _params(*sem):
    return pltpu.CompilerParams(dimension_semantics=sem, vmem_limit_bytes=VMEM_LIMIT)


def _pick(n, cands):
    for c in cands:
        if n % c == 0:
            return c
    return n


def _silu(x):
    return x * jax.nn.sigmoid(x)


def _softplus(x):
    return jnp.maximum(x, 0.0) + jnp.log(1.0 + jnp.exp(-jnp.abs(x)))


def _dot_t(a, b):
    return lax.dot_general(a, b, (((1,), (1,)), ((), ())), preferred_element_type=F32)


def _hp_dot(a, b):
    return jnp.dot(a, b, precision=_HP, preferred_element_type=F32)


def _rmsnorm_kernel(x_ref, g_ref, o_ref):
    x = x_ref[...]
    ms = jnp.mean(x * x, axis=-1, keepdims=True)
    o_ref[...] = (x * lax.rsqrt(ms + EPS) * g_ref[...]).astype(o_ref.dtype)


def _rmsnorm(x, g, out_dtype):
    m, d = x.shape
    tm = _pick(m, (256, 128, 64, 32, 16, 8))
    return pl.pallas_call(
        _rmsnorm_kernel,
        out_shape=jax.ShapeDtypeStruct((m, d), out_dtype),
        grid=(m // tm,),
        in_specs=[pl.BlockSpec((tm, d), lambda i: (i, 0)), pl.BlockSpec((1, d), lambda i: (0, 0))],
        out_specs=pl.BlockSpec((tm, d), lambda i: (i, 0)),
        compiler_params=_params("parallel"),
        name="rmsnorm",
    )(x, g.reshape(1, d))


def _mm_kernel(*refs, n_a, has_res):
    a_refs = refs[:n_a]
    w_refs = refs[n_a:2 * n_a]
    o_ref = refs[-1]
    acc = jnp.dot(a_refs[0][...], w_refs[0][...], preferred_element_type=F32)
    for a_ref, w_ref in zip(a_refs[1:], w_refs[1:]):
        acc = acc + jnp.dot(a_ref[...], w_ref[...], preferred_element_type=F32)
    if has_res:
        acc = refs[2 * n_a][...] + acc
    o_ref[...] = acc


def _matmul(a_list, w_list, res=None):
    m = a_list[0].shape[0]
    n = w_list[0].shape[1]
    k_total = sum(a.shape[1] for a in a_list)
    tm = _pick(m, (1024, 512, 256, 128, 64, 32, 16)) if k_total <= 4096 else _pick(m, (512, 256, 128, 64, 32, 16))
    tn = _pick(n, (512, 256, 128)) if k_total <= 4096 else _pick(n, (256, 128))
    in_specs = [pl.BlockSpec((tm, a.shape[1]), lambda i, j: (i, 0)) for a in a_list]
    in_specs += [pl.BlockSpec((w.shape[0], tn), lambda i, j: (0, j)) for w in w_list]
    args = list(a_list) + list(w_list)
    if res is not None:
        in_specs.append(pl.BlockSpec((tm, tn), lambda i, j: (i, j)))
        args.append(res)
    return pl.pallas_call(
        functools.partial(_mm_kernel, n_a=len(a_list), has_res=res is not None),
        out_shape=jax.ShapeDtypeStruct((m, n), F32),
        grid=(m // tm, n // tn),
        in_specs=in_specs,
        out_specs=pl.BlockSpec((tm, tn), lambda i, j: (i, j)),
        compiler_params=_params("parallel", "parallel"),
        name="matmul",
    )(*args)


CAST_ROWS = 512


def _cast_rows(dst_ref, c0, c1, load):
    k = dst_ref.shape[0]
    step = CAST_ROWS if k % CAST_ROWS == 0 else k

    def body(c, carry):
        rows = pl.ds(pl.multiple_of(c * step, step), step)
        dst_ref[rows, c0:c1] = load(rows).astype(BF16)
        return carry

    lax.fori_loop(0, k // step, body, 0)


def _wmm_kernel(a_ref, wa_ref, wb_ref, o_ref, wbf_ref, *, shift, n_plain):
    j = pl.program_id(0)
    i = pl.program_id(1)
    tn = wa_ref.shape[1]

    def plain():
        _cast_rows(wbf_ref, 0, tn, lambda rows: wa_ref[rows, :])

    def shifted():
        _cast_rows(wbf_ref, 0, tn,
                   lambda rows: jnp.concatenate([wa_ref[rows, :], wb_ref[rows, :]], axis=1)[:, shift:shift + tn])

    @pl.when(i == 0)
    def _():
        if shift == 0:
            plain()
        else:
            pl.when(j < n_plain)(plain)
            pl.when(j >= n_plain)(shifted)

    o_ref[...] = jnp.dot(a_ref[...], wbf_ref[...], preferred_element_type=F32)


def _wmatmul(a, w, l, c0, ncols, tn, plain_cols=0, shift=0):
    m, k = a.shape
    assert c0 % tn == 0 and ncols % tn == 0 and plain_cols % tn == 0 and tn % HEAD_DIM == 0 and shift < HEAD_DIM
    tm = _pick(m, (1024, 512, 256, 128, 64, 32, 16))
    jb = c0 // tn
    n_plain = max(0, min(ncols, plain_cols - c0) // tn) if shift else ncols // tn
    lanes_per = tn // HEAD_DIM
    last_blk = -(-w.shape[2] // HEAD_DIM) - 1
    return pl.pallas_call(
        functools.partial(_wmm_kernel, shift=shift, n_plain=n_plain),
        out_shape=jax.ShapeDtypeStruct((m, ncols), F32),
        grid=(ncols // tn, m // tm),
        in_specs=[pl.BlockSpec((tm, k), lambda j, i: (i, 0)),
                  pl.BlockSpec((None, k, tn), lambda j, i: (l, 0, jb + j)),
                  pl.BlockSpec((None, k, HEAD_DIM),
                               lambda j, i: (l, 0, jnp.minimum((jb + j + 1) * lanes_per, last_blk)))],
        out_specs=pl.BlockSpec((tm, tn), lambda j, i: (i, j)),
        scratch_shapes=[pltpu.VMEM((k, tn), BF16)],
        compiler_params=_params("arbitrary", "arbitrary"),
        name="wmatmul",
    )(a, w, w)


def _gates_kernel(ps_ref, pf_ref, alog_ref, dtb_ref, bf_ref, gb_ref, lf_ref, c_ref, carry_ref, *, nvh, nch):
    t = pl.program_id(1)
    ps = ps_ref[...]
    tt = ps.shape[0]
    b = ps[:, 0:nvh]
    a = ps[:, nvh:2 * nvh]
    f = pf_ref[:, 2 * nvh:2 * nvh + nch]
    gb_ref[:, 0:nvh] = -jnp.exp(alog_ref[...]) * _softplus(a + dtb_ref[...])
    gb_ref[:, nvh:2 * nvh] = jax.nn.sigmoid(b)
    lf = -_softplus(-(f + bf_ref[...]))
    lf_ref[...] = lf

    @pl.when(t == 0)
    def _():
        carry_ref[...] = jnp.zeros_like(carry_ref)

    ii = lax.broadcasted_iota(jnp.int32, (tt, tt), 0)
    jj = lax.broadcasted_iota(jnp.int32, (tt, tt), 1)
    tri = (ii >= jj).astype(F32)
    c = _hp_dot(tri, lf) + carry_ref[...]
    c_ref[...] = c
    carry_ref[...] = c[tt - 1:tt, :]


def _gates(ps, pf, a_log, dt_bias, b_f, bsz, t):
    nvh = a_log.shape[0]
    nch = b_f.shape[0]
    tt = _pick(t, (256, 128, 64, 32, 16, 8))
    nt = t // tt
    pw = ps.shape[1]
    row = lambda b, i: (b * nt + i, 0)
    fix = lambda b, i: (0, 0)
    return pl.pallas_call(
        functools.partial(_gates_kernel, nvh=nvh, nch=nch),
        out_shape=(jax.ShapeDtypeStruct((bsz * t, 2 * nvh), F32),
                   jax.ShapeDtypeStruct((bsz * t, nch), F32),
                   jax.ShapeDtypeStruct((bsz * t, nch), F32)),
        grid=(bsz, nt),
        in_specs=[pl.BlockSpec((tt, pw), row), pl.BlockSpec((tt, pw), row), pl.BlockSpec((1, nvh), fix),
                  pl.BlockSpec((1, nvh), fix), pl.BlockSpec((1, nch), fix)],
        out_specs=(pl.BlockSpec((tt, 2 * nvh), row), pl.BlockSpec((tt, nch), row), pl.BlockSpec((tt, nch), row)),
        scratch_shapes=[pltpu.VMEM((1, nch), F32)],
        compiler_params=_params("parallel", "arbitrary"),
        name="gates",
    )(ps, pf, a_log.reshape(1, nvh), dt_bias.reshape(1, nvh), b_f.reshape(1, nch))


def _gdn_prep_kernel(x_ref, w_ref, prev_ref, qk_ref, v_ref, cnew_ref, buf_ref, *, tt, kw, width):
    t = pl.program_id(1)
    halo = width - 1
    base = SUBLANES
    lo = base - halo

    @pl.when(t == 0)
    def _():
        buf_ref[lo:base, :] = prev_ref[0]

    @pl.when(t > 0)
    def _():
        buf_ref[lo:base, :] = buf_ref[lo + tt:base + tt, :]

    buf_ref[base:base + tt, :] = x_ref[...]
    chans = x_ref.shape[1]
    for c0 in range(0, chans, HEAD_DIM):
        cs = slice(c0, c0 + HEAD_DIM)
        y = buf_ref[lo:lo + tt, cs] * w_ref[0:1, cs]
        for i in range(1, width):
            y = y + buf_ref[lo + i:lo + i + tt, cs] * w_ref[i:i + 1, cs]
        y = _silu(y)
        if c0 < 2 * kw:
            y = y * lax.rsqrt(jnp.sum(y * y, axis=-1, keepdims=True) + EPS)
            if c0 < kw:
                y = y * HEAD_DIM ** -0.5
            qk_ref[:, cs] = y
        else:
            v_ref[:, c0 - 2 * kw:c0 - 2 * kw + HEAD_DIM] = y

    @pl.when(t == pl.num_programs(1) - 1)
    def _():
        cnew_ref[0] = buf_ref[lo + tt:base + tt, :]


def _gdn_prep(pm, conv_w, prev, bsz, t, kw, vw):
    width, chans = conv_w.shape
    tt = _pick(t, (256, 128, 64))
    nt = t // tt
    row = lambda b, i: (b * nt + i, 0)
    return pl.pallas_call(
        functools.partial(_gdn_prep_kernel, tt=tt, kw=kw, width=width),
        out_shape=(jax.ShapeDtypeStruct((bsz * t, 2 * kw), F32),
                   jax.ShapeDtypeStruct((bsz * t, vw), F32),
                   jax.ShapeDtypeStruct((bsz, width - 1, chans), F32)),
        grid=(bsz, nt),
        in_specs=[pl.BlockSpec((tt, chans), row),
                  pl.BlockSpec((width, chans), lambda b, i: (0, 0)),
                  pl.BlockSpec((1, width - 1, chans), lambda b, i: (b, 0, 0))],
        out_specs=(pl.BlockSpec((tt, 2 * kw), row), pl.BlockSpec((tt, vw), row),
                   pl.BlockSpec((1, width - 1, chans), lambda b, i: (b, 0, 0))),
        scratch_shapes=[pltpu.VMEM((SUBLANES + tt, chans), F32)],
        compiler_params=_params("parallel", "arbitrary"),
        name="gdn_prep",
    )(pm, conv_w, prev)


def _bdot(a, b):
    return jnp.dot(a.astype(BF16), b.astype(BF16), preferred_element_type=F32)


def _gdn_chunk_kernel(qk_ref, v_ref, z_ref, gb_ref, grow_ref, s0_ref, nw_ref, o_ref, sout_ref, s_ref,
                      *, c, nkh, nvh, kw):
    ci = pl.program_id(1)

    @pl.when(ci == 0)
    def _():
        s_ref[...] = s0_ref[0]

    rep = nvh // nkh
    ii = lax.broadcasted_iota(jnp.int32, (c, c), 0)
    jj = lax.broadcasted_iota(jnp.int32, (c, c), 1)
    lower = ii >= jj
    strict = ii > jj
    same_blk = {}
    size = SUBLANES
    while size <= c:
        sh = int(math.log2(size))
        same_blk[size] = (ii >> sh) == (jj >> sh)
        size *= 2
    gb = gb_ref[...]
    grow = grow_ref[0]
    nw = nw_ref[...]
    heads = range(nvh)
    hsl = [slice(h * HEAD_DIM, (h + 1) * HEAD_DIM) for h in heads]
    q = [qk_ref[:, kh * HEAD_DIM:(kh + 1) * HEAD_DIM] for kh in range(nkh)]
    k = [qk_ref[:, kw + kh * HEAD_DIM:kw + (kh + 1) * HEAD_DIM] for kh in range(nkh)]
    v = [v_ref[:, hsl[h]] for h in heads]
    s_old = [s_ref[h] for h in heads]
    k16 = [x.astype(BF16) for x in k]
    kk = [_dot_t(k16[i], k16[i]) for i in range(nkh)]
    qk = [_dot_t(q[i].astype(BF16), k16[i]) for i in range(nkh)]
    beta = [gb[:, nvh + h:nvh + h + 1] for h in heads]
    gc_col = [jnp.sum(jnp.where(lower, grow[h:h + 1, :], 0.0), axis=1, keepdims=True) for h in heads]
    gc_row = [jnp.sum(jnp.where(ii <= jj, gb[:, h:h + 1], 0.0), axis=0, keepdims=True) for h in heads]
    decay = [jnp.exp(jnp.where(lower, gc_col[h] - gc_row[h], -1e30)) for h in heads]
    a = [jnp.where(strict, kk[h // rep] * beta[h] * decay[h], 0.0) for h in heads]
    d = [jnp.where(same_blk[SUBLANES], a[h], 0.0) for h in heads]
    d2 = [_bdot(d[h], d[h]) for h in heads]
    d4 = [_bdot(d2[h], d2[h]) for h in heads]
    x = [d2[h] - d[h] - _bdot(d2[h], d[h]) for h in heads]
    x = [x[h] + d4[h] + _bdot(d4[h], x[h]) for h in heads]
    size = SUBLANES
    while size < c:
        r = [jnp.where(same_blk[2 * size], a[h], 0.0) - jnp.where(same_blk[size], a[h], 0.0) for h in heads]
        y = [r[h] + _bdot(x[h], r[h]) for h in heads]
        x = [x[h] - y[h] - _bdot(y[h], x[h]) for h in heads]
        size *= 2
    egc = [jnp.exp(gc_col[h]) for h in heads]
    rhs = [jnp.concatenate([v[h] * beta[h], k[h // rep] * (beta[h] * egc[h])], axis=1) for h in heads]
    uw = [rhs[h] + _bdot(x[h], rhs[h]) for h in heads]
    s16 = [s_old[h].astype(BF16) for h in heads]
    v_new = [uw[h][:, :HEAD_DIM] - jnp.dot(uw[h][:, HEAD_DIM:].astype(BF16), s16[h], preferred_element_type=F32)
             for h in heads]
    vn16 = [v_new[h].astype(BF16) for h in heads]
    o = [jnp.dot((q[h // rep] * egc[h]).astype(BF16), s16[h], preferred_element_type=F32)
         + jnp.dot((qk[h // rep] * decay[h]).astype(BF16), vn16[h], preferred_element_type=F32) for h in heads]
    g_last = [gc_col[h][c - 1:c, :] for h in heads]
    kd = [(k[h // rep] * jnp.exp(g_last[h] - gc_col[h])).astype(BF16) for h in heads]
    s_new = [s_old[h] * jnp.exp(g_last[h])
             + lax.dot_general(kd[h], vn16[h], (((0,), (0,)), ((), ())), preferred_element_type=F32) for h in heads]
    on = [o[h] * lax.rsqrt(jnp.mean(o[h] * o[h], axis=-1, keepdims=True) + EPS) * nw for h in heads]
    for h in heads:
        s_ref[h] = s_new[h]
        o_ref[:, hsl[h]] = (on[h] * _silu(z_ref[:, hsl[h]])).astype(o_ref.dtype)

    @pl.when(ci == pl.num_programs(1) - 1)
    def _():
        sout_ref[0] = s_ref[...]


def _gdn_chunks(qk, v, pm, z_blk, gb, grow, s0, norm_w, bsz, t, nkh, nvh):
    c = GDN_CHUNK
    n = t // c
    kw = nkh * HEAD_DIM
    vw = nvh * HEAD_DIM
    row = lambda b, i: (b * n + i, 0)
    return pl.pallas_call(
        functools.partial(_gdn_chunk_kernel, c=c, nkh=nkh, nvh=nvh, kw=kw),
        out_shape=(jax.ShapeDtypeStruct((bsz * t, vw), BF16),
                   jax.ShapeDtypeStruct((bsz, nvh, HEAD_DIM, HEAD_DIM), F32)),
        grid=(bsz, n),
        in_specs=[pl.BlockSpec((c, 2 * kw), row),
                  pl.BlockSpec((c, vw), row),
                  pl.BlockSpec((c, vw), lambda b, i: (b * n + i, z_blk)),
                  pl.BlockSpec((c, 2 * nvh), row),
                  pl.BlockSpec((1, nvh, c), lambda b, i: (b * n + i, 0, 0)),
                  pl.BlockSpec((1, nvh, HEAD_DIM, HEAD_DIM), lambda b, i: (b, 0, 0, 0)),
                  pl.BlockSpec((1, HEAD_DIM), lambda b, i: (0, 0))],
        out_specs=(pl.BlockSpec((c, vw), row),
                   pl.BlockSpec((1, nvh, HEAD_DIM, HEAD_DIM), lambda b, i: (b, 0, 0, 0))),
        scratch_shapes=[pltpu.VMEM((nvh, HEAD_DIM, HEAD_DIM), F32)],
        compiler_params=_params("parallel", "arbitrary"),
        name="gdn_chunks",
    )(qk, v, pm, gb, grow, s0, norm_w.reshape(1, HEAD_DIM))


def _conf_kernel(val0_ref, val1_ref, gate0_ref, gate1_ref, w_ref, b_ref, g_ref, beta_ref, prev_ref, o_ref, cnew_ref,
                 buf_ref, *, tt, width, base):
    t = pl.program_id(1)
    halo = width - 1
    lo = base - halo
    half = val0_ref.shape[1]

    @pl.when(t == 0)
    def _():
        buf_ref[lo:base, :] = prev_ref[0]

    @pl.when(t > 0)
    def _():
        buf_ref[lo:base, :] = buf_ref[lo + tt:base + tt, :]

    buf_ref[base:base + tt, 0:half] = val0_ref[...] * jax.nn.sigmoid(gate0_ref[...])
    buf_ref[base:base + tt, half:2 * half] = val1_ref[...] * jax.nn.sigmoid(gate1_ref[...])
    y = buf_ref[lo:lo + tt, :] * w_ref[0:1, :]
    for i in range(1, width):
        y = y + buf_ref[lo + i:lo + i + tt, :] * w_ref[i:i + 1, :]
    y = y + b_ref[...]
    yc = y - jnp.mean(y, axis=-1, keepdims=True)
    yn = yc * lax.rsqrt(jnp.mean(yc * yc, axis=-1, keepdims=True) + EPS)
    o_ref[...] = _silu(yn * g_ref[...] + beta_ref[...]).astype(o_ref.dtype)

    @pl.when(t == pl.num_programs(1) - 1)
    def _():
        cnew_ref[0] = buf_ref[lo + tt:base + tt, :]


def _conformer(pm, val_blk, dw_w, dw_b, ln_g, ln_b, prev, bsz, t):
    width, ch = dw_w.shape
    tt = _pick(t, (128, 64))
    nt = t // tt
    base = -(-(width - 1) // SUBLANES) * SUBLANES
    fix = lambda b, i: (0, 0)
    half = ch // 2
    part = lambda p: pl.BlockSpec((tt, half), lambda b, i: (b * nt + i, val_blk + p))
    return pl.pallas_call(
        functools.partial(_conf_kernel, tt=tt, width=width, base=base),
        out_shape=(jax.ShapeDtypeStruct((bsz * t, ch), BF16),
                   jax.ShapeDtypeStruct((bsz, width - 1, ch), F32)),
        grid=(bsz, nt),
        in_specs=[part(0), part(1), part(2), part(3),
                  pl.BlockSpec((width, ch), fix), pl.BlockSpec((1, ch), fix), pl.BlockSpec((1, ch), fix),
                  pl.BlockSpec((1, ch), fix),
                  pl.BlockSpec((1, width - 1, ch), lambda b, i: (b, 0, 0))],
        out_specs=(pl.BlockSpec((tt, ch), lambda b, i: (b * nt + i, 0)),
                   pl.BlockSpec((1, width - 1, ch), lambda b, i: (b, 0, 0))),
        scratch_shapes=[pltpu.VMEM((base + tt, ch), F32)],
        compiler_params=_params("parallel", "arbitrary"),
        name="conformer",
    )(pm, pm, pm, pm, dw_w, dw_b.reshape(1, ch), ln_g.reshape(1, ch), ln_b.reshape(1, ch), prev)


def _fox_prompt_kernel(q_ref, k_ref, v_ref, cq_ref, ck_ref, o_ref, *, tq):
    qi = pl.program_id(2)
    q = q_ref[...].astype(BF16)
    cq = cq_ref[0]
    scale = HEAD_DIM ** -0.5

    def block(ki, carry, diag):
        m, l, acc = carry
        ks = pl.ds(pl.multiple_of(ki * tq, tq), tq)
        k = k_ref[ks, :].astype(BF16)
        v = v_ref[ks, :].astype(BF16)
        s = _dot_t(q, k) * scale + (cq - ck_ref[0, ki])
        if diag:
            ii = lax.broadcasted_iota(jnp.int32, (tq, tq), 0)
            jj = lax.broadcasted_iota(jnp.int32, (tq, tq), 1)
            s = jnp.where(jj <= ii, s, -1e30)
        m_new = jnp.maximum(m, jnp.max(s, axis=-1, keepdims=True))
        alpha = jnp.exp(m - m_new)
        p = jnp.exp(s - m_new)
        l = alpha * l + jnp.sum(p, axis=-1, keepdims=True)
        acc = alpha * acc + jnp.dot(p.astype(BF16), v, preferred_element_type=F32)
        return m_new, l, acc

    init = (jnp.full((tq, 1), -1e30, F32), jnp.zeros((tq, 1), F32), jnp.zeros((tq, HEAD_DIM), F32))
    carry = lax.fori_loop(0, qi, lambda ki, cr: block(ki, cr, False), init)
    m, l, acc = block(qi, carry, True)
    o_ref[...] = (acc / l).astype(o_ref.dtype)


def _fox_prompt(pm, q_blk, kc, vc, c_col, c_row, bsz, t, nch, tq):
    nq = t // tq
    return pl.pallas_call(
        functools.partial(_fox_prompt_kernel, tq=tq),
        out_shape=jax.ShapeDtypeStruct((bsz * t, nch * HEAD_DIM), BF16),
        grid=(bsz, nch, nq),
        in_specs=[pl.BlockSpec((tq, HEAD_DIM), lambda b, h, i: (b * nq + i, q_blk + h)),
                  pl.BlockSpec((t, HEAD_DIM), lambda b, h, i: (b, h)),
                  pl.BlockSpec((t, HEAD_DIM), lambda b, h, i: (b, h)),
                  pl.BlockSpec((1, tq, 1), lambda b, h, i: (b * nch + h, i, 0)),
                  pl.BlockSpec((1, nq, 1, tq), lambda b, h, i: (b * nch + h, 0, 0, 0))],
        out_specs=pl.BlockSpec((tq, HEAD_DIM), lambda b, h, i: (b * nq + i, h)),
        compiler_params=_params("parallel", "parallel", "parallel"),
        name="fox_prompt",
    )(pm, kc, vc, c_col, c_row)


def _ffn_up_kernel(a_ref, wg_ref, wv_ref, cwg_ref, cwv_ref, sg_ref, sv_ref, act_ref, ng_ref, nv_ref,
                   wcat_ref, buf_ref, *, tm, tn, tpb, width, n_sub):
    i = pl.program_id(1)
    halo = width - 1
    base = SUBLANES
    lo = base - halo

    @pl.when(i == 0)
    def _():
        _cast_rows(wcat_ref, 0, tn, lambda rows: wg_ref[rows, :])
        _cast_rows(wcat_ref, tn, 2 * tn, lambda rows: wv_ref[rows, :])

    first = (i % tpb) == 0

    @pl.when(first)
    def _():
        buf_ref[lo:base, 0:tn] = sg_ref[0]
        buf_ref[lo:base, tn:2 * tn] = sv_ref[0]

    @pl.when(jnp.logical_not(first))
    def _():
        buf_ref[lo:base, :] = buf_ref[lo + tm:base + tm, :]

    sub = tm // n_sub

    def conv_rows(r0):
        yg = buf_ref[lo + r0:lo + r0 + sub, 0:tn] * cwg_ref[0:1, :]
        yv = buf_ref[lo + r0:lo + r0 + sub, tn:2 * tn] * cwv_ref[0:1, :]
        for t in range(1, width):
            yg = yg + buf_ref[lo + r0 + t:lo + r0 + t + sub, 0:tn] * cwg_ref[t:t + 1, :]
            yv = yv + buf_ref[lo + r0 + t:lo + r0 + t + sub, tn:2 * tn] * cwv_ref[t:t + 1, :]
        act_ref[r0:r0 + sub, :] = (_silu(yg) * yv).astype(act_ref.dtype)

    for r in range(n_sub):
        buf_ref[base + r * sub:base + (r + 1) * sub, :] = jnp.dot(
            a_ref[r * sub:(r + 1) * sub, :], wcat_ref[...], preferred_element_type=F32)
        if r > 0:
            conv_rows((r - 1) * sub)
    conv_rows((n_sub - 1) * sub)
    ng_ref[0] = buf_ref[lo + tm:base + tm, 0:tn]
    nv_ref[0] = buf_ref[lo + tm:base + tm, tn:2 * tn]


def _ffn_up(h, w_up, l, conv_w, state, bsz, t):
    m, d = h.shape
    f2 = w_up.shape[2]
    f = f2 // 2
    width = conv_w.shape[0]
    tm = _pick(t, (1024, 512, 256, 128, 64))
    tn = _pick(f, (256, 128))
    nj = f // tn
    tpb = t // tm
    n_sub = 2 if tm >= 512 else 1
    return pl.pallas_call(
        functools.partial(_ffn_up_kernel, tm=tm, tn=tn, tpb=tpb, width=width, n_sub=n_sub),
        out_shape=(jax.ShapeDtypeStruct((m, f), BF16),
                   jax.ShapeDtypeStruct((bsz, width - 1, f), F32),
                   jax.ShapeDtypeStruct((bsz, width - 1, f), F32)),
        grid=(nj, m // tm),
        in_specs=[pl.BlockSpec((tm, d), lambda j, i: (i, 0)),
                  pl.BlockSpec((None, d, tn), lambda j, i: (l, 0, j)),
                  pl.BlockSpec((None, d, tn), lambda j, i: (l, 0, j + nj)),
                  pl.BlockSpec((width, tn), lambda j, i: (0, j)),
                  pl.BlockSpec((width, tn), lambda j, i: (0, j + nj)),
                  pl.BlockSpec((1, width - 1, tn), lambda j, i: (i // tpb, 0, j)),
                  pl.BlockSpec((1, width - 1, tn), lambda j, i: (i // tpb, 0, j + nj))],
        out_specs=(pl.BlockSpec((tm, tn), lambda j, i: (i, j)),
                   pl.BlockSpec((1, width - 1, tn), lambda j, i: (i // tpb, 0, j)),
                   pl.BlockSpec((1, width - 1, tn), lambda j, i: (i // tpb, 0, j))),
        scratch_shapes=[pltpu.VMEM((d, 2 * tn), BF16), pltpu.VMEM((SUBLANES + tm, 2 * tn), F32)],
        compiler_params=_params("arbitrary", "arbitrary"),
        name="ffn_up",
    )(h, w_up, w_up, conv_w, conv_w, state, state)


def _col_bcast(row):
    n = row.shape[1]
    return jnp.transpose(jnp.broadcast_to(row, (n, n)))


def _sample_mix_kernel(pm_ref, ps_ref, pf_ref, gprev_ref, sprev_ref, cprev_ref, gw_ref, alog_ref, dtb_ref, nw_ref,
                       cw_ref, cb_ref, lg_ref, lb_ref, bf_ref,
                       oa_ref, ob_ref, snew_ref, gnew_ref, cnew_ref, lf_ref,
                       *, nkh, nvh, nch, z_off, glu_off, bch):
    kw = nkh * HEAD_DIM
    vw = nvh * HEAD_DIM
    aqkv = 2 * kw + vw
    rep = nvh // nkh
    gwidth = gw_ref.shape[0]
    x = pm_ref[0, :, 0:aqkv]
    gprev = gprev_ref[0]
    y = jnp.sum(gprev * gw_ref[0:gwidth - 1, :], axis=0, keepdims=True) + x * gw_ref[gwidth - 1:gwidth, :]
    gnew_ref[0, 0:gwidth - 2, :] = gprev[1:gwidth - 1, :]
    gnew_ref[0, gwidth - 2:gwidth - 1, :] = x
    y = _silu(y)
    ps = ps_ref[0]
    beta_all = jax.nn.sigmoid(ps[:, 0:nvh])
    g_all = -jnp.exp(alog_ref[...]) * _softplus(ps[:, nvh:2 * nvh] + dtb_ref[...])
    lf_ref[0] = -_softplus(-(pf_ref[0, :, 2 * nvh:2 * nvh + nch] + bf_ref[...]))
    nw = nw_ref[...]
    for kh in range(nkh):
        q = y[:, kh * HEAD_DIM:(kh + 1) * HEAD_DIM]
        k = y[:, kw + kh * HEAD_DIM:kw + (kh + 1) * HEAD_DIM]
        q = q * lax.rsqrt(jnp.sum(q * q, axis=-1, keepdims=True) + EPS) * HEAD_DIM ** -0.5
        k = k * lax.rsqrt(jnp.sum(k * k, axis=-1, keepdims=True) + EPS)
        q_cols = _col_bcast(q)
        k_cols = _col_bcast(k)
        qk = jnp.sum(q * k, axis=-1, keepdims=True)
        for r in range(rep):
            h = kh * rep + r
            v = y[:, 2 * kw + h * HEAD_DIM:2 * kw + (h + 1) * HEAD_DIM]
            beta = beta_all[:, h:h + 1]
            eg = jnp.exp(g_all[:, h:h + 1])
            s = sprev_ref[0, h]
            k_s = jnp.sum(k_cols * s, axis=0, keepdims=True)
            q_s = jnp.sum(q_cols * s, axis=0, keepdims=True)
            v_new = beta * v - (beta * eg) * k_s
            o = eg * q_s + qk * v_new
            snew_ref[0, h] = s * eg + k_cols * v_new
            on = o * lax.rsqrt(jnp.mean(o * o, axis=-1, keepdims=True) + EPS) * nw
            z = pm_ref[0, :, z_off + h * HEAD_DIM:z_off + (h + 1) * HEAD_DIM]
            oa_ref[0, :, h * HEAD_DIM:(h + 1) * HEAD_DIM] = (on * _silu(z)).astype(oa_ref.dtype)
    cwidth = cw_ref.shape[0]
    val = pm_ref[0, :, glu_off:glu_off + bch]
    gate = pm_ref[0, :, glu_off + bch:glu_off + 2 * bch]
    u = val * jax.nn.sigmoid(gate)
    cprev = cprev_ref[0]
    yc = (jnp.sum(cprev * cw_ref[0:cwidth - 1, :], axis=0, keepdims=True) + u * cw_ref[cwidth - 1:cwidth, :]
          + cb_ref[...])
    cnew_ref[0, 0:cwidth - 2, :] = cprev[1:cwidth - 1, :]
    cnew_ref[0, cwidth - 2:cwidth - 1, :] = u
    yc = yc - jnp.mean(yc, axis=-1, keepdims=True)
    yn = yc * lax.rsqrt(jnp.mean(yc * yc, axis=-1, keepdims=True) + EPS)
    ob_ref[0] = _silu(yn * lg_ref[...] + lb_ref[...]).astype(ob_ref.dtype)


def _sample_mix(pm, ps, pf, gprev, sprev, cprev, gw, a_log, dt_bias, nw, cw, cb, lg, lb, b_f, nkh, z_off, glu_off):
    bsz = gprev.shape[0]
    nvh = a_log.shape[0]
    nch = b_f.shape[0]
    bch = cw.shape[1]
    vw = nvh * HEAD_DIM
    pmw = pm.shape[1]
    psw = ps.shape[1]
    pm3 = pm[:bsz].reshape(bsz, 1, pmw)
    ps3 = ps[:bsz].reshape(bsz, 1, psw)
    pf3 = pf[:bsz].reshape(bsz, 1, psw)
    per_b = lambda *blk: pl.BlockSpec((1,) + blk, lambda b: (b,) + (0,) * len(blk))
    fixed = lambda a: pl.BlockSpec(a.shape, lambda b: (0,) * a.ndim)
    consts = [gw, a_log.reshape(1, nvh), dt_bias.reshape(1, nvh), nw.reshape(1, HEAD_DIM), cw, cb.reshape(1, bch),
              lg.reshape(1, bch), lb.reshape(1, bch), b_f.reshape(1, nch)]
    return pl.pallas_call(
        functools.partial(_sample_mix_kernel, nkh=nkh, nvh=nvh, nch=nch, z_off=z_off, glu_off=glu_off, bch=bch),
        out_shape=(jax.ShapeDtypeStruct((bsz, 1, vw), BF16),
                   jax.ShapeDtypeStruct((bsz, 1, bch), BF16),
                   jax.ShapeDtypeStruct(sprev.shape, F32),
                   jax.ShapeDtypeStruct(gprev.shape, F32),
                   jax.ShapeDtypeStruct(cprev.shape, F32),
                   jax.ShapeDtypeStruct((bsz, 1, nch), F32)),
        grid=(bsz,),
        in_specs=[per_b(1, pmw), per_b(1, psw), per_b(1, psw), per_b(*gprev.shape[1:]), per_b(*sprev.shape[1:]),
                  per_b(*cprev.shape[1:])] + [fixed(a) for a in consts],
        out_specs=(per_b(1, vw), per_b(1, bch), per_b(*sprev.shape[1:]), per_b(*gprev.shape[1:]),
                   per_b(*cprev.shape[1:]), per_b(1, nch)),
        compiler_params=_params("parallel"),
        name="sample_mix",
    )(pm3, ps3, pf3, gprev, sprev, cprev, *consts)


def _head_rows(ref, h):
    n, hp, d = ref.shape
    return ref.reshape(n * hp, d)[pl.ds(h, n, stride=hp), :]


def _fox_sample_kernel(pt_ref, q_ref, kn_ref, vn_ref, lfn_ref, kp_ref, vp_ref, lfp_ref, o_ref,
                       qt_ref, m_ref, l_ref, r_ref, acc_ref, *, nch, hp):
    pi = pl.program_id(1)
    page = kp_ref.shape[0]
    scale = HEAD_DIM ** -0.5

    @pl.when(pi == 0)
    def _():
        q = q_ref[0]
        qt_ref[...] = jnp.transpose(q).astype(BF16)
        m_ref[...] = jnp.sum(jnp.transpose(q * kn_ref[0]), axis=0, keepdims=True) * scale
        l_ref[...] = jnp.ones_like(l_ref)
        r_ref[...] = lfn_ref[0]
        acc_ref[...] = vn_ref[0]

    lf = lfp_ref[...]
    jj = lax.broadcasted_iota(jnp.int32, (page, page), 0)
    mm = lax.broadcasted_iota(jnp.int32, (page, page), 1)
    later = (mm > jj).astype(BF16)
    lf_hi = lf.astype(BF16)
    lf_lo = (lf - lf_hi.astype(F32)).astype(BF16)
    bias = (r_ref[...] + jnp.dot(later, lf_hi, preferred_element_type=F32)
            + jnp.dot(later, lf_lo, preferred_element_type=F32))
    lanes = lax.broadcasted_iota(jnp.int32, (page, HEAD_DIM), 1)
    s = jnp.zeros((page, HEAD_DIM), F32)
    for h in range(nch):
        kh = _head_rows(kp_ref, h).astype(BF16)
        s = jnp.where(lanes == h, jnp.dot(kh, qt_ref[...], preferred_element_type=F32), s)
    s = s * scale + bias
    m_old = m_ref[...]
    m_new = jnp.maximum(m_old, jnp.max(s, axis=0, keepdims=True))
    alpha = jnp.exp(m_old - m_new)
    p = jnp.exp(s - m_new)
    l_ref[...] = alpha * l_ref[...] + jnp.sum(p, axis=0, keepdims=True)
    m_ref[...] = m_new
    r_ref[...] = r_ref[...] + jnp.sum(lf, axis=0, keepdims=True)
    p_t = jnp.transpose(p)[0:hp, :].astype(BF16)
    rows = lax.broadcasted_iota(jnp.int32, (hp, HEAD_DIM), 0)
    upd = jnp.zeros((hp, HEAD_DIM), F32)
    for h in range(nch):
        vh = _head_rows(vp_ref, h).astype(BF16)
        upd = jnp.where(rows == h, jnp.dot(p_t, vh, preferred_element_type=F32), upd)
    acc_ref[...] = acc_ref[...] * _col_bcast(alpha)[0:hp, :] + upd

    @pl.when(pi == pl.num_programs(1) - 1)
    def _():
        o_ref[0] = (acc_ref[...] / _col_bcast(l_ref[...])[0:hp, :]).astype(o_ref.dtype)


def _fox_sample(q, kn, vn, lfn, cache_k, cache_v, lf_pad, l, page_table, nch):
    bsz = q.shape[0]
    n_pages = page_table.shape[1]
    page = cache_k.shape[2]
    hp = vn.shape[1]
    assert page == HEAD_DIM and cache_k.shape[4] == HEAD_DIM
    pt = page_table.reshape(-1)
    per_b = lambda *blk: pl.BlockSpec((1,) + blk, lambda b, p, pt: (b,) + (0,) * len(blk))
    paged = lambda *blk: pl.BlockSpec((None, None) + blk,
                                      lambda b, p, pt: (l, pt[b * n_pages + n_pages - 1 - p]) + (0,) * len(blk))
    grid_spec = pltpu.PrefetchScalarGridSpec(
        num_scalar_prefetch=1,
        grid=(bsz, n_pages),
        in_specs=[per_b(HEAD_DIM, HEAD_DIM), per_b(HEAD_DIM, HEAD_DIM), per_b(hp, HEAD_DIM), per_b(1, HEAD_DIM),
                  paged(page, hp, HEAD_DIM), paged(page, hp, HEAD_DIM), paged(page, HEAD_DIM)],
        out_specs=per_b(hp, HEAD_DIM),
        scratch_shapes=[pltpu.VMEM((HEAD_DIM, HEAD_DIM), BF16), pltpu.VMEM((1, HEAD_DIM), F32),
                        pltpu.VMEM((1, HEAD_DIM), F32), pltpu.VMEM((1, HEAD_DIM), F32),
                        pltpu.VMEM((hp, HEAD_DIM), F32)],
    )
    return pl.pallas_call(
        functools.partial(_fox_sample_kernel, nch=nch, hp=hp),
        out_shape=jax.ShapeDtypeStruct((bsz, hp, HEAD_DIM), BF16),
        grid_spec=grid_spec,
        compiler_params=_params("parallel", "arbitrary"),
        name="fox_sample",
    )(pt, q, kn, vn, lfn, cache_k, cache_v, lf_pad)


def _ffn_sample_kernel(upg_ref, upv_ref, sg_ref, sv_ref, cwg_ref, cwv_ref, act_ref, *, width):
    ys = []
    for up_ref, s_ref, cw_ref in ((upg_ref, sg_ref, cwg_ref), (upv_ref, sv_ref, cwv_ref)):
        y = up_ref[...] * cw_ref[width - 1:width, :]
        for i in range(width - 1):
            y = y + s_ref[i] * cw_ref[i:i + 1, :]
        ys.append(y)
    act_ref[...] = (_silu(ys[0]) * ys[1]).astype(act_ref.dtype)


def _ffn_sample(up, state_t, conv_w):
    r, f2 = up.shape
    f = f2 // 2
    width = conv_w.shape[0]
    tn = _pick(f, (1024, 512, 256, 128))
    nj = f // tn
    return pl.pallas_call(
        functools.partial(_ffn_sample_kernel, width=width),
        out_shape=jax.ShapeDtypeStruct((r, f), BF16),
        grid=(nj,),
        in_specs=[pl.BlockSpec((r, tn), lambda j: (0, j)), pl.BlockSpec((r, tn), lambda j: (0, j + nj)),
                  pl.BlockSpec((width - 1, r, tn), lambda j: (0, 0, j)),
                  pl.BlockSpec((width - 1, r, tn), lambda j: (0, 0, j + nj)),
                  pl.BlockSpec((width, tn), lambda j: (0, j)), pl.BlockSpec((width, tn), lambda j: (0, j + nj))],
        out_specs=pl.BlockSpec((r, tn), lambda j: (0, j)),
        compiler_params=_params("parallel"),
        name="ffn_sample",
    )(up, up, state_t, state_t, conv_w, conv_w)


def _in_proj(h, w_in, l, o_b, shift, glu_w, cw, tn):
    main = _wmatmul(h, w_in, l, 0, o_b + glu_w + cw, tn, o_b, shift)
    k = _wmatmul(h, w_in, l, o_b + glu_w + cw, cw, tn, o_b, shift)
    v = _wmatmul(h, w_in, l, o_b + glu_w + 2 * cw, cw, tn, o_b, shift)
    ps = _wmatmul(h, w_in, l, o_b, HEAD_DIM, HEAD_DIM)
    pf = _wmatmul(h, w_in, l, o_b + glu_w + 3 * cw, HEAD_DIM, HEAD_DIM)
    return main, k, v, ps, pf


def kernel(x_prompt, x_sample, cache_k, cache_v, cache_logf, page_table, state_gdn, state_gdn_conv, state_conf_conv, state_ffn_conv, norm_mix, w_in, gdn_conv_w, gdn_a_log, gdn_dt_bias, gdn_norm_w, conf_dw_w, conf_dw_b, conf_ln_g, conf_ln_b, fox_b_f, w_out, norm_ffn, ffn_conv_w, w_up, w_down, norm_final):
    bp, t, d = x_prompt.shape
    bs, ts, _ = x_sample.shape
    assert ts == 1, "the sample step handles one new token per sequence"
    depth = w_in.shape[0]
    nvh = gdn_a_log.shape[1]
    aqkv = gdn_conv_w.shape[2]
    vw = nvh * HEAD_DIM
    kw = (aqkv - vw) // 2
    nkh = kw // HEAD_DIM
    nch = fox_b_f.shape[1]
    cw = nch * HEAD_DIM
    bch = conf_dw_w.shape[2]
    f = w_down.shape[1]
    o_b = aqkv + vw
    shift = 2 * nvh
    glu_w = 2 * bch
    tn_in = _pick(o_b, (512, 256, 128))
    assert (t % GDN_CHUNK == 0 and cw == vw and aqkv % vw == 0 and o_b % (bch // 2) == 0
            and glu_w % tn_in == 0 and cw % tn_in == 0)
    z_blk = aqkv // vw
    glu_off = o_b
    q_off = o_b + glu_w
    tq = _pick(t, (512, 256, 128))
    nq = t // tq
    n_chunks = t // GDN_CHUNK
    hp = -(-nch // SUBLANES) * SUBLANES
    tn_up = _pick(2 * f, (512, 256, 128))

    xp = x_prompt.reshape(bp * t, d)
    rs = -(-bs // ROW_PAD) * ROW_PAD
    xs = jnp.zeros((rs, d), F32).at[:bs].set(x_sample.reshape(bs, d))
    lf_pad = jnp.pad(cache_logf, ((0, 0), (0, 0), (0, 0), (0, HEAD_DIM - nch)))
    outs_p, outs_s = [], []
    for l in range(depth):
        wo = w_out[l]
        w_o = (wo[0:vw].astype(BF16), wo[vw:vw + bch].astype(BF16), wo[vw + bch:].astype(BF16))
        w_d = w_down[l].astype(BF16)

        h = _rmsnorm(xp, norm_mix[l], BF16)
        pm, kc, vc, ps, pf = _in_proj(h, w_in, l, o_b, shift, glu_w, cw, tn_in)
        gb, logf, ccum = _gates(ps, pf, gdn_a_log[l], gdn_dt_bias[l], fox_b_f[l], bp, t)
        qk, vact, gconv_new = _gdn_prep(pm, gdn_conv_w[l], jnp.zeros((bp, gdn_conv_w.shape[1] - 1, aqkv), F32),
                                        bp, t, kw, vw)
        grow = gb[:, :nvh].reshape(bp * n_chunks, GDN_CHUNK, nvh).transpose(0, 2, 1)
        o_a, s_new = _gdn_chunks(qk, vact, pm, z_blk, gb, grow, jnp.zeros((bp, nvh, HEAD_DIM, HEAD_DIM), F32),
                                 gdn_norm_w[l], bp, t, nkh, nvh)
        o_bm, cconv_new = _conformer(pm, glu_off // (bch // 2), conf_dw_w[l], conf_dw_b[l], conf_ln_g[l],
                                     conf_ln_b[l], jnp.zeros((bp, conf_dw_w.shape[1] - 1, bch), F32), bp, t)
        c_t = ccum.reshape(bp, t, nch).transpose(0, 2, 1)
        o_c = _fox_prompt(pm, q_off // HEAD_DIM, kc, vc, c_t.reshape(bp * nch, t, 1),
                          c_t.reshape(bp * nch, nq, 1, tq), bp, t, nch, tq)
        xp = _matmul([o_a, o_bm, o_c], list(w_o), res=xp)
        h = _rmsnorm(xp, norm_ffn[l], BF16)
        act, fnew_g, fnew_v = _ffn_up(h, w_up, l, ffn_conv_w[l],
                                      jnp.zeros((bp, ffn_conv_w.shape[1] - 1, 2 * f), F32), bp, t)
        xp = _matmul([act], [w_d], res=xp)
        outs_p.append((kc.reshape(bp, t, nch, HEAD_DIM), vc.reshape(bp, t, nch, HEAD_DIM),
                       logf.reshape(bp, t, nch), s_new, gconv_new, cconv_new,
                       jnp.concatenate([fnew_g, fnew_v], axis=-1)))

        h = _rmsnorm(xs, norm_mix[l], BF16)
        pm, kc, vc, ps, pf = _in_proj(h, w_in, l, o_b, shift, glu_w, cw, tn_in)
        o_a, o_bm, s_new, gconv_new, cconv_new, logf = _sample_mix(
            pm, ps, pf, state_gdn_conv[l], state_gdn[l], state_conf_conv[l], gdn_conv_w[l], gdn_a_log[l],
            gdn_dt_bias[l], gdn_norm_w[l], conf_dw_w[l], conf_dw_b[l], conf_ln_g[l], conf_ln_b[l], fox_b_f[l],
            nkh, aqkv, glu_off)
        head_rows = lambda a, rows: jnp.zeros((bs, rows, HEAD_DIM), F32).at[:, :nch].set(a.reshape(bs, nch, HEAD_DIM))
        lfn = jnp.zeros((bs, 1, HEAD_DIM), F32).at[:, :, :nch].set(logf)
        o_c = _fox_sample(head_rows(pm[:bs, q_off:q_off + cw], HEAD_DIM), head_rows(kc[:bs], HEAD_DIM),
                          head_rows(vc[:bs], hp), lfn, cache_k, cache_v, lf_pad, l, page_table, nch)
        o_c = o_c[:, :nch].reshape(bs, cw)
        pad_rows = lambda a: jnp.zeros((rs, a.shape[-1]), a.dtype).at[:bs].set(a.reshape(bs, -1))
        xs = _matmul([pad_rows(o_a), pad_rows(o_bm), pad_rows(o_c)], list(w_o), res=xs)
        h = _rmsnorm(xs, norm_ffn[l], BF16)
        up = _wmatmul(h, w_up, l, 0, 2 * f, tn_up)
        st = jnp.zeros((ffn_conv_w.shape[1] - 1, rs, 2 * f), F32).at[:, :bs].set(state_ffn_conv[l].transpose(1, 0, 2))
        act = _ffn_sample(up, st, ffn_conv_w[l])
        xs = _matmul([act], [w_d], res=xs)
        fconv_new = jnp.concatenate([state_ffn_conv[l][:, 1:], up[:bs, None, :]], axis=1)
        outs_s.append((kc[:bs].reshape(bs, 1, nch, HEAD_DIM), vc[:bs].reshape(bs, 1, nch, HEAD_DIM), logf,
                       s_new, gconv_new, cconv_new, fconv_new))

    y_prompt = _rmsnorm(xp, norm_final, F32).reshape(bp, t, d)
    y_sample = _rmsnorm(xs, norm_final, F32)[:bs].reshape(bs, 1, d)
    stack = lambda outs, i: jnp.stack([o[i] for o in outs], axis=0)
    return (y_prompt, y_sample) + tuple(stack(outs_p, i) for i in range(7)) + tuple(stack(outs_s, i) for i in range(7))
```

```python
import functools
import math

import jax
import jax.numpy as jnp
from jax import lax
from jax.experimental import pallas as pl
from jax.experimental.pallas import tpu as pltpu

EPS = 1e-6
HEAD_DIM = 128
GDN_CHUNK = 64
SUBLANES = 8
ROW_PAD = 16
VMEM_LIMIT = 56 * 2**20
FOX_PAGE_GROUP = 4

F32 = jnp.float32
BF16 = jnp.bfloat16
_HP = lax.Precision.HIGHEST


def _params(*sem):
    return pltpu.CompilerParams(dimension_semantics=sem, vmem_limit_bytes=VMEM_LIMIT)


def _pick(n, cands):
    for c in cands:
        if n % c == 0:
            return c
    return n


def _silu(x):
    return x * jax.nn.sigmoid(x)


def _softplus(x):
    return jnp.maximum(x, 0.0) + jnp.log(1.0 + jnp.exp(-jnp.abs(x)))


def _dot_t(a, b):
    return lax.dot_general(a, b, (((1,), (1,)), ((), ())), preferred_element_type=F32)


def _hp_dot(a, b):
    return jnp.dot(a, b, precision=_HP, preferred_element_type=F32)


def _rmsnorm_kernel(x_ref, g_ref, o_ref):
    x = x_ref[...]
    ms = jnp.mean(x * x, axis=-1, keepdims=True)
    o_ref[...] = (x * lax.rsqrt(ms + EPS) * g_ref[...]).astype(o_ref.dtype)


def _rmsnorm(x, g, out_dtype):
    m, d = x.shape
    tm = _pick(m, (256, 128, 64, 32, 16, 8))
    return pl.pallas_call(
        _rmsnorm_kernel,
        out_shape=jax.ShapeDtypeStruct((m, d), out_dtype),
        grid=(m // tm,),
        in_specs=[pl.BlockSpec((tm, d), lambda i: (i, 0)), pl.BlockSpec((1, d), lambda i: (0, 0))],
        out_specs=pl.BlockSpec((tm, d), lambda i: (i, 0)),
        compiler_params=_params("parallel"),
        name="rmsnorm",
    )(x, g.reshape(1, d))


def _mm_kernel(*refs, n_a, has_res):
    a_refs = refs[:n_a]
    w_refs = refs[n_a:2 * n_a]
    o_ref = refs[-1]
    acc = jnp.dot(a_refs[0][...], w_refs[0][...], preferred_element_type=F32)
    for a_ref, w_ref in zip(a_refs[1:], w_refs[1:]):
        acc = acc + jnp.dot(a_ref[...], w_ref[...], preferred_element_type=F32)
    if has_res:
        acc = refs[2 * n_a][...] + acc
    o_ref[...] = acc


def _matmul(a_list, w_list, res=None):
    m = a_list[0].shape[0]
    n = w_list[0].shape[1]
    k_total = sum(a.shape[1] for a in a_list)
    tm = _pick(m, (1024, 512, 256, 128, 64, 32, 16)) if k_total <= 4096 else _pick(m, (512, 256, 128, 64, 32, 16))
    tn = _pick(n, (512, 256, 128)) if k_total <= 4096 else _pick(n, (256, 128))
    in_specs = [pl.BlockSpec((tm, a.shape[1]), lambda i, j: (i, 0)) for a in a_list]
    in_specs += [pl.BlockSpec((w.shape[0], tn), lambda i, j: (0, j)) for w in w_list]
    args = list(a_list) + list(w_list)
    if res is not None:
        in_specs.append(pl.BlockSpec((tm, tn), lambda i, j: (i, j)))
        args.append(res)
    return pl.pallas_call(
        functools.partial(_mm_kernel, n_a=len(a_list), has_res=res is not None),
        out_shape=jax.ShapeDtypeStruct((m, n), F32),
        grid=(m // tm, n // tn),
        in_specs=in_specs,
        out_specs=pl.BlockSpec((tm, tn), lambda i, j: (i, j)),
        compiler_params=_params("parallel", "parallel"),
        name="matmul",
    )(*args)


CAST_ROWS = 512


def _cast_rows(dst_ref, c0, c1, load):
    k = dst_ref.shape[0]
    step = CAST_ROWS if k % CAST_ROWS == 0 else k

    def body(c, carry):
        rows = pl.ds(pl.multiple_of(c * step, step), step)
        dst_ref[rows, c0:c1] = load(rows).astype(BF16)
        return carry

    lax.fori_loop(0, k // step, body, 0)


def _wmm_kernel(*refs, shift, n_plain, has_a2, has_prev, heads_out):
    refs = list(refs)
    a_ref = refs.pop(0)
    a2_ref = refs.pop(0) if has_a2 else None
    wa_ref = refs.pop(0)
    wb_ref = refs.pop(0)
    if has_prev:
        refs.pop(0)
    o_ref = refs.pop(0)
    o2_ref = refs.pop(0) if has_a2 else None
    wbf_ref = refs.pop(0)
    j = pl.program_id(0)
    i = pl.program_id(1)
    tn = wa_ref.shape[1]

    def plain():
        _cast_rows(wbf_ref, 0, tn, lambda rows: wa_ref[rows, :])

    def shifted():
        _cast_rows(wbf_ref, 0, tn,
                   lambda rows: jnp.concatenate([wa_ref[rows, :], wb_ref[rows, :]], axis=1)[:, shift:shift + tn])

    @pl.when(i == 0)
    def _():
        if shift == 0:
            plain()
        else:
            pl.when(j < n_plain)(plain)
            pl.when(j >= n_plain)(shifted)

    acc = jnp.dot(a_ref[...], wbf_ref[...], preferred_element_type=F32)
    if heads_out:
        for hh in range(tn // HEAD_DIM):
            o_ref[hh] = acc[:, hh * HEAD_DIM:(hh + 1) * HEAD_DIM]
    else:
        o_ref[...] = acc
    if has_a2:
        @pl.when(i == pl.num_programs(1) - 1)
        def _():
            o2_ref[...] = jnp.dot(a2_ref[...], wbf_ref[...], preferred_element_type=F32)


def _wmatmul(a, w, l, c0, ncols, tn, plain_cols=0, shift=0, a2=None, heads=None):
    m, k = a.shape
    assert c0 % tn == 0 and ncols % tn == 0 and plain_cols % tn == 0 and tn % HEAD_DIM == 0 and shift < HEAD_DIM
    tm = _pick(m if heads is None else heads[1], (1024, 512, 256, 128, 64, 32, 16))
    jb = c0 // tn
    n_plain = max(0, min(ncols, plain_cols - c0) // tn) if shift else ncols // tn
    lanes_per = tn // HEAD_DIM
    last_blk = -(-w.shape[2] // HEAD_DIM) - 1
    in_specs = [pl.BlockSpec((tm, k), lambda j, i: (i, 0))]
    args = [a]
    if a2 is not None:
        in_specs.append(pl.BlockSpec(a2.shape, lambda j, i: (0, 0)))
        args.append(a2)
    in_specs += [pl.BlockSpec((None, k, tn), lambda j, i: (l, 0, jb + j)),
                 pl.BlockSpec((None, k, HEAD_DIM),
                              lambda j, i: (l, 0, jnp.minimum((jb + j + 1) * lanes_per, last_blk)))]
    args += [w, w]
    aliases = {}
    if heads is None:
        out_shape = [jax.ShapeDtypeStruct((m, ncols), F32)]
        out_specs = [pl.BlockSpec((tm, tn), lambda j, i: (i, j))]
    else:
        bsz, t, prev = heads
        assert t % tm == 0
        tpb = t // tm
        out_shape = [jax.ShapeDtypeStruct((w.shape[0], bsz, ncols // HEAD_DIM, t, HEAD_DIM), F32)]
        out_specs = [pl.BlockSpec((None, None, lanes_per, tm, HEAD_DIM),
                                  lambda j, i: (l, i // tpb, j, i % tpb, 0))]
        if prev is not None:
            in_specs.append(pl.BlockSpec(memory_space=pl.ANY))
            aliases = {len(args): 0}
            args.append(prev)
    if a2 is not None:
        out_shape.append(jax.ShapeDtypeStruct((a2.shape[0], ncols), F32))
        out_specs.append(pl.BlockSpec((a2.shape[0], tn), lambda j, i: (0, j)))
    res = pl.pallas_call(
        functools.partial(_wmm_kernel, shift=shift, n_plain=n_plain, has_a2=a2 is not None,
                          has_prev=heads is not None and heads[2] is not None, heads_out=heads is not None),
        out_shape=out_shape,
        grid=(ncols // tn, m // tm),
        in_specs=in_specs,
        out_specs=out_specs,
        scratch_shapes=[pltpu.VMEM((k, tn), BF16)],
        input_output_aliases=aliases,
        compiler_params=_params("arbitrary", "arbitrary"),
        name="wmatmul",
    )(*args)
    return res[0] if a2 is None else tuple(res)


def _gates_kernel(ps_ref, pf_ref, alog_ref, dtb_ref, bf_ref, gb_ref, lf_ref, c_ref, carry_ref, *, nvh, nch):
    t = pl.program_id(1)
    ps = ps_ref[...]
    tt = ps.shape[0]
    b = ps[:, 0:nvh]
    a = ps[:, nvh:2 * nvh]
    f = pf_ref[:, 2 * nvh:2 * nvh + nch]
    gb_ref[:, 0:nvh] = -jnp.exp(alog_ref[...]) * _softplus(a + dtb_ref[...])
    gb_ref[:, nvh:2 * nvh] = jax.nn.sigmoid(b)
    lf = -_softplus(-(f + bf_ref[...]))
    lf_ref[...] = lf

    @pl.when(t == 0)
    def _():
        carry_ref[...] = jnp.zeros_like(carry_ref)

    ii = lax.broadcasted_iota(jnp.int32, (tt, tt), 0)
    jj = lax.broadcasted_iota(jnp.int32, (tt, tt), 1)
    tri = (ii >= jj).astype(F32)
    c = _hp_dot(tri, lf) + carry_ref[...]
    c_ref[...] = c
    carry_ref[...] = c[tt - 1:tt, :]


def _gates(ps, pf, a_log, dt_bias, b_f, bsz, t):
    nvh = a_log.shape[0]
    nch = b_f.shape[0]
    tt = _pick(t, (256, 128, 64, 32, 16, 8))
    nt = t // tt
    pw = ps.shape[1]
    row = lambda b, i: (b * nt + i, 0)
    fix = lambda b, i: (0, 0)
    return pl.pallas_call(
        functools.partial(_gates_kernel, nvh=nvh, nch=nch),
        out_shape=(jax.ShapeDtypeStruct((bsz * t, 2 * nvh), F32),
                   jax.ShapeDtypeStruct((bsz * t, nch), F32),
                   jax.ShapeDtypeStruct((bsz * t, nch), F32)),
        grid=(bsz, nt),
        in_specs=[pl.BlockSpec((tt, pw), row), pl.BlockSpec((tt, pw), row), pl.BlockSpec((1, nvh), fix),
                  pl.BlockSpec((1, nvh), fix), pl.BlockSpec((1, nch), fix)],
        out_specs=(pl.BlockSpec((tt, 2 * nvh), row), pl.BlockSpec((tt, nch), row), pl.BlockSpec((tt, nch), row)),
        scratch_shapes=[pltpu.VMEM((1, nch), F32)],
        compiler_params=_params("parallel", "arbitrary"),
        name="gates",
    )(ps, pf, a_log.reshape(1, nvh), dt_bias.reshape(1, nvh), b_f.reshape(1, nch))


def _gdn_prep_kernel(x_ref, w_ref, prev_ref, qk_ref, v_ref, cnew_ref, buf_ref, *, tt, kw, width):
    t = pl.program_id(1)
    halo = width - 1
    base = SUBLANES
    lo = base - halo

    @pl.when(t == 0)
    def _():
        buf_ref[lo:base, :] = prev_ref[0]

    @pl.when(t > 0)
    def _():
        buf_ref[lo:base, :] = buf_ref[lo + tt:base + tt, :]

    buf_ref[base:base + tt, :] = x_ref[...]
    chans = x_ref.shape[1]
    for c0 in range(0, chans, HEAD_DIM):
        cs = slice(c0, c0 + HEAD_DIM)
        y = buf_ref[lo:lo + tt, cs] * w_ref[0:1, cs]
        for i in range(1, width):
            y = y + buf_ref[lo + i:lo + i + tt, cs] * w_ref[i:i + 1, cs]
        y = _silu(y)
        if c0 < 2 * kw:
            y = y * lax.rsqrt(jnp.sum(y * y, axis=-1, keepdims=True) + EPS)
            if c0 < kw:
                y = y * HEAD_DIM ** -0.5
            qk_ref[:, cs] = y
        else:
            v_ref[:, c0 - 2 * kw:c0 - 2 * kw + HEAD_DIM] = y

    @pl.when(t == pl.num_programs(1) - 1)
    def _():
        cnew_ref[0] = buf_ref[lo + tt:base + tt, :]


def _gdn_prep(pm, conv_w, prev, bsz, t, kw, vw):
    width, chans = conv_w.shape
    tt = _pick(t, (256, 128, 64))
    nt = t // tt
    row = lambda b, i: (b * nt + i, 0)
    return pl.pallas_call(
        functools.partial(_gdn_prep_kernel, tt=tt, kw=kw, width=width),
        out_shape=(jax.ShapeDtypeStruct((bsz * t, 2 * kw), F32),
                   jax.ShapeDtypeStruct((bsz * t, vw), F32),
                   jax.ShapeDtypeStruct((bsz, width - 1, chans), F32)),
        grid=(bsz, nt),
        in_specs=[pl.BlockSpec((tt, chans), row),
                  pl.BlockSpec((width, chans), lambda b, i: (0, 0)),
                  pl.BlockSpec((1, width - 1, chans), lambda b, i: (b, 0, 0))],
        out_specs=(pl.BlockSpec((tt, 2 * kw), row), pl.BlockSpec((tt, vw), row),
                   pl.BlockSpec((1, width - 1, chans), lambda b, i: (b, 0, 0))),
        scratch_shapes=[pltpu.VMEM((SUBLANES + tt, chans), F32)],
        compiler_params=_params("parallel", "arbitrary"),
        name="gdn_prep",
    )(pm, conv_w, prev)


def _bdot(a, b):
    return jnp.dot(a.astype(BF16), b.astype(BF16), preferred_element_type=F32)


def _gdn_chunk_kernel(qk_ref, v_ref, z_ref, gb_ref, grow_ref, s0_ref, nw_ref, o_ref, sout_ref, s_ref,
                      *, c, nkh, nvh, kw):
    ci = pl.program_id(1)

    @pl.when(ci == 0)
    def _():
        s_ref[...] = s0_ref[0]

    rep = nvh // nkh
    ii = lax.broadcasted_iota(jnp.int32, (c, c), 0)
    jj = lax.broadcasted_iota(jnp.int32, (c, c), 1)
    lower = ii >= jj
    strict = ii > jj
    same_blk = {}
    size = SUBLANES
    while size <= c:
        sh = int(math.log2(size))
        same_blk[size] = (ii >> sh) == (jj >> sh)
        size *= 2
    gb = gb_ref[...]
    grow = grow_ref[0]
    nw = nw_ref[...]
    heads = range(nvh)
    hsl = [slice(h * HEAD_DIM, (h + 1) * HEAD_DIM) for h in heads]
    q = [qk_ref[:, kh * HEAD_DIM:(kh + 1) * HEAD_DIM] for kh in range(nkh)]
    k = [qk_ref[:, kw + kh * HEAD_DIM:kw + (kh + 1) * HEAD_DIM] for kh in range(nkh)]
    v = [v_ref[:, hsl[h]] for h in heads]
    s_old = [s_ref[h] for h in heads]
    k16 = [x.astype(BF16) for x in k]
    kk = [_dot_t(k16[i], k16[i]) for i in range(nkh)]
    qk = [_dot_t(q[i].astype(BF16), k16[i]) for i in range(nkh)]
    beta = [gb[:, nvh + h:nvh + h + 1] for h in heads]
    gc_col = [jnp.sum(jnp.where(lower, grow[h:h + 1, :], 0.0), axis=1, keepdims=True) for h in heads]
    gc_row = [jnp.sum(jnp.where(ii <= jj, gb[:, h:h + 1], 0.0), axis=0, keepdims=True) for h in heads]
    decay = [jnp.exp(jnp.where(lower, gc_col[h] - gc_row[h], -1e30)) for h in heads]
    a = [jnp.where(strict, kk[h // rep] * beta[h] * decay[h], 0.0) for h in heads]
    d = [jnp.where(same_blk[SUBLANES], a[h], 0.0) for h in heads]
    d2 = [_bdot(d[h], d[h]) for h in heads]
    d4 = [_bdot(d2[h], d2[h]) for h in heads]
    x = [d2[h] - d[h] - _bdot(d2[h], d[h]) for h in heads]
    x = [x[h] + d4[h] + _bdot(d4[h], x[h]) for h in heads]
    size = SUBLANES
    while size < c:
        r = [jnp.where(same_blk[2 * size], a[h], 0.0) - jnp.where(same_blk[size], a[h], 0.0) for h in heads]
        y = [r[h] + _bdot(x[h], r[h]) for h in heads]
        x = [x[h] - y[h] - _bdot(y[h], x[h]) for h in heads]
        size *= 2
    egc = [jnp.exp(gc_col[h]) for h in heads]
    rhs = [jnp.concatenate([v[h] * beta[h], k[h // rep] * (beta[h] * egc[h])], axis=1) for h in heads]
    uw = [rhs[h] + _bdot(x[h], rhs[h]) for h in heads]
    s16 = [s_old[h].astype(BF16) for h in heads]
    v_new = [uw[h][:, :HEAD_DIM] - jnp.dot(uw[h][:, HEAD_DIM:].astype(BF16), s16[h], preferred_element_type=F32)
             for h in heads]
    vn16 = [v_new[h].astype(BF16) for h in heads]
    o = [jnp.dot((q[h // rep] * egc[h]).astype(BF16), s16[h], preferred_element_type=F32)
         + jnp.dot((qk[h // rep] * decay[h]).astype(BF16), vn16[h], preferred_element_type=F32) for h in heads]
    g_last = [gc_col[h][c - 1:c, :] for h in heads]
    kd = [(k[h // rep] * jnp.exp(g_last[h] - gc_col[h])).astype(BF16) for h in heads]
    s_new = [s_old[h] * jnp.exp(g_last[h])
             + lax.dot_general(kd[h], vn16[h], (((0,), (0,)), ((), ())), preferred_element_type=F32) for h in heads]
    on = [o[h] * lax.rsqrt(jnp.mean(o[h] * o[h], axis=-1, keepdims=True) + EPS) * nw for h in heads]
    for h in heads:
        s_ref[h] = s_new[h]
        o_ref[:, hsl[h]] = (on[h] * _silu(z_ref[:, hsl[h]])).astype(o_ref.dtype)

    @pl.when(ci == pl.num_programs(1) - 1)
    def _():
        sout_ref[0] = s_ref[...]


def _gdn_chunks(qk, v, pm, z_blk, gb, grow, s0, norm_w, bsz, t, nkh, nvh):
    c = GDN_CHUNK
    n = t // c
    kw = nkh * HEAD_DIM
    vw = nvh * HEAD_DIM
    row = lambda b, i: (b * n + i, 0)
    return pl.pallas_call(
        functools.partial(_gdn_chunk_kernel, c=c, nkh=nkh, nvh=nvh, kw=kw),
        out_shape=(jax.ShapeDtypeStruct((bsz * t, vw), BF16),
                   jax.ShapeDtypeStruct((bsz, nvh, HEAD_DIM, HEAD_DIM), F32)),
        grid=(bsz, n),
        in_specs=[pl.BlockSpec((c, 2 * kw), row),
                  pl.BlockSpec((c, vw), row),
                  pl.BlockSpec((c, vw), lambda b, i: (b * n + i, z_blk)),
                  pl.BlockSpec((c, 2 * nvh), row),
                  pl.BlockSpec((1, nvh, c), lambda b, i: (b * n + i, 0, 0)),
                  pl.BlockSpec((1, nvh, HEAD_DIM, HEAD_DIM), lambda b, i: (b, 0, 0, 0)),
                  pl.BlockSpec((1, HEAD_DIM), lambda b, i: (0, 0))],
        out_specs=(pl.BlockSpec((c, vw), row),
                   pl.BlockSpec((1, nvh, HEAD_DIM, HEAD_DIM), lambda b, i: (b, 0, 0, 0))),
        scratch_shapes=[pltpu.VMEM((nvh, HEAD_DIM, HEAD_DIM), F32)],
        compiler_params=_params("parallel", "arbitrary"),
        name="gdn_chunks",
    )(qk, v, pm, gb, grow, s0, norm_w.reshape(1, HEAD_DIM))


def _conf_kernel(val0_ref, val1_ref, gate0_ref, gate1_ref, w_ref, b_ref, g_ref, beta_ref, prev_ref, o_ref, cnew_ref,
                 buf_ref, *, tt, width, base):
    t = pl.program_id(1)
    halo = width - 1
    lo = base - halo
    half = val0_ref.shape[1]

    @pl.when(t == 0)
    def _():
        buf_ref[lo:base, :] = prev_ref[0]

    @pl.when(t > 0)
    def _():
        buf_ref[lo:base, :] = buf_ref[lo + tt:base + tt, :]

    buf_ref[base:base + tt, 0:half] = val0_ref[...] * jax.nn.sigmoid(gate0_ref[...])
    buf_ref[base:base + tt, half:2 * half] = val1_ref[...] * jax.nn.sigmoid(gate1_ref[...])
    y = buf_ref[lo:lo + tt, :] * w_ref[0:1, :]
    for i in range(1, width):
        y = y + buf_ref[lo + i:lo + i + tt, :] * w_ref[i:i + 1, :]
    y = y + b_ref[...]
    yc = y - jnp.mean(y, axis=-1, keepdims=True)
    yn = yc * lax.rsqrt(jnp.mean(yc * yc, axis=-1, keepdims=True) + EPS)
    o_ref[...] = _silu(yn * g_ref[...] + beta_ref[...]).astype(o_ref.dtype)

    @pl.when(t == pl.num_programs(1) - 1)
    def _():
        cnew_ref[0] = buf_ref[lo + tt:base + tt, :]


def _conformer(pm, val_blk, dw_w, dw_b, ln_g, ln_b, prev, bsz, t):
    width, ch = dw_w.shape
    tt = _pick(t, (128, 64))
    nt = t // tt
    base = -(-(width - 1) // SUBLANES) * SUBLANES
    fix = lambda b, i: (0, 0)
    half = ch // 2
    part = lambda p: pl.BlockSpec((tt, half), lambda b, i: (b * nt + i, val_blk + p))
    return pl.pallas_call(
        functools.partial(_conf_kernel, tt=tt, width=width, base=base),
        out_shape=(jax.ShapeDtypeStruct((bsz * t, ch), BF16),
                   jax.ShapeDtypeStruct((bsz, width - 1, ch), F32)),
        grid=(bsz, nt),
        in_specs=[part(0), part(1), part(2), part(3),
                  pl.BlockSpec((width, ch), fix), pl.BlockSpec((1, ch), fix), pl.BlockSpec((1, ch), fix),
                  pl.BlockSpec((1, ch), fix),
                  pl.BlockSpec((1, width - 1, ch), lambda b, i: (b, 0, 0))],
        out_specs=(pl.BlockSpec((tt, ch), lambda b, i: (b * nt + i, 0)),
                   pl.BlockSpec((1, width - 1, ch), lambda b, i: (b, 0, 0))),
        scratch_shapes=[pltpu.VMEM((base + tt, ch), F32)],
        compiler_params=_params("parallel", "arbitrary"),
        name="conformer",
    )(pm, pm, pm, pm, dw_w, dw_b.reshape(1, ch), ln_g.reshape(1, ch), ln_b.reshape(1, ch), prev)


def _fox_prompt_kernel(q_ref, k_ref, v_ref, ck_ref, o_ref, *, tq):
    qi = pl.program_id(2)
    q = (q_ref[...] * HEAD_DIM ** -0.5).astype(BF16)

    def block(ki, carry, diag):
        m, l, acc = carry
        ks = pl.ds(pl.multiple_of(ki * tq, tq), tq)
        k = k_ref[ks, :].astype(BF16)
        v = v_ref[ks, :].astype(BF16)
        s = _dot_t(q, k) - ck_ref[0, ki]
        if diag:
            ii = lax.broadcasted_iota(jnp.int32, (tq, tq), 0)
            jj = lax.broadcasted_iota(jnp.int32, (tq, tq), 1)
            s = jnp.where(jj <= ii, s, -1e30)
        m_new = jnp.maximum(m, jnp.max(s, axis=-1, keepdims=True))
        alpha = jnp.exp(m - m_new)
        p = jnp.exp(s - m_new)
        l = alpha * l + jnp.sum(p, axis=-1, keepdims=True)
        acc = alpha * acc + jnp.dot(p.astype(BF16), v, preferred_element_type=F32)
        return m_new, l, acc

    init = (jnp.full((tq, 1), -1e30, F32), jnp.zeros((tq, 1), F32), jnp.zeros((tq, HEAD_DIM), F32))
    carry = lax.fori_loop(0, qi, lambda ki, cr: block(ki, cr, False), init)
    m, l, acc = block(qi, carry, True)
    o_ref[...] = (acc / l).astype(o_ref.dtype)


def _fox_prompt(pm, q_blk, kh, vh, l, c_row, bsz, t, nch, tq):
    nq = t // tq
    kv_spec = pl.BlockSpec((None, None, None, t, HEAD_DIM), lambda b, h, i: (l, b, h, 0, 0))
    return pl.pallas_call(
        functools.partial(_fox_prompt_kernel, tq=tq),
        out_shape=jax.ShapeDtypeStruct((bsz * t, nch * HEAD_DIM), BF16),
        grid=(bsz, nch, nq),
        in_specs=[pl.BlockSpec((tq, HEAD_DIM), lambda b, h, i: (b * nq + i, q_blk + h)),
                  kv_spec, kv_spec,
                  pl.BlockSpec((1, nq, 1, tq), lambda b, h, i: (b * nch + h, 0, 0, 0))],
        out_specs=pl.BlockSpec((tq, HEAD_DIM), lambda b, h, i: (b * nq + i, h)),
        compiler_params=_params("parallel", "parallel", "parallel"),
        name="fox_prompt",
    )(pm, kh, vh, c_row)


def _ffn_up_kernel(a_ref, a2_ref, wg_ref, wv_ref, cwg_ref, cwv_ref, sg_ref, sv_ref, act_ref, ng_ref, nv_ref,
                   u2g_ref, u2v_ref, wcat_ref, buf_ref, *, tm, tn, tpb, width, n_sub):
    i = pl.program_id(1)
    halo = width - 1
    base = SUBLANES
    lo = base - halo

    @pl.when(i == 0)
    def _():
        _cast_rows(wcat_ref, 0, tn, lambda rows: wg_ref[rows, :])
        _cast_rows(wcat_ref, tn, 2 * tn, lambda rows: wv_ref[rows, :])

    first = (i % tpb) == 0

    @pl.when(first)
    def _():
        buf_ref[lo:base, 0:tn] = sg_ref[0]
        buf_ref[lo:base, tn:2 * tn] = sv_ref[0]

    @pl.when(jnp.logical_not(first))
    def _():
        buf_ref[lo:base, :] = buf_ref[lo + tm:base + tm, :]

    sub = tm // n_sub

    def conv_rows(r0):
        yg = buf_ref[lo + r0:lo + r0 + sub, 0:tn] * cwg_ref[0:1, :]
        yv = buf_ref[lo + r0:lo + r0 + sub, tn:2 * tn] * cwv_ref[0:1, :]
        for t in range(1, width):
            yg = yg + buf_ref[lo + r0 + t:lo + r0 + t + sub, 0:tn] * cwg_ref[t:t + 1, :]
            yv = yv + buf_ref[lo + r0 + t:lo + r0 + t + sub, tn:2 * tn] * cwv_ref[t:t + 1, :]
        act_ref[r0:r0 + sub, :] = (_silu(yg) * yv).astype(act_ref.dtype)

    for r in range(n_sub):
        buf_ref[base + r * sub:base + (r + 1) * sub, :] = jnp.dot(
            a_ref[r * sub:(r + 1) * sub, :], wcat_ref[...], preferred_element_type=F32)
        if r > 0:
            conv_rows((r - 1) * sub)
    conv_rows((n_sub - 1) * sub)
    ng_ref[0] = buf_ref[lo + tm:base + tm, 0:tn]
    nv_ref[0] = buf_ref[lo + tm:base + tm, tn:2 * tn]

    @pl.when(i == pl.num_programs(1) - 1)
    def _():
        up2 = jnp.dot(a2_ref[...], wcat_ref[...], preferred_element_type=F32)
        u2g_ref[...] = up2[:, 0:tn]
        u2v_ref[...] = up2[:, tn:2 * tn]


def _ffn_up(h, h2, w_up, l, conv_w, state, bsz, t):
    m, d = h.shape
    r2 = h2.shape[0]
    f2 = w_up.shape[2]
    f = f2 // 2
    width = conv_w.shape[0]
    tm = _pick(t, (1024, 512, 256, 128, 64))
    tn = _pick(f, (256, 128))
    nj = f // tn
    tpb = t // tm
    n_sub = 2 if tm >= 512 else 1
    return pl.pallas_call(
        functools.partial(_ffn_up_kernel, tm=tm, tn=tn, tpb=tpb, width=width, n_sub=n_sub),
        out_shape=(jax.ShapeDtypeStruct((m, f), BF16),
                   jax.ShapeDtypeStruct((bsz, width - 1, f), F32),
                   jax.ShapeDtypeStruct((bsz, width - 1, f), F32),
                   jax.ShapeDtypeStruct((r2, f), F32),
                   jax.ShapeDtypeStruct((r2, f), F32)),
        grid=(nj, m // tm),
        in_specs=[pl.BlockSpec((tm, d), lambda j, i: (i, 0)),
                  pl.BlockSpec((r2, d), lambda j, i: (0, 0)),
                  pl.BlockSpec((None, d, tn), lambda j, i: (l, 0, j)),
                  pl.BlockSpec((None, d, tn), lambda j, i: (l, 0, j + nj)),
                  pl.BlockSpec((width, tn), lambda j, i: (0, j)),
                  pl.BlockSpec((width, tn), lambda j, i: (0, j + nj)),
                  pl.BlockSpec((1, width - 1, tn), lambda j, i: (i // tpb, 0, j)),
                  pl.BlockSpec((1, width - 1, tn), lambda j, i: (i // tpb, 0, j + nj))],
        out_specs=(pl.BlockSpec((tm, tn), lambda j, i: (i, j)),
                   pl.BlockSpec((1, width - 1, tn), lambda j, i: (i // tpb, 0, j)),
                   pl.BlockSpec((1, width - 1, tn), lambda j, i: (i // tpb, 0, j)),
                   pl.BlockSpec((r2, tn), lambda j, i: (0, j)),
                   pl.BlockSpec((r2, tn), lambda j, i: (0, j))),
        scratch_shapes=[pltpu.VMEM((d, 2 * tn), BF16), pltpu.VMEM((SUBLANES + tm, 2 * tn), F32)],
        compiler_params=_params("arbitrary", "arbitrary"),
        name="ffn_up",
    )(h, h2, w_up, w_up, conv_w, conv_w, state, state)


def _col_bcast(row):
    n = row.shape[1]
    return jnp.transpose(jnp.broadcast_to(row, (n, n)))


def _sample_mix_kernel(pm_ref, ps_ref, pf_ref, gprev_ref, sprev_ref, cprev_ref, gw_ref, alog_ref, dtb_ref, nw_ref,
                       cw_ref, cb_ref, lg_ref, lb_ref, bf_ref,
                       oa_ref, ob_ref, snew_ref, gnew_ref, cnew_ref, lf_ref,
                       *, nkh, nvh, nch, z_off, glu_off, bch):
    kw = nkh * HEAD_DIM
    vw = nvh * HEAD_DIM
    aqkv = 2 * kw + vw
    rep = nvh // nkh
    gwidth = gw_ref.shape[0]
    x = pm_ref[0, :, 0:aqkv]
    gprev = gprev_ref[0]
    y = jnp.sum(gprev * gw_ref[0:gwidth - 1, :], axis=0, keepdims=True) + x * gw_ref[gwidth - 1:gwidth, :]
    gnew_ref[0, 0:gwidth - 2, :] = gprev[1:gwidth - 1, :]
    gnew_ref[0, gwidth - 2:gwidth - 1, :] = x
    y = _silu(y)
    ps = ps_ref[0]
    beta_all = jax.nn.sigmoid(ps[:, 0:nvh])
    g_all = -jnp.exp(alog_ref[...]) * _softplus(ps[:, nvh:2 * nvh] + dtb_ref[...])
    lf_ref[0] = -_softplus(-(pf_ref[0, :, 2 * nvh:2 * nvh + nch] + bf_ref[...]))
    nw = nw_ref[...]
    for kh in range(nkh):
        q = y[:, kh * HEAD_DIM:(kh + 1) * HEAD_DIM]
        k = y[:, kw + kh * HEAD_DIM:kw + (kh + 1) * HEAD_DIM]
        q = q * lax.rsqrt(jnp.sum(q * q, axis=-1, keepdims=True) + EPS) * HEAD_DIM ** -0.5
        k = k * lax.rsqrt(jnp.sum(k * k, axis=-1, keepdims=True) + EPS)
        q_cols = _col_bcast(q)
        k_cols = _col_bcast(k)
        qk = jnp.sum(q * k, axis=-1, keepdims=True)
        for r in range(rep):
            h = kh * rep + r
            v = y[:, 2 * kw + h * HEAD_DIM:2 * kw + (h + 1) * HEAD_DIM]
            beta = beta_all[:, h:h + 1]
            eg = jnp.exp(g_all[:, h:h + 1])
            s = sprev_ref[0, h]
            k_s = jnp.sum(k_cols * s, axis=0, keepdims=True)
            q_s = jnp.sum(q_cols * s, axis=0, keepdims=True)
            v_new = beta * v - (beta * eg) * k_s
            o = eg * q_s + qk * v_new
            snew_ref[0, h] = s * eg + k_cols * v_new
            on = o * lax.rsqrt(jnp.mean(o * o, axis=-1, keepdims=True) + EPS) * nw
            z = pm_ref[0, :, z_off + h * HEAD_DIM:z_off + (h + 1) * HEAD_DIM]
            oa_ref[0, :, h * HEAD_DIM:(h + 1) * HEAD_DIM] = (on * _silu(z)).astype(oa_ref.dtype)
    cwidth = cw_ref.shape[0]
    val = pm_ref[0, :, glu_off:glu_off + bch]
    gate = pm_ref[0, :, glu_off + bch:glu_off + 2 * bch]
    u = val * jax.nn.sigmoid(gate)
    cprev = cprev_ref[0]
    yc = (jnp.sum(cprev * cw_ref[0:cwidth - 1, :], axis=0, keepdims=True) + u * cw_ref[cwidth - 1:cwidth, :]
          + cb_ref[...])
    cnew_ref[0, 0:cwidth - 2, :] = cprev[1:cwidth - 1, :]
    cnew_ref[0, cwidth - 2:cwidth - 1, :] = u
    yc = yc - jnp.mean(yc, axis=-1, keepdims=True)
    yn = yc * lax.rsqrt(jnp.mean(yc * yc, axis=-1, keepdims=True) + EPS)
    ob_ref[0] = _silu(yn * lg_ref[...] + lb_ref[...]).astype(ob_ref.dtype)


def _sample_mix(pm, ps, pf, gprev, sprev, cprev, gw, a_log, dt_bias, nw, cw, cb, lg, lb, b_f, nkh, z_off, glu_off):
    bsz = gprev.shape[0]
    nvh = a_log.shape[0]
    nch = b_f.shape[0]
    bch = cw.shape[1]
    vw = nvh * HEAD_DIM
    pmw = pm.shape[1]
    psw = ps.shape[1]
    pm3 = pm[:bsz].reshape(bsz, 1, pmw)
    ps3 = ps[:bsz].reshape(bsz, 1, psw)
    pf3 = pf[:bsz].reshape(bsz, 1, psw)
    per_b = lambda *blk: pl.BlockSpec((1,) + blk, lambda b: (b,) + (0,) * len(blk))
    fixed = lambda a: pl.BlockSpec(a.shape, lambda b: (0,) * a.ndim)
    consts = [gw, a_log.reshape(1, nvh), dt_bias.reshape(1, nvh), nw.reshape(1, HEAD_DIM), cw, cb.reshape(1, bch),
              lg.reshape(1, bch), lb.reshape(1, bch), b_f.reshape(1, nch)]
    return pl.pallas_call(
        functools.partial(_sample_mix_kernel, nkh=nkh, nvh=nvh, nch=nch, z_off=z_off, glu_off=glu_off, bch=bch),
        out_shape=(jax.ShapeDtypeStruct((bsz, 1, vw), BF16),
                   jax.ShapeDtypeStruct((bsz, 1, bch), BF16),
                   jax.ShapeDtypeStruct(sprev.shape, F32),
                   jax.ShapeDtypeStruct(gprev.shape, F32),
                   jax.ShapeDtypeStruct(cprev.shape, F32),
                   jax.ShapeDtypeStruct((bsz, 1, nch), F32)),
        grid=(bsz,),
        in_specs=[per_b(1, pmw), per_b(1, psw), per_b(1, psw), per_b(*gprev.shape[1:]), per_b(*sprev.shape[1:]),
                  per_b(*cprev.shape[1:])] + [fixed(a) for a in consts],
        out_specs=(per_b(1, vw), per_b(1, bch), per_b(*sprev.shape[1:]), per_b(*gprev.shape[1:]),
                   per_b(*cprev.shape[1:]), per_b(1, nch)),
        compiler_params=_params("parallel"),
        name="sample_mix",
    )(pm3, ps3, pf3, gprev, sprev, cprev, *consts)


def _fox_sample_kernel(pt_ref, q_ref, kn_ref, vn_ref, lfn_ref, *refs, nch, hp, group):
    kp_refs = refs[0:group]
    vp_refs = refs[group:2 * group]
    lfp_refs = refs[2 * group:3 * group]
    o_ref, qt_ref, m_ref, l_ref, r_ref, acc_ref = refs[3 * group:]
    pi = pl.program_id(1)
    page = kp_refs[0].shape[1]
    scale = HEAD_DIM ** -0.5

    @pl.when(pi == 0)
    def _():
        q = q_ref[0]
        qt_ref[...] = jnp.transpose(q).astype(BF16)
        m_ref[...] = jnp.sum(jnp.transpose(q * kn_ref[0]), axis=0, keepdims=True) * scale
        l_ref[...] = jnp.ones_like(l_ref)
        r_ref[...] = lfn_ref[0]
        acc_ref[...] = vn_ref[0]

    jj = lax.broadcasted_iota(jnp.int32, (page, page), 0)
    mm = lax.broadcasted_iota(jnp.int32, (page, page), 1)
    later = (mm > jj).astype(BF16)
    lanes = lax.broadcasted_iota(jnp.int32, (page, HEAD_DIM), 1)
    slots = range(group)
    lf = [lfp_refs[g][...] for g in slots]
    lf_hi = [x.astype(BF16) for x in lf]
    lf_lo = [(lf[g] - lf_hi[g].astype(F32)).astype(BF16) for g in slots]
    inner = [jnp.dot(later, lf_hi[g], preferred_element_type=F32)
             + jnp.dot(later, lf_lo[g], preferred_element_type=F32) for g in slots]
    total = [jnp.sum(x, axis=0, keepdims=True) for x in lf]
    r_after = [r_ref[...]]
    for g in slots:
        r_after.append(r_after[g] + total[g])
    qt = qt_ref[...]
    s = []
    for g in slots:
        sg = jnp.zeros((page, HEAD_DIM), F32)
        for h in range(nch):
            sg = jnp.where(lanes == h, jnp.dot(kp_refs[g][h].astype(BF16), qt, preferred_element_type=F32), sg)
        s.append(sg * scale + (r_after[g] + inner[g]))
    m_old = m_ref[...]
    m_new = m_old
    for g in slots:
        m_new = jnp.maximum(m_new, jnp.max(s[g], axis=0, keepdims=True))
    alpha = jnp.exp(m_old - m_new)
    p = [jnp.exp(s[g] - m_new) for g in slots]
    l_new = alpha * l_ref[...]
    for g in slots:
        l_new = l_new + jnp.sum(p[g], axis=0, keepdims=True)
    l_ref[...] = l_new
    m_ref[...] = m_new
    r_ref[...] = r_after[group]
    p_t = [jnp.transpose(p[g])[0:hp, :].astype(BF16) for g in slots]
    rows = lax.broadcasted_iota(jnp.int32, (hp, HEAD_DIM), 0)
    upd = jnp.zeros((hp, HEAD_DIM), F32)
    for h in range(nch):
        oh = jnp.dot(p_t[0], vp_refs[0][h].astype(BF16), preferred_element_type=F32)
        for g in range(1, group):
            oh = oh + jnp.dot(p_t[g], vp_refs[g][h].astype(BF16), preferred_element_type=F32)
        upd = jnp.where(rows == h, oh, upd)
    acc_ref[...] = acc_ref[...] * _col_bcast(alpha)[0:hp, :] + upd

    @pl.when(pi == pl.num_programs(1) - 1)
    def _():
        o_ref[0] = (acc_ref[...] / _col_bcast(l_ref[...])[0:hp, :]).astype(o_ref.dtype)


def _fox_sample(q, kn, vn, lfn, cache_k, cache_v, lf_pad, l, page_table, nch):
    bsz = q.shape[0]
    n_pages = page_table.shape[1]
    page = cache_k.shape[3]
    hp = vn.shape[1]
    assert page == HEAD_DIM and cache_k.shape[4] == HEAD_DIM and cache_k.shape[2] == nch
    pt = page_table.reshape(-1)
    group = _pick(n_pages, (FOX_PAGE_GROUP, 2, 1))
    per_b = lambda *blk: pl.BlockSpec((1,) + blk, lambda b, p, pt: (b,) + (0,) * len(blk))

    def paged(g, *blk):
        return pl.BlockSpec((None, None) + blk,
                            lambda b, p, pt: (l, pt[b * n_pages + n_pages - 1 - (p * group + g)]) + (0,) * len(blk))

    grid_spec = pltpu.PrefetchScalarGridSpec(
        num_scalar_prefetch=1,
        grid=(bsz, n_pages // group),
        in_specs=([per_b(HEAD_DIM, HEAD_DIM), per_b(HEAD_DIM, HEAD_DIM), per_b(hp, HEAD_DIM), per_b(1, HEAD_DIM)]
                  + [paged(g, nch, page, HEAD_DIM) for g in range(group)]
                  + [paged(g, nch, page, HEAD_DIM) for g in range(group)]
                  + [paged(g, page, HEAD_DIM) for g in range(group)]),
        out_specs=per_b(hp, HEAD_DIM),
        scratch_shapes=[pltpu.VMEM((HEAD_DIM, HEAD_DIM), BF16), pltpu.VMEM((1, HEAD_DIM), F32),
                        pltpu.VMEM((1, HEAD_DIM), F32), pltpu.VMEM((1, HEAD_DIM), F32),
                        pltpu.VMEM((hp, HEAD_DIM), F32)],
    )
    return pl.pallas_call(
        functools.partial(_fox_sample_kernel, nch=nch, hp=hp, group=group),
        out_shape=jax.ShapeDtypeStruct((bsz, hp, HEAD_DIM), BF16),
        grid_spec=grid_spec,
        compiler_params=_params("parallel", "arbitrary"),
        name="fox_sample",
    )(pt, q, kn, vn, lfn, *([cache_k] * group + [cache_v] * group + [lf_pad] * group))


def _ffn_sample_kernel(upg_ref, upv_ref, sg_ref, sv_ref, cwg_ref, cwv_ref, act_ref, *, width):
    ys = []
    for up_ref, s_ref, cw_ref in ((upg_ref, sg_ref, cwg_ref), (upv_ref, sv_ref, cwv_ref)):
        y = up_ref[...] * cw_ref[width - 1:width, :]
        for i in range(width - 1):
            y = y + s_ref[i] * cw_ref[i:i + 1, :]
        ys.append(y)
    act_ref[...] = (_silu(ys[0]) * ys[1]).astype(act_ref.dtype)


def _ffn_sample(up, state_t, conv_w):
    r, f2 = up.shape
    f = f2 // 2
    width = conv_w.shape[0]
    tn = _pick(f, (1024, 512, 256, 128))
    nj = f // tn
    return pl.pallas_call(
        functools.partial(_ffn_sample_kernel, width=width),
        out_shape=jax.ShapeDtypeStruct((r, f), BF16),
        grid=(nj,),
        in_specs=[pl.BlockSpec((r, tn), lambda j: (0, j)), pl.BlockSpec((r, tn), lambda j: (0, j + nj)),
                  pl.BlockSpec((width - 1, r, tn), lambda j: (0, 0, j)),
                  pl.BlockSpec((width - 1, r, tn), lambda j: (0, 0, j + nj)),
                  pl.BlockSpec((width, tn), lambda j: (0, j)), pl.BlockSpec((width, tn), lambda j: (0, j + nj))],
        out_specs=pl.BlockSpec((r, tn), lambda j: (0, j)),
        compiler_params=_params("parallel"),
        name="ffn_sample",
    )(up, up, state_t, state_t, conv_w, conv_w)


def _in_proj(h, h2, w_in, l, o_b, shift, glu_w, cw, tn, bsz, t, k_prev, v_prev):
    main = _wmatmul(h, w_in, l, 0, o_b + glu_w + cw, tn, o_b, shift, a2=h2)
    k = _wmatmul(h, w_in, l, o_b + glu_w + cw, cw, tn, o_b, shift, a2=h2, heads=(bsz, t, k_prev))
    v = _wmatmul(h, w_in, l, o_b + glu_w + 2 * cw, cw, tn, o_b, shift, a2=h2, heads=(bsz, t, v_prev))
    ps = _wmatmul(h, w_in, l, o_b, HEAD_DIM, HEAD_DIM, a2=h2)
    pf = _wmatmul(h, w_in, l, o_b + glu_w + 3 * cw, HEAD_DIM, HEAD_DIM, a2=h2)
    return main, k, v, ps, pf


def kernel(x_prompt, x_sample, cache_k, cache_v, cache_logf, page_table, state_gdn, state_gdn_conv, state_conf_conv, state_ffn_conv, norm_mix, w_in, gdn_conv_w, gdn_a_log, gdn_dt_bias, gdn_norm_w, conf_dw_w, conf_dw_b, conf_ln_g, conf_ln_b, fox_b_f, w_out, norm_ffn, ffn_conv_w, w_up, w_down, norm_final):
    bp, t, d = x_prompt.shape
    bs, ts, _ = x_sample.shape
    assert ts == 1, "the sample step handles one new token per sequence"
    depth = w_in.shape[0]
    nvh = gdn_a_log.shape[1]
    aqkv = gdn_conv_w.shape[2]
    vw = nvh * HEAD_DIM
    kw = (aqkv - vw) // 2
    nkh = kw // HEAD_DIM
    nch = fox_b_f.shape[1]
    cw = nch * HEAD_DIM
    bch = conf_dw_w.shape[2]
    f = w_down.shape[1]
    o_b = aqkv + vw
    shift = 2 * nvh
    glu_w = 2 * bch
    tn_in = _pick(o_b, (512, 256, 128))
    assert (t % GDN_CHUNK == 0 and cw == vw and aqkv % vw == 0 and o_b % (bch // 2) == 0
            and glu_w % tn_in == 0 and cw % tn_in == 0)
    z_blk = aqkv // vw
    glu_off = o_b
    q_off = o_b + glu_w
    tq = _pick(t, (512, 256, 128))
    nq = t // tq
    n_chunks = t // GDN_CHUNK
    hp = -(-nch // SUBLANES) * SUBLANES

    xp = x_prompt.reshape(bp * t, d)
    rs = -(-bs // ROW_PAD) * ROW_PAD
    xs = jnp.zeros((rs, d), F32).at[:bs].set(x_sample.reshape(bs, d))
    lf_pad = jnp.pad(cache_logf, ((0, 0), (0, 0), (0, 0), (0, HEAD_DIM - nch)))
    cache_kh = cache_k.transpose(0, 1, 3, 2, 4)
    cache_vh = cache_v.transpose(0, 1, 3, 2, 4)
    k_all = v_all = None
    outs_p, outs_s = [], []
    for l in range(depth):
        wo = w_out[l]
        w_o = (wo[0:vw].astype(BF16), wo[vw:vw + bch].astype(BF16), wo[vw + bch:].astype(BF16))
        w_d = w_down[l].astype(BF16)

        h = _rmsnorm(xp, norm_mix[l], BF16)
        hs = _rmsnorm(xs, norm_mix[l], BF16)
        (pm, pm_s), (k_all, kc_s), (v_all, vc_s), (ps, ps_s), (pf, pf_s) = _in_proj(
            h, hs, w_in, l, o_b, shift, glu_w, cw, tn_in, bp, t, k_all, v_all)

        gb, logf, ccum = _gates(ps, pf, gdn_a_log[l], gdn_dt_bias[l], fox_b_f[l], bp, t)
        qk, vact, gconv_new = _gdn_prep(pm, gdn_conv_w[l], jnp.zeros((bp, gdn_conv_w.shape[1] - 1, aqkv), F32),
                                        bp, t, kw, vw)
        grow = gb[:, :nvh].reshape(bp * n_chunks, GDN_CHUNK, nvh).transpose(0, 2, 1)
        o_a, s_new = _gdn_chunks(qk, vact, pm, z_blk, gb, grow, jnp.zeros((bp, nvh, HEAD_DIM, HEAD_DIM), F32),
                                 gdn_norm_w[l], bp, t, nkh, nvh)
        o_bm, cconv_new = _conformer(pm, glu_off // (bch // 2), conf_dw_w[l], conf_dw_b[l], conf_ln_g[l],
                                     conf_ln_b[l], jnp.zeros((bp, conf_dw_w.shape[1] - 1, bch), F32), bp, t)
        c_t = ccum.reshape(bp, t, nch).transpose(0, 2, 1)
        o_c = _fox_prompt(pm, q_off // HEAD_DIM, k_all, v_all, l, c_t.reshape(bp * nch, nq, 1, tq), bp, t, nch, tq)
        xp = _matmul([o_a, o_bm, o_c], list(w_o), res=xp)
        outs_p.append((logf.reshape(bp, t, nch), s_new, gconv_new, cconv_new))

        o_a, o_bm, s_new, gconv_new, cconv_new, logf = _sample_mix(
            pm_s, ps_s, pf_s, state_gdn_conv[l], state_gdn[l], state_conf_conv[l], gdn_conv_w[l], gdn_a_log[l],
            gdn_dt_bias[l], gdn_norm_w[l], conf_dw_w[l], conf_dw_b[l], conf_ln_g[l], conf_ln_b[l], fox_b_f[l],
            nkh, aqkv, glu_off)
        head_rows = lambda a, rows: jnp.zeros((bs, rows, HEAD_DIM), F32).at[:, :nch].set(a.reshape(bs, nch, HEAD_DIM))
        lfn = jnp.zeros((bs, 1, HEAD_DIM), F32).at[:, :, :nch].set(logf)
        o_c = _fox_sample(head_rows(pm_s[:bs, q_off:q_off + cw], HEAD_DIM), head_rows(kc_s[:bs], HEAD_DIM),
                          head_rows(vc_s[:bs], hp), lfn, cache_kh, cache_vh, lf_pad, l, page_table, nch)
        o_c = o_c[:, :nch].reshape(bs, cw)
        pad_rows = lambda a: jnp.zeros((rs, a.shape[-1]), a.dtype).at[:bs].set(a.reshape(bs, -1))
        xs = _matmul([pad_rows(o_a), pad_rows(o_bm), pad_rows(o_c)], list(w_o), res=xs)

        h = _rmsnorm(xp, norm_ffn[l], BF16)
        hs = _rmsnorm(xs, norm_ffn[l], BF16)
        act, fnew_g, fnew_v, up_g, up_v = _ffn_up(h, hs, w_up, l, ffn_conv_w[l],
                                                  jnp.zeros((bp, ffn_conv_w.shape[1] - 1, 2 * f), F32), bp, t)
        xp = _matmul([act], [w_d], res=xp)
        up = jnp.concatenate([up_g, up_v], axis=1)
        st = jnp.zeros((ffn_conv_w.shape[1] - 1, rs, 2 * f), F32).at[:, :bs].set(state_ffn_conv[l].transpose(1, 0, 2))
        act = _ffn_sample(up, st, ffn_conv_w[l])
        xs = _matmul([act], [w_d], res=xs)
        outs_p[-1] += (jnp.concatenate([fnew_g, fnew_v], axis=-1),)
        fconv_new = jnp.concatenate([state_ffn_conv[l][:, 1:], up[:bs, None, :]], axis=1)
        outs_s.append((kc_s[:bs].reshape(bs, 1, nch, HEAD_DIM), vc_s[:bs].reshape(bs, 1, nch, HEAD_DIM), logf,
                       s_new, gconv_new, cconv_new, fconv_new))

    y_prompt = _rmsnorm(xp, norm_final, F32).reshape(bp, t, d)
    y_sample = _rmsnorm(xs, norm_final, F32)[:bs].reshape(bs, 1, d)
    stack = lambda outs, i: jnp.stack([o[i] for o in outs], axis=0)
    k_rows_p = k_all.transpose(0, 1, 3, 2, 4)
    v_rows_p = v_all.transpose(0, 1, 3, 2, 4)
    return ((y_prompt, y_sample, k_rows_p, v_rows_p) + tuple(stack(outs_p, i) for i in range(5))
            + tuple(stack(outs_s, i) for i in range(7)))
```

```python
import functools
import math

import jax
import jax.numpy as jnp
from jax import lax
from jax.experimental import pallas as pl
from jax.experimental.pallas import tpu as pltpu

EPS = 1e-6
HEAD_DIM = 128
GDN_CHUNK = 64
SUBLANES = 8
ROW_PAD = 16
VMEM_LIMIT = 56 * 2**20
FOX_PAGE_GROUP = 4
FFN_SUB_ROWS = 512

F32 = jnp.float32
BF16 = jnp.bfloat16
_HP = lax.Precision.HIGHEST


def _params(*sem):
    return pltpu.CompilerParams(dimension_semantics=sem, vmem_limit_bytes=VMEM_LIMIT)


def _pick(n, cands):
    for c in cands:
        if n % c == 0:
            return c
    return n


def _silu(x):
    return x * jax.nn.sigmoid(x)


def _softplus(x):
    return jnp.maximum(x, 0.0) + jnp.log(1.0 + jnp.exp(-jnp.abs(x)))


def _dot_t(a, b):
    return lax.dot_general(a, b, (((1,), (1,)), ((), ())), preferred_element_type=F32)


def _hp_dot(a, b):
    return jnp.dot(a, b, precision=_HP, preferred_element_type=F32)


def _rmsnorm_kernel(x_ref, g_ref, o_ref):
    x = x_ref[...]
    ms = jnp.mean(x * x, axis=-1, keepdims=True)
    o_ref[...] = (x * lax.rsqrt(ms + EPS) * g_ref[...]).astype(o_ref.dtype)


def _rmsnorm(x, g, out_dtype):
    m, d = x.shape
    tm = _pick(m, (256, 128, 64, 32, 16, 8))
    return pl.pallas_call(
        _rmsnorm_kernel,
        out_shape=jax.ShapeDtypeStruct((m, d), out_dtype),
        grid=(m // tm,),
        in_specs=[pl.BlockSpec((tm, d), lambda i: (i, 0)), pl.BlockSpec((1, d), lambda i: (0, 0))],
        out_specs=pl.BlockSpec((tm, d), lambda i: (i, 0)),
        compiler_params=_params("parallel"),
        name="rmsnorm",
    )(x, g.reshape(1, d))


def _mm_kernel(*refs, n_a, has_res):
    a_refs = refs[:n_a]
    w_refs = refs[n_a:2 * n_a]
    o_ref = refs[-1]
    acc = jnp.dot(a_refs[0][...], w_refs[0][...], preferred_element_type=F32)
    for a_ref, w_ref in zip(a_refs[1:], w_refs[1:]):
        acc = acc + jnp.dot(a_ref[...], w_ref[...], preferred_element_type=F32)
    if has_res:
        acc = refs[2 * n_a][...] + acc
    o_ref[...] = acc


def _matmul(a_list, w_list, res=None):
    m = a_list[0].shape[0]
    n = w_list[0].shape[1]
    k_total = sum(a.shape[1] for a in a_list)
    tm = _pick(m, (1024, 512, 256, 128, 64, 32, 16)) if k_total <= 4096 else _pick(m, (512, 256, 128, 64, 32, 16))
    tn = _pick(n, (512, 256, 128)) if k_total <= 4096 else _pick(n, (256, 128))
    in_specs = [pl.BlockSpec((tm, a.shape[1]), lambda i, j: (i, 0)) for a in a_list]
    in_specs += [pl.BlockSpec((w.shape[0], tn), lambda i, j: (0, j)) for w in w_list]
    args = list(a_list) + list(w_list)
    if res is not None:
        in_specs.append(pl.BlockSpec((tm, tn), lambda i, j: (i, j)))
        args.append(res)
    return pl.pallas_call(
        functools.partial(_mm_kernel, n_a=len(a_list), has_res=res is not None),
        out_shape=jax.ShapeDtypeStruct((m, n), F32),
        grid=(m // tm, n // tn),
        in_specs=in_specs,
        out_specs=pl.BlockSpec((tm, tn), lambda i, j: (i, j)),
        compiler_params=_params("parallel", "parallel"),
        name="matmul",
    )(*args)


CAST_ROWS = 512


def _cast_rows(dst_ref, c0, c1, load):
    k = dst_ref.shape[0]
    step = CAST_ROWS if k % CAST_ROWS == 0 else k

    def body(c, carry):
        rows = pl.ds(pl.multiple_of(c * step, step), step)
        dst_ref[rows, c0:c1] = load(rows).astype(BF16)
        return carry

    lax.fori_loop(0, k // step, body, 0)


def _wmm_kernel(*refs, shift, n_plain, has_a2, has_prev, heads_out):
    refs = list(refs)
    a_ref = refs.pop(0)
    a2_ref = refs.pop(0) if has_a2 else None
    wa_ref = refs.pop(0)
    wb_ref = refs.pop(0)
    if has_prev:
        refs.pop(0)
    o_ref = refs.pop(0)
    o2_ref = refs.pop(0) if has_a2 else None
    wbf_ref = refs.pop(0)
    j = pl.program_id(0)
    i = pl.program_id(1)
    tn = wa_ref.shape[1]

    def plain():
        _cast_rows(wbf_ref, 0, tn, lambda rows: wa_ref[rows, :])

    def shifted():
        _cast_rows(wbf_ref, 0, tn,
                   lambda rows: jnp.concatenate([wa_ref[rows, :], wb_ref[rows, :]], axis=1)[:, shift:shift + tn])

    @pl.when(i == 0)
    def _():
        if shift == 0:
            plain()
        else:
            pl.when(j < n_plain)(plain)
            pl.when(j >= n_plain)(shifted)

    acc = jnp.dot(a_ref[...], wbf_ref[...], preferred_element_type=F32)
    if heads_out:
        for hh in range(tn // HEAD_DIM):
            o_ref[hh] = acc[:, hh * HEAD_DIM:(hh + 1) * HEAD_DIM]
    else:
        o_ref[...] = acc
    if has_a2:
        @pl.when(i == pl.num_programs(1) - 1)
        def _():
            o2_ref[...] = jnp.dot(a2_ref[...], wbf_ref[...], preferred_element_type=F32)


def _wmatmul(a, w, l, c0, ncols, tn, plain_cols=0, shift=0, a2=None, heads=None):
    m, k = a.shape
    assert c0 % tn == 0 and ncols % tn == 0 and plain_cols % tn == 0 and tn % HEAD_DIM == 0 and shift < HEAD_DIM
    tm = _pick(m if heads is None else heads[1], (1024, 512, 256, 128, 64, 32, 16))
    jb = c0 // tn
    n_plain = max(0, min(ncols, plain_cols - c0) // tn) if shift else ncols // tn
    lanes_per = tn // HEAD_DIM
    last_blk = -(-w.shape[2] // HEAD_DIM) - 1
    in_specs = [pl.BlockSpec((tm, k), lambda j, i: (i, 0))]
    args = [a]
    if a2 is not None:
        in_specs.append(pl.BlockSpec(a2.shape, lambda j, i: (0, 0)))
        args.append(a2)
    in_specs += [pl.BlockSpec((None, k, tn), lambda j, i: (l, 0, jb + j)),
                 pl.BlockSpec((None, k, HEAD_DIM),
                              lambda j, i: (l, 0, jnp.minimum((jb + j + 1) * lanes_per, last_blk)))]
    args += [w, w]
    aliases = {}
    if heads is None:
        out_shape = [jax.ShapeDtypeStruct((m, ncols), F32)]
        out_specs = [pl.BlockSpec((tm, tn), lambda j, i: (i, j))]
    else:
        bsz, t, prev = heads
        assert t % tm == 0
        tpb = t // tm
        out_shape = [jax.ShapeDtypeStruct((w.shape[0], bsz, ncols // HEAD_DIM, t, HEAD_DIM), F32)]
        out_specs = [pl.BlockSpec((None, None, lanes_per, tm, HEAD_DIM),
                                  lambda j, i: (l, i // tpb, j, i % tpb, 0))]
        if prev is not None:
            in_specs.append(pl.BlockSpec(memory_space=pl.ANY))
            aliases = {len(args): 0}
            args.append(prev)
    if a2 is not None:
        out_shape.append(jax.ShapeDtypeStruct((a2.shape[0], ncols), F32))
        out_specs.append(pl.BlockSpec((a2.shape[0], tn), lambda j, i: (0, j)))
    res = pl.pallas_call(
        functools.partial(_wmm_kernel, shift=shift, n_plain=n_plain, has_a2=a2 is not None,
                          has_prev=heads is not None and heads[2] is not None, heads_out=heads is not None),
        out_shape=out_shape,
        grid=(ncols // tn, m // tm),
        in_specs=in_specs,
        out_specs=out_specs,
        scratch_shapes=[pltpu.VMEM((k, tn), BF16)],
        input_output_aliases=aliases,
        compiler_params=_params("arbitrary", "arbitrary"),
        name="wmatmul",
    )(*args)
    return res[0] if a2 is None else tuple(res)


def _gates_kernel(ps_ref, pf_ref, alog_ref, dtb_ref, bf_ref, gb_ref, lf_ref, c_ref, carry_ref, *, nvh, nch):
    t = pl.program_id(1)
    ps = ps_ref[...]
    tt = ps.shape[0]
    b = ps[:, 0:nvh]
    a = ps[:, nvh:2 * nvh]
    f = pf_ref[:, 2 * nvh:2 * nvh + nch]
    gb_ref[:, 0:nvh] = -jnp.exp(alog_ref[...]) * _softplus(a + dtb_ref[...])
    gb_ref[:, nvh:2 * nvh] = jax.nn.sigmoid(b)
    lf = -_softplus(-(f + bf_ref[...]))
    lf_ref[...] = lf

    @pl.when(t == 0)
    def _():
        carry_ref[...] = jnp.zeros_like(carry_ref)

    ii = lax.broadcasted_iota(jnp.int32, (tt, tt), 0)
    jj = lax.broadcasted_iota(jnp.int32, (tt, tt), 1)
    tri = (ii >= jj).astype(F32)
    c = _hp_dot(tri, lf) + carry_ref[...]
    c_ref[...] = c
    carry_ref[...] = c[tt - 1:tt, :]


def _gates(ps, pf, a_log, dt_bias, b_f, bsz, t):
    nvh = a_log.shape[0]
    nch = b_f.shape[0]
    tt = _pick(t, (256, 128, 64, 32, 16, 8))
    nt = t // tt
    pw = ps.shape[1]
    row = lambda b, i: (b * nt + i, 0)
    fix = lambda b, i: (0, 0)
    return pl.pallas_call(
        functools.partial(_gates_kernel, nvh=nvh, nch=nch),
        out_shape=(jax.ShapeDtypeStruct((bsz * t, 2 * nvh), F32),
                   jax.ShapeDtypeStruct((bsz * t, nch), F32),
                   jax.ShapeDtypeStruct((bsz * t, nch), F32)),
        grid=(bsz, nt),
        in_specs=[pl.BlockSpec((tt, pw), row), pl.BlockSpec((tt, pw), row), pl.BlockSpec((1, nvh), fix),
                  pl.BlockSpec((1, nvh), fix), pl.BlockSpec((1, nch), fix)],
        out_specs=(pl.BlockSpec((tt, 2 * nvh), row), pl.BlockSpec((tt, nch), row), pl.BlockSpec((tt, nch), row)),
        scratch_shapes=[pltpu.VMEM((1, nch), F32)],
        compiler_params=_params("parallel", "arbitrary"),
        name="gates",
    )(ps, pf, a_log.reshape(1, nvh), dt_bias.reshape(1, nvh), b_f.reshape(1, nch))


def _gdn_prep_kernel(x_ref, w_ref, prev_ref, qk_ref, v_ref, cnew_ref, buf_ref, *, tt, kw, width):
    t = pl.program_id(1)
    halo = width - 1
    base = SUBLANES
    lo = base - halo

    @pl.when(t == 0)
    def _():
        buf_ref[lo:base, :] = prev_ref[0]

    @pl.when(t > 0)
    def _():
        buf_ref[lo:base, :] = buf_ref[lo + tt:base + tt, :]

    buf_ref[base:base + tt, :] = x_ref[...]
    chans = x_ref.shape[1]
    for c0 in range(0, chans, HEAD_DIM):
        cs = slice(c0, c0 + HEAD_DIM)
        y = buf_ref[lo:lo + tt, cs] * w_ref[0:1, cs]
        for i in range(1, width):
            y = y + buf_ref[lo + i:lo + i + tt, cs] * w_ref[i:i + 1, cs]
        y = _silu(y)
        if c0 < 2 * kw:
            y = y * lax.rsqrt(jnp.sum(y * y, axis=-1, keepdims=True) + EPS)
            if c0 < kw:
                y = y * HEAD_DIM ** -0.5
            qk_ref[:, cs] = y
        else:
            v_ref[:, c0 - 2 * kw:c0 - 2 * kw + HEAD_DIM] = y

    @pl.when(t == pl.num_programs(1) - 1)
    def _():
        cnew_ref[0] = buf_ref[lo + tt:base + tt, :]


def _gdn_prep(pm, conv_w, prev, bsz, t, kw, vw):
    width, chans = conv_w.shape
    tt = _pick(t, (256, 128, 64))
    nt = t // tt
    row = lambda b, i: (b * nt + i, 0)
    return pl.pallas_call(
        functools.partial(_gdn_prep_kernel, tt=tt, kw=kw, width=width),
        out_shape=(jax.ShapeDtypeStruct((bsz * t, 2 * kw), F32),
                   jax.ShapeDtypeStruct((bsz * t, vw), F32),
                   jax.ShapeDtypeStruct((bsz, width - 1, chans), F32)),
        grid=(bsz, nt),
        in_specs=[pl.BlockSpec((tt, chans), row),
                  pl.BlockSpec((width, chans), lambda b, i: (0, 0)),
                  pl.BlockSpec((1, width - 1, chans), lambda b, i: (b, 0, 0))],
        out_specs=(pl.BlockSpec((tt, 2 * kw), row), pl.BlockSpec((tt, vw), row),
                   pl.BlockSpec((1, width - 1, chans), lambda b, i: (b, 0, 0))),
        scratch_shapes=[pltpu.VMEM((SUBLANES + tt, chans), F32)],
        compiler_params=_params("parallel", "arbitrary"),
        name="gdn_prep",
    )(pm, conv_w, prev)


def _bdot(a, b):
    return jnp.dot(a.astype(BF16), b.astype(BF16), preferred_element_type=F32)


def _gdn_chunk_kernel(qk_ref, v_ref, z_ref, gb_ref, grow_ref, s0_ref, nw_ref, o_ref, sout_ref, s_ref,
                      *, c, nkh, nvh, kw):
    ci = pl.program_id(1)

    @pl.when(ci == 0)
    def _():
        s_ref[...] = s0_ref[0]

    rep = nvh // nkh
    ii = lax.broadcasted_iota(jnp.int32, (c, c), 0)
    jj = lax.broadcasted_iota(jnp.int32, (c, c), 1)
    lower = ii >= jj
    strict = ii > jj
    same_blk = {}
    size = SUBLANES
    while size <= c:
        sh = int(math.log2(size))
        same_blk[size] = (ii >> sh) == (jj >> sh)
        size *= 2
    gb = gb_ref[...]
    grow = grow_ref[0]
    nw = nw_ref[...]
    heads = range(nvh)
    hsl = [slice(h * HEAD_DIM, (h + 1) * HEAD_DIM) for h in heads]
    q = [qk_ref[:, kh * HEAD_DIM:(kh + 1) * HEAD_DIM] for kh in range(nkh)]
    k = [qk_ref[:, kw + kh * HEAD_DIM:kw + (kh + 1) * HEAD_DIM] for kh in range(nkh)]
    v = [v_ref[:, hsl[h]] for h in heads]
    s_old = [s_ref[h] for h in heads]
    k16 = [x.astype(BF16) for x in k]
    kk = [_dot_t(k16[i], k16[i]) for i in range(nkh)]
    qk = [_dot_t(q[i].astype(BF16), k16[i]) for i in range(nkh)]
    beta = [gb[:, nvh + h:nvh + h + 1] for h in heads]
    gc_col = [jnp.sum(jnp.where(lower, grow[h:h + 1, :], 0.0), axis=1, keepdims=True) for h in heads]
    gc_row = [jnp.sum(jnp.where(ii <= jj, gb[:, h:h + 1], 0.0), axis=0, keepdims=True) for h in heads]
    decay = [jnp.exp(jnp.where(lower, gc_col[h] - gc_row[h], -1e30)) for h in heads]
    a = [jnp.where(strict, kk[h // rep] * beta[h] * decay[h], 0.0) for h in heads]
    d = [jnp.where(same_blk[SUBLANES], a[h], 0.0) for h in heads]
    d2 = [_bdot(d[h], d[h]) for h in heads]
    d4 = [_bdot(d2[h], d2[h]) for h in heads]
    x = [d2[h] - d[h] - _bdot(d2[h], d[h]) for h in heads]
    x = [x[h] + d4[h] + _bdot(d4[h], x[h]) for h in heads]
    size = SUBLANES
    while size < c:
        r = [jnp.where(same_blk[2 * size], a[h], 0.0) - jnp.where(same_blk[size], a[h], 0.0) for h in heads]
        y = [r[h] + _bdot(x[h], r[h]) for h in heads]
        x = [x[h] - y[h] - _bdot(y[h], x[h]) for h in heads]
        size *= 2
    egc = [jnp.exp(gc_col[h]) for h in heads]
    rhs = [jnp.concatenate([v[h] * beta[h], k[h // rep] * (beta[h] * egc[h])], axis=1) for h in heads]
    uw = [rhs[h] + _bdot(x[h], rhs[h]) for h in heads]
    s16 = [s_old[h].astype(BF16) for h in heads]
    v_new = [uw[h][:, :HEAD_DIM] - jnp.dot(uw[h][:, HEAD_DIM:].astype(BF16), s16[h], preferred_element_type=F32)
             for h in heads]
    vn16 = [v_new[h].astype(BF16) for h in heads]
    o = [jnp.dot((q[h // rep] * egc[h]).astype(BF16), s16[h], preferred_element_type=F32)
         + jnp.dot((qk[h // rep] * decay[h]).astype(BF16), vn16[h], preferred_element_type=F32) for h in heads]
    g_last = [gc_col[h][c - 1:c, :] for h in heads]
    kd = [(k[h // rep] * jnp.exp(g_last[h] - gc_col[h])).astype(BF16) for h in heads]
    s_new = [s_old[h] * jnp.exp(g_last[h])
             + lax.dot_general(kd[h], vn16[h], (((0,), (0,)), ((), ())), preferred_element_type=F32) for h in heads]
    on = [o[h] * lax.rsqrt(jnp.mean(o[h] * o[h], axis=-1, keepdims=True) + EPS) * nw for h in heads]
    for h in heads:
        s_ref[h] = s_new[h]
        o_ref[:, hsl[h]] = (on[h] * _silu(z_ref[:, hsl[h]])).astype(o_ref.dtype)

    @pl.when(ci == pl.num_programs(1) - 1)
    def _():
        sout_ref[0] = s_ref[...]


def _gdn_chunks(qk, v, pm, z_blk, gb, grow, s0, norm_w, bsz, t, nkh, nvh):
    c = GDN_CHUNK
    n = t // c
    kw = nkh * HEAD_DIM
    vw = nvh * HEAD_DIM
    row = lambda b, i: (b * n + i, 0)
    return pl.pallas_call(
        functools.partial(_gdn_chunk_kernel, c=c, nkh=nkh, nvh=nvh, kw=kw),
        out_shape=(jax.ShapeDtypeStruct((bsz * t, vw), BF16),
                   jax.ShapeDtypeStruct((bsz, nvh, HEAD_DIM, HEAD_DIM), F32)),
        grid=(bsz, n),
        in_specs=[pl.BlockSpec((c, 2 * kw), row),
                  pl.BlockSpec((c, vw), row),
                  pl.BlockSpec((c, vw), lambda b, i: (b * n + i, z_blk)),
                  pl.BlockSpec((c, 2 * nvh), row),
                  pl.BlockSpec((1, nvh, c), lambda b, i: (b * n + i, 0, 0)),
                  pl.BlockSpec((1, nvh, HEAD_DIM, HEAD_DIM), lambda b, i: (b, 0, 0, 0)),
                  pl.BlockSpec((1, HEAD_DIM), lambda b, i: (0, 0))],
        out_specs=(pl.BlockSpec((c, vw), row),
                   pl.BlockSpec((1, nvh, HEAD_DIM, HEAD_DIM), lambda b, i: (b, 0, 0, 0))),
        scratch_shapes=[pltpu.VMEM((nvh, HEAD_DIM, HEAD_DIM), F32)],
        compiler_params=_params("parallel", "arbitrary"),
        name="gdn_chunks",
    )(qk, v, pm, gb, grow, s0, norm_w.reshape(1, HEAD_DIM))


def _conf_kernel(val0_ref, val1_ref, gate0_ref, gate1_ref, w_ref, b_ref, g_ref, beta_ref, prev_ref, o_ref, cnew_ref,
                 buf_ref, y_ref, *, tt, width, base):
    t = pl.program_id(1)
    halo = width - 1
    lo = base - halo
    half = val0_ref.shape[1]
    ch = 2 * half

    @pl.when(t == 0)
    def _():
        buf_ref[lo:base, :] = prev_ref[0]
        buf_ref[base + tt:base + tt + SUBLANES, :] = jnp.zeros((SUBLANES, ch), F32)

    @pl.when(t > 0)
    def _():
        buf_ref[lo:base, :] = buf_ref[lo + tt:base + tt, :]

    buf_ref[base:base + tt, 0:half] = val0_ref[...] * jax.nn.sigmoid(gate0_ref[...])
    buf_ref[base:base + tt, half:2 * half] = val1_ref[...] * jax.nn.sigmoid(gate1_ref[...])
    cchunk = 2 * HEAD_DIM if ch % (2 * HEAD_DIM) == 0 else ch
    for c0 in range(0, ch, cchunk):
        cs = slice(c0, c0 + cchunk)
        y = None
        for r in range(SUBLANES):
            z = None
            for i in range(width):
                if (lo + i) % SUBLANES != r:
                    continue
                a0 = (lo + i) - r
                term = buf_ref[a0:a0 + tt + SUBLANES, cs] * w_ref[i:i + 1, cs]
                z = term if z is None else z + term
            if z is not None:
                zr = z[r:r + tt, :]
                y = zr if y is None else y + zr
        y_ref[:, cs] = y
    y = y_ref[...] + b_ref[...]
    yc = y - jnp.mean(y, axis=-1, keepdims=True)
    yn = yc * lax.rsqrt(jnp.mean(yc * yc, axis=-1, keepdims=True) + EPS)
    o_ref[...] = _silu(yn * g_ref[...] + beta_ref[...]).astype(o_ref.dtype)

    @pl.when(t == pl.num_programs(1) - 1)
    def _():
        cnew_ref[0] = buf_ref[lo + tt:base + tt, :]


def _conformer(pm, val_blk, dw_w, dw_b, ln_g, ln_b, prev, bsz, t):
    width, ch = dw_w.shape
    tt = _pick(t, (128, 64))
    nt = t // tt
    base = -(-(width - 1) // SUBLANES) * SUBLANES
    fix = lambda b, i: (0, 0)
    half = ch // 2
    part = lambda p: pl.BlockSpec((tt, half), lambda b, i: (b * nt + i, val_blk + p))
    return pl.pallas_call(
        functools.partial(_conf_kernel, tt=tt, width=width, base=base),
        out_shape=(jax.ShapeDtypeStruct((bsz * t, ch), BF16),
                   jax.ShapeDtypeStruct((bsz, width - 1, ch), F32)),
        grid=(bsz, nt),
        in_specs=[part(0), part(1), part(2), part(3),
                  pl.BlockSpec((width, ch), fix), pl.BlockSpec((1, ch), fix), pl.BlockSpec((1, ch), fix),
                  pl.BlockSpec((1, ch), fix),
                  pl.BlockSpec((1, width - 1, ch), lambda b, i: (b, 0, 0))],
        out_specs=(pl.BlockSpec((tt, ch), lambda b, i: (b * nt + i, 0)),
                   pl.BlockSpec((1, width - 1, ch), lambda b, i: (b, 0, 0))),
        scratch_shapes=[pltpu.VMEM((base + tt + SUBLANES, ch), F32), pltpu.VMEM((tt, ch), F32)],
        compiler_params=_params("parallel", "arbitrary"),
        name="conformer",
    )(pm, pm, pm, pm, dw_w, dw_b.reshape(1, ch), ln_g.reshape(1, ch), ln_b.reshape(1, ch), prev)


def _fox_prompt_kernel(q_ref, k_ref, v_ref, ck_ref, o_ref, *, tq):
    nq = q_ref.shape[0] // tq
    tiles = [slice(i * tq, (i + 1) * tq) for i in range(nq)]
    k16 = [k_ref[ts, :].astype(BF16) for ts in tiles]
    v16 = [v_ref[ts, :].astype(BF16) for ts in tiles]
    ii = lax.broadcasted_iota(jnp.int32, (tq, tq), 0)
    jj = lax.broadcasted_iota(jnp.int32, (tq, tq), 1)
    for qi in range(nq):
        q = (q_ref[tiles[qi], :] * HEAD_DIM ** -0.5).astype(BF16)
        m = l = acc = None
        for ki in range(qi + 1):
            s = _dot_t(q, k16[ki]) - ck_ref[0, ki]
            if ki == qi:
                s = jnp.where(jj <= ii, s, -1e30)
            s_max = jnp.max(s, axis=-1, keepdims=True)
            if ki == 0:
                m = s_max
                p = jnp.exp(s - m)
                l = jnp.sum(p, axis=-1, keepdims=True)
                acc = jnp.dot(p.astype(BF16), v16[ki], preferred_element_type=F32)
            else:
                m_new = jnp.maximum(m, s_max)
                alpha = jnp.exp(m - m_new)
                p = jnp.exp(s - m_new)
                l = alpha * l + jnp.sum(p, axis=-1, keepdims=True)
                acc = alpha * acc + jnp.dot(p.astype(BF16), v16[ki], preferred_element_type=F32)
                m = m_new
        o_ref[tiles[qi], :] = (acc / l).astype(o_ref.dtype)


def _fox_prompt(pm, q_blk, kh, vh, l, c_row, bsz, t, nch, tq):
    nq = t // tq
    kv_spec = pl.BlockSpec((None, None, None, t, HEAD_DIM), lambda b, h: (l, b, h, 0, 0))
    return pl.pallas_call(
        functools.partial(_fox_prompt_kernel, tq=tq),
        out_shape=jax.ShapeDtypeStruct((bsz * t, nch * HEAD_DIM), BF16),
        grid=(bsz, nch),
        in_specs=[pl.BlockSpec((t, HEAD_DIM), lambda b, h: (b, q_blk + h)),
                  kv_spec, kv_spec,
                  pl.BlockSpec((1, nq, 1, tq), lambda b, h: (b * nch + h, 0, 0, 0))],
        out_specs=pl.BlockSpec((t, HEAD_DIM), lambda b, h: (b, h)),
        compiler_params=_params("parallel", "parallel"),
        name="fox_prompt",
    )(pm, kh, vh, c_row)


def _ffn_up_kernel(a_ref, a2_ref, wg_ref, wv_ref, cwg_ref, cwv_ref, sg_ref, sv_ref, act_ref, ng_ref, nv_ref,
                   u2g_ref, u2v_ref, wcat_ref, *buf_refs, tm, tn, tpb, width):
    i = pl.program_id(1)
    halo = width - 1
    base = SUBLANES
    lo = base - halo
    n_sub = len(buf_refs)
    sub = tm // n_sub
    last = buf_refs[n_sub - 1]

    @pl.when(i == 0)
    def _():
        _cast_rows(wcat_ref, 0, tn, lambda rows: wg_ref[rows, :])
        _cast_rows(wcat_ref, tn, 2 * tn, lambda rows: wv_ref[rows, :])

    first = (i % tpb) == 0

    @pl.when(first)
    def _():
        buf_refs[0][lo:base, 0:tn] = sg_ref[0]
        buf_refs[0][lo:base, tn:2 * tn] = sv_ref[0]

    @pl.when(jnp.logical_not(first))
    def _():
        buf_refs[0][lo:base, :] = last[lo + sub:base + sub, :]

    def conv_rows(r):
        buf = buf_refs[r]
        yg = buf[lo:lo + sub, 0:tn] * cwg_ref[0:1, :]
        yv = buf[lo:lo + sub, tn:2 * tn] * cwv_ref[0:1, :]
        for t in range(1, width):
            yg = yg + buf[lo + t:lo + t + sub, 0:tn] * cwg_ref[t:t + 1, :]
            yv = yv + buf[lo + t:lo + t + sub, tn:2 * tn] * cwv_ref[t:t + 1, :]
        act_ref[r * sub:(r + 1) * sub, :] = (_silu(yg) * yv).astype(act_ref.dtype)

    for r in range(n_sub):
        buf_refs[r][base:base + sub, :] = jnp.dot(a_ref[r * sub:(r + 1) * sub, :], wcat_ref[...],
                                                  preferred_element_type=F32)
        if r + 1 < n_sub:
            buf_refs[r + 1][lo:base, :] = buf_refs[r][lo + sub:base + sub, :]
        if r > 0:
            conv_rows(r - 1)
    conv_rows(n_sub - 1)
    ng_ref[0] = last[lo + sub:base + sub, 0:tn]
    nv_ref[0] = last[lo + sub:base + sub, tn:2 * tn]

    @pl.when(i == pl.num_programs(1) - 1)
    def _():
        up2 = jnp.dot(a2_ref[...], wcat_ref[...], preferred_element_type=F32)
        u2g_ref[...] = up2[:, 0:tn]
        u2v_ref[...] = up2[:, tn:2 * tn]


def _ffn_up(h, h2, w_up, l, conv_w, state, bsz, t):
    m, d = h.shape
    r2 = h2.shape[0]
    f2 = w_up.shape[2]
    f = f2 // 2
    width = conv_w.shape[0]
    tm = _pick(t, (1024, 512, 256, 128, 64))
    tn = _pick(f, (256, 128))
    nj = f // tn
    tpb = t // tm
    n_sub = max(1, tm // FFN_SUB_ROWS)
    return pl.pallas_call(
        functools.partial(_ffn_up_kernel, tm=tm, tn=tn, tpb=tpb, width=width),
        out_shape=(jax.ShapeDtypeStruct((m, f), BF16),
                   jax.ShapeDtypeStruct((bsz, width - 1, f), F32),
                   jax.ShapeDtypeStruct((bsz, width - 1, f), F32),
                   jax.ShapeDtypeStruct((r2, f), F32),
                   jax.ShapeDtypeStruct((r2, f), F32)),
        grid=(nj, m // tm),
        in_specs=[pl.BlockSpec((tm, d), lambda j, i: (i, 0)),
                  pl.BlockSpec((r2, d), lambda j, i: (0, 0)),
                  pl.BlockSpec((None, d, tn), lambda j, i: (l, 0, j)),
                  pl.BlockSpec((None, d, tn), lambda j, i: (l, 0, j + nj)),
                  pl.BlockSpec((width, tn), lambda j, i: (0, j)),
                  pl.BlockSpec((width, tn), lambda j, i: (0, j + nj)),
                  pl.BlockSpec((1, width - 1, tn), lambda j, i: (i // tpb, 0, j)),
                  pl.BlockSpec((1, width - 1, tn), lambda j, i: (i // tpb, 0, j + nj))],
        out_specs=(pl.BlockSpec((tm, tn), lambda j, i: (i, j)),
                   pl.BlockSpec((1, width - 1, tn), lambda j, i: (i // tpb, 0, j)),
                   pl.BlockSpec((1, width - 1, tn), lambda j, i: (i // tpb, 0, j)),
                   pl.BlockSpec((r2, tn), lambda j, i: (0, j)),
                   pl.BlockSpec((r2, tn), lambda j, i: (0, j))),
        scratch_shapes=[pltpu.VMEM((d, 2 * tn), BF16)]
        + [pltpu.VMEM((SUBLANES + tm // n_sub, 2 * tn), F32) for _ in range(n_sub)],
        compiler_params=_params("arbitrary", "arbitrary"),
        name="ffn_up",
    )(h, h2, w_up, w_up, conv_w, conv_w, state, state)


def _col_bcast(row):
    n = row.shape[1]
    return jnp.transpose(jnp.broadcast_to(row, (n, n)))


def _sample_mix_kernel(pm_ref, ps_ref, pf_ref, gprev_ref, sprev_ref, cprev_ref, gw_ref, alog_ref, dtb_ref, nw_ref,
                       cw_ref, cb_ref, lg_ref, lb_ref, bf_ref,
                       oa_ref, ob_ref, snew_ref, gnew_ref, cnew_ref, lf_ref,
                       *, nkh, nvh, nch, z_off, glu_off, bch):
    kw = nkh * HEAD_DIM
    vw = nvh * HEAD_DIM
    aqkv = 2 * kw + vw
    rep = nvh // nkh
    gwidth = gw_ref.shape[0]
    x = pm_ref[0, :, 0:aqkv]
    gprev = gprev_ref[0]
    y = jnp.sum(gprev * gw_ref[0:gwidth - 1, :], axis=0, keepdims=True) + x * gw_ref[gwidth - 1:gwidth, :]
    gnew_ref[0, 0:gwidth - 2, :] = gprev[1:gwidth - 1, :]
    gnew_ref[0, gwidth - 2:gwidth - 1, :] = x
    y = _silu(y)
    ps = ps_ref[0]
    beta_all = jax.nn.sigmoid(ps[:, 0:nvh])
    g_all = -jnp.exp(alog_ref[...]) * _softplus(ps[:, nvh:2 * nvh] + dtb_ref[...])
    lf_ref[0] = -_softplus(-(pf_ref[0, :, 2 * nvh:2 * nvh + nch] + bf_ref[...]))
    nw = nw_ref[...]
    for kh in range(nkh):
        q = y[:, kh * HEAD_DIM:(kh + 1) * HEAD_DIM]
        k = y[:, kw + kh * HEAD_DIM:kw + (kh + 1) * HEAD_DIM]
        q = q * lax.rsqrt(jnp.sum(q * q, axis=-1, keepdims=True) + EPS) * HEAD_DIM ** -0.5
        k = k * lax.rsqrt(jnp.sum(k * k, axis=-1, keepdims=True) + EPS)
        q_cols = _col_bcast(q)
        k_cols = _col_bcast(k)
        qk = jnp.sum(q * k, axis=-1, keepdims=True)
        for r in range(rep):
            h = kh * rep + r
            v = y[:, 2 * kw + h * HEAD_DIM:2 * kw + (h + 1) * HEAD_DIM]
            beta = beta_all[:, h:h + 1]
            eg = jnp.exp(g_all[:, h:h + 1])
            s = sprev_ref[0, h]
            k_s = jnp.sum(k_cols * s, axis=0, keepdims=True)
            q_s = jnp.sum(q_cols * s, axis=0, keepdims=True)
            v_new = beta * v - (beta * eg) * k_s
            o = eg * q_s + qk * v_new
            snew_ref[0, h] = s * eg + k_cols * v_new
            on = o * lax.rsqrt(jnp.mean(o * o, axis=-1, keepdims=True) + EPS) * nw
            z = pm_ref[0, :, z_off + h * HEAD_DIM:z_off + (h + 1) * HEAD_DIM]
            oa_ref[0, :, h * HEAD_DIM:(h + 1) * HEAD_DIM] = (on * _silu(z)).astype(oa_ref.dtype)
    cwidth = cw_ref.shape[0]
    val = pm_ref[0, :, glu_off:glu_off + bch]
    gate = pm_ref[0, :, glu_off + bch:glu_off + 2 * bch]
    u = val * jax.nn.sigmoid(gate)
    cprev = cprev_ref[0]
    yc = (jnp.sum(cprev * cw_ref[0:cwidth - 1, :], axis=0, keepdims=True) + u * cw_ref[cwidth - 1:cwidth, :]
          + cb_ref[...])
    cnew_ref[0, 0:cwidth - 2, :] = cprev[1:cwidth - 1, :]
    cnew_ref[0, cwidth - 2:cwidth - 1, :] = u
    yc = yc - jnp.mean(yc, axis=-1, keepdims=True)
    yn = yc * lax.rsqrt(jnp.mean(yc * yc, axis=-1, keepdims=True) + EPS)
    ob_ref[0] = _silu(yn * lg_ref[...] + lb_ref[...]).astype(ob_ref.dtype)


def _sample_mix(pm, ps, pf, gprev, sprev, cprev, gw, a_log, dt_bias, nw, cw, cb, lg, lb, b_f, nkh, z_off, glu_off):
    bsz = gprev.shape[0]
    nvh = a_log.shape[0]
    nch = b_f.shape[0]
    bch = cw.shape[1]
    vw = nvh * HEAD_DIM
    pmw = pm.shape[1]
    psw = ps.shape[1]
    pm3 = pm[:bsz].reshape(bsz, 1, pmw)
    ps3 = ps[:bsz].reshape(bsz, 1, psw)
    pf3 = pf[:bsz].reshape(bsz, 1, psw)
    per_b = lambda *blk: pl.BlockSpec((1,) + blk, lambda b: (b,) + (0,) * len(blk))
    fixed = lambda a: pl.BlockSpec(a.shape, lambda b: (0,) * a.ndim)
    consts = [gw, a_log.reshape(1, nvh), dt_bias.reshape(1, nvh), nw.reshape(1, HEAD_DIM), cw, cb.reshape(1, bch),
              lg.reshape(1, bch), lb.reshape(1, bch), b_f.reshape(1, nch)]
    return pl.pallas_call(
        functools.partial(_sample_mix_kernel, nkh=nkh, nvh=nvh, nch=nch, z_off=z_off, glu_off=glu_off, bch=bch),
        out_shape=(jax.ShapeDtypeStruct((bsz, 1, vw), BF16),
                   jax.ShapeDtypeStruct((bsz, 1, bch), BF16),
                   jax.ShapeDtypeStruct(sprev.shape, F32),
                   jax.ShapeDtypeStruct(gprev.shape, F32),
                   jax.ShapeDtypeStruct(cprev.shape, F32),
                   jax.ShapeDtypeStruct((bsz, 1, nch), F32)),
        grid=(bsz,),
        in_specs=[per_b(1, pmw), per_b(1, psw), per_b(1, psw), per_b(*gprev.shape[1:]), per_b(*sprev.shape[1:]),
                  per_b(*cprev.shape[1:])] + [fixed(a) for a in consts],
        out_specs=(per_b(1, vw), per_b(1, bch), per_b(*sprev.shape[1:]), per_b(*gprev.shape[1:]),
                   per_b(*cprev.shape[1:]), per_b(1, nch)),
        compiler_params=_params("parallel"),
        name="sample_mix",
    )(pm3, ps3, pf3, gprev, sprev, cprev, *consts)


def _fox_sample_kernel(pt_ref, q_ref, kn_ref, vn_ref, lfn_ref, *refs, nch, hp, group):
    kp_refs = refs[0:group]
    vp_refs = refs[group:2 * group]
    lfp_refs = refs[2 * group:3 * group]
    o_ref, qt_ref, m_ref, l_ref, r_ref, acc_ref = refs[3 * group:]
    pi = pl.program_id(1)
    page = kp_refs[0].shape[1]
    scale = HEAD_DIM ** -0.5

    @pl.when(pi == 0)
    def _():
        q = q_ref[0]
        qt_ref[...] = jnp.transpose(q).astype(BF16)
        m_ref[...] = jnp.sum(jnp.transpose(q * kn_ref[0]), axis=0, keepdims=True) * scale
        l_ref[...] = jnp.ones_like(l_ref)
        r_ref[...] = lfn_ref[0]
        acc_ref[...] = vn_ref[0]

    jj = lax.broadcasted_iota(jnp.int32, (page, page), 0)
    mm = lax.broadcasted_iota(jnp.int32, (page, page), 1)
    later = (mm > jj).astype(BF16)
    lanes = lax.broadcasted_iota(jnp.int32, (page, HEAD_DIM), 1)
    slots = range(group)
    lf = [lfp_refs[g][...] for g in slots]
    lf_hi = [x.astype(BF16) for x in lf]
    lf_lo = [(lf[g] - lf_hi[g].astype(F32)).astype(BF16) for g in slots]
    inner = [jnp.dot(later, lf_hi[g], preferred_element_type=F32)
             + jnp.dot(later, lf_lo[g], preferred_element_type=F32) for g in slots]
    total = [jnp.sum(x, axis=0, keepdims=True) for x in lf]
    r_after = [r_ref[...]]
    for g in slots:
        r_after.append(r_after[g] + total[g])
    qt = qt_ref[...]
    s = []
    for g in slots:
        sg = jnp.zeros((page, HEAD_DIM), F32)
        for h in range(nch):
            sg = jnp.where(lanes == h, jnp.dot(kp_refs[g][h].astype(BF16), qt, preferred_element_type=F32), sg)
        s.append(sg * scale + (r_after[g] + inner[g]))
    m_old = m_ref[...]
    m_new = m_old
    for g in slots:
        m_new = jnp.maximum(m_new, jnp.max(s[g], axis=0, keepdims=True))
    alpha = jnp.exp(m_old - m_new)
    p = [jnp.exp(s[g] - m_new) for g in slots]
    l_new = alpha * l_ref[...]
    for g in slots:
        l_new = l_new + jnp.sum(p[g], axis=0, keepdims=True)
    l_ref[...] = l_new
    m_ref[...] = m_new
    r_ref[...] = r_after[group]
    p_t = [jnp.transpose(p[g])[0:hp, :].astype(BF16) for g in slots]
    rows = lax.broadcasted_iota(jnp.int32, (hp, HEAD_DIM), 0)
    upd = jnp.zeros((hp, HEAD_DIM), F32)
    for h in range(nch):
        oh = jnp.dot(p_t[0], vp_refs[0][h].astype(BF16), preferred_element_type=F32)
        for g in range(1, group):
            oh = oh + jnp.dot(p_t[g], vp_refs[g][h].astype(BF16), preferred_element_type=F32)
        upd = jnp.where(rows == h, oh, upd)
    acc_ref[...] = acc_ref[...] * _col_bcast(alpha)[0:hp, :] + upd

    @pl.when(pi == pl.num_programs(1) - 1)
    def _():
        o_ref[0] = (acc_ref[...] / _col_bcast(l_ref[...])[0:hp, :]).astype(o_ref.dtype)


def _fox_sample(q, kn, vn, lfn, cache_k, cache_v, lf_pad, l, page_table, nch):
    bsz = q.shape[0]
    n_pages = page_table.shape[1]
    page = cache_k.shape[3]
    hp = vn.shape[1]
    assert page == HEAD_DIM and cache_k.shape[4] == HEAD_DIM and cache_k.shape[2] == nch
    pt = page_table.reshape(-1)
    group = _pick(n_pages, (FOX_PAGE_GROUP, 2, 1))
    per_b = lambda *blk: pl.BlockSpec((1,) + blk, lambda b, p, pt: (b,) + (0,) * len(blk))

    def paged(g, *blk):
        return pl.BlockSpec((None, None) + blk,
                            lambda b, p, pt: (l, pt[b * n_pages + n_pages - 1 - (p * group + g)]) + (0,) * len(blk))

    grid_spec = pltpu.PrefetchScalarGridSpec(
        num_scalar_prefetch=1,
        grid=(bsz, n_pages // group),
        in_specs=([per_b(HEAD_DIM, HEAD_DIM), per_b(HEAD_DIM, HEAD_DIM), per_b(hp, HEAD_DIM), per_b(1, HEAD_DIM)]
                  + [paged(g, nch, page, HEAD_DIM) for g in range(group)]
                  + [paged(g, nch, page, HEAD_DIM) for g in range(group)]
                  + [paged(g, page, HEAD_DIM) for g in range(group)]),
        out_specs=per_b(hp, HEAD_DIM),
        scratch_shapes=[pltpu.VMEM((HEAD_DIM, HEAD_DIM), BF16), pltpu.VMEM((1, HEAD_DIM), F32),
                        pltpu.VMEM((1, HEAD_DIM), F32), pltpu.VMEM((1, HEAD_DIM), F32),
                        pltpu.VMEM((hp, HEAD_DIM), F32)],
    )
    return pl.pallas_call(
        functools.partial(_fox_sample_kernel, nch=nch, hp=hp, group=group),
        out_shape=jax.ShapeDtypeStruct((bsz, hp, HEAD_DIM), BF16),
        grid_spec=grid_spec,
        compiler_params=_params("parallel", "arbitrary"),
        name="fox_sample",
    )(pt, q, kn, vn, lfn, *([cache_k] * group + [cache_v] * group + [lf_pad] * group))


def _ffn_sample_kernel(upg_ref, upv_ref, sg_ref, sv_ref, cwg_ref, cwv_ref, act_ref, *, width):
    ys = []
    for up_ref, s_ref, cw_ref in ((upg_ref, sg_ref, cwg_ref), (upv_ref, sv_ref, cwv_ref)):
        y = up_ref[...] * cw_ref[width - 1:width, :]
        for i in range(width - 1):
            y = y + s_ref[i] * cw_ref[i:i + 1, :]
        ys.append(y)
    act_ref[...] = (_silu(ys[0]) * ys[1]).astype(act_ref.dtype)


def _ffn_sample(up, state_t, conv_w):
    r, f2 = up.shape
    f = f2 // 2
    width = conv_w.shape[0]
    tn = _pick(f, (1024, 512, 256, 128))
    nj = f // tn
    return pl.pallas_call(
        functools.partial(_ffn_sample_kernel, width=width),
        out_shape=jax.ShapeDtypeStruct((r, f), BF16),
        grid=(nj,),
        in_specs=[pl.BlockSpec((r, tn), lambda j: (0, j)), pl.BlockSpec((r, tn), lambda j: (0, j + nj)),
                  pl.BlockSpec((width - 1, r, tn), lambda j: (0, 0, j)),
                  pl.BlockSpec((width - 1, r, tn), lambda j: (0, 0, j + nj)),
                  pl.BlockSpec((width, tn), lambda j: (0, j)), pl.BlockSpec((width, tn), lambda j: (0, j + nj))],
        out_specs=pl.BlockSpec((r, tn), lambda j: (0, j)),
        compiler_params=_params("parallel"),
        name="ffn_sample",
    )(up, up, state_t, state_t, conv_w, conv_w)


def _in_proj(h, h2, w_in, l, o_b, shift, glu_w, cw, tn, bsz, t, k_prev, v_prev):
    main = _wmatmul(h, w_in, l, 0, o_b + glu_w + cw, tn, o_b, shift, a2=h2)
    k = _wmatmul(h, w_in, l, o_b + glu_w + cw, cw, tn, o_b, shift, a2=h2, heads=(bsz, t, k_prev))
    v = _wmatmul(h, w_in, l, o_b + glu_w + 2 * cw, cw, tn, o_b, shift, a2=h2, heads=(bsz, t, v_prev))
    ps = _wmatmul(h, w_in, l, o_b, HEAD_DIM, HEAD_DIM, a2=h2)
    pf = _wmatmul(h, w_in, l, o_b + glu_w + 3 * cw, HEAD_DIM, HEAD_DIM, a2=h2)
    return main, k, v, ps, pf


def kernel(x_prompt, x_sample, cache_k, cache_v, cache_logf, page_table, state_gdn, state_gdn_conv, state_conf_conv, state_ffn_conv, norm_mix, w_in, gdn_conv_w, gdn_a_log, gdn_dt_bias, gdn_norm_w, conf_dw_w, conf_dw_b, conf_ln_g, conf_ln_b, fox_b_f, w_out, norm_ffn, ffn_conv_w, w_up, w_down, norm_final):
    bp, t, d = x_prompt.shape
    bs, ts, _ = x_sample.shape
    assert ts == 1, "the sample step handles one new token per sequence"
    depth = w_in.shape[0]
    nvh = gdn_a_log.shape[1]
    aqkv = gdn_conv_w.shape[2]
    vw = nvh * HEAD_DIM
    kw = (aqkv - vw) // 2
    nkh = kw // HEAD_DIM
    nch = fox_b_f.shape[1]
    cw = nch * HEAD_DIM
    bch = conf_dw_w.shape[2]
    f = w_down.shape[1]
    o_b = aqkv + vw
    shift = 2 * nvh
    glu_w = 2 * bch
    tn_in = _pick(o_b, (512, 256, 128))
    assert (t % GDN_CHUNK == 0 and cw == vw and aqkv % vw == 0 and o_b % (bch // 2) == 0
            and glu_w % tn_in == 0 and cw % tn_in == 0)
    z_blk = aqkv // vw
    glu_off = o_b
    q_off = o_b + glu_w
    tq = _pick(t, (512, 256, 128))
    nq = t // tq
    n_chunks = t // GDN_CHUNK
    hp = -(-nch // SUBLANES) * SUBLANES

    xp = x_prompt.reshape(bp * t, d)
    rs = -(-bs // ROW_PAD) * ROW_PAD
    xs = jnp.zeros((rs, d), F32).at[:bs].set(x_sample.reshape(bs, d))
    lf_pad = jnp.pad(cache_logf, ((0, 0), (0, 0), (0, 0), (0, HEAD_DIM - nch)))
    cache_kh = cache_k.transpose(0, 1, 3, 2, 4)
    cache_vh = cache_v.transpose(0, 1, 3, 2, 4)
    k_all = v_all = None
    outs_p, outs_s = [], []
    for l in range(depth):
        wo = w_out[l]
        w_o = (wo[0:vw].astype(BF16), wo[vw:vw + bch].astype(BF16), wo[vw + bch:].astype(BF16))
        w_d = w_down[l].astype(BF16)

        h = _rmsnorm(xp, norm_mix[l], BF16)
        hs = _rmsnorm(xs, norm_mix[l], BF16)
        (pm, pm_s), (k_all, kc_s), (v_all, vc_s), (ps, ps_s), (pf, pf_s) = _in_proj(
            h, hs, w_in, l, o_b, shift, glu_w, cw, tn_in, bp, t, k_all, v_all)

        gb, logf, ccum = _gates(ps, pf, gdn_a_log[l], gdn_dt_bias[l], fox_b_f[l], bp, t)
        qk, vact, gconv_new = _gdn_prep(pm, gdn_conv_w[l], jnp.zeros((bp, gdn_conv_w.shape[1] - 1, aqkv), F32),
                                        bp, t, kw, vw)
        grow = gb[:, :nvh].reshape(bp * n_chunks, GDN_CHUNK, nvh).transpose(0, 2, 1)
        o_a, s_new = _gdn_chunks(qk, vact, pm, z_blk, gb, grow, jnp.zeros((bp, nvh, HEAD_DIM, HEAD_DIM), F32),
                                 gdn_norm_w[l], bp, t, nkh, nvh)
        o_bm, cconv_new = _conformer(pm, glu_off // (bch // 2), conf_dw_w[l], conf_dw_b[l], conf_ln_g[l],
                                     conf_ln_b[l], jnp.zeros((bp, conf_dw_w.shape[1] - 1, bch), F32), bp, t)
        c_t = ccum.reshape(bp, t, nch).transpose(0, 2, 1)
        o_c = _fox_prompt(pm, q_off // HEAD_DIM, k_all, v_all, l, c_t.reshape(bp * nch, nq, 1, tq), bp, t, nch, tq)
        xp = _matmul([o_a, o_bm, o_c], list(w_o), res=xp)
        outs_p.append((logf.reshape(bp, t, nch), s_new, gconv_new, cconv_new))

        o_a, o_bm, s_new, gconv_new, cconv_new, logf = _sample_mix(
            pm_s, ps_s, pf_s, state_gdn_conv[l], state_gdn[l], state_conf_conv[l], gdn_conv_w[l], gdn_a_log[l],
            gdn_dt_bias[l], gdn_norm_w[l], conf_dw_w[l], conf_dw_b[l], conf_ln_g[l], conf_ln_b[l], fox_b_f[l],
            nkh, aqkv, glu_off)
        head_rows = lambda a, rows: jnp.zeros((bs, rows, HEAD_DIM), F32).at[:, :nch].set(a.reshape(bs, nch, HEAD_DIM))
        lfn = jnp.zeros((bs, 1, HEAD_DIM), F32).at[:, :, :nch].set(logf)
        o_c = _fox_sample(head_rows(pm_s[:bs, q_off:q_off + cw], HEAD_DIM), head_rows(kc_s[:bs], HEAD_DIM),
                          head_rows(vc_s[:bs], hp), lfn, cache_kh, cache_vh, lf_pad, l, page_table, nch)
        o_c = o_c[:, :nch].reshape(bs, cw)
        pad_rows = lambda a: jnp.zeros((rs, a.shape[-1]), a.dtype).at[:bs].set(a.reshape(bs, -1))
        xs = _matmul([pad_rows(o_a), pad_rows(o_bm), pad_rows(o_c)], list(w_o), res=xs)

        h = _rmsnorm(xp, norm_ffn[l], BF16)
        hs = _rmsnorm(xs, norm_ffn[l], BF16)
        act, fnew_g, fnew_v, up_g, up_v = _ffn_up(h, hs, w_up, l, ffn_conv_w[l],
                                                  jnp.zeros((bp, ffn_conv_w.shape[1] - 1, 2 * f), F32), bp, t)
        xp = _matmul([act], [w_d], res=xp)
        up = jnp.concatenate([up_g, up_v], axis=1)
        st = jnp.zeros((ffn_conv_w.shape[1] - 1, rs, 2 * f), F32).at[:, :bs].set(state_ffn_conv[l].transpose(1, 0, 2))
        act = _ffn_sample(up, st, ffn_conv_w[l])
        xs = _matmul([act], [w_d], res=xs)
        outs_p[-1] += (jnp.concatenate([fnew_g, fnew_v], axis=-1),)
        fconv_new = jnp.concatenate([state_ffn_conv[l][:, 1:], up[:bs, None, :]], axis=1)
        outs_s.append((kc_s[:bs].reshape(bs, 1, nch, HEAD_DIM), vc_s[:bs].reshape(bs, 1, nch, HEAD_DIM), logf,
                       s_new, gconv_new, cconv_new, fconv_new))

    y_prompt = _rmsnorm(xp, norm_final, F32).reshape(bp, t, d)
    y_sample = _rmsnorm(xs, norm_final, F32)[:bs].reshape(bs, 1, d)
    stack = lambda outs, i: jnp.stack([o[i] for o in outs], axis=0)
    k_rows_p = k_all.transpose(0, 1, 3, 2, 4)
    v_rows_p = v_all.transpose(0, 1, 3, 2, 4)
    return ((y_prompt, y_sample, k_rows_p, v_rows_p) + tuple(stack(outs_p, i) for i in range(5))
            + tuple(stack(outs_s, i) for i in range(7)))
```

```python
import functools
import math

import jax
import jax.numpy as jnp
from jax import lax
from jax.experimental import pallas as pl
from jax.experimental.pallas import tpu as pltpu

EPS = 1e-6
HEAD_DIM = 128
GDN_CHUNK = 64
SUBLANES = 8
ROW_PAD = 16
VMEM_LIMIT = 56 * 2**20
FOX_PAGE_GROUP = 8
FFN_SUB_ROWS = 512

F32 = jnp.float32
BF16 = jnp.bfloat16
_HP = lax.Precision.HIGHEST


def _params(*sem):
    return pltpu.CompilerParams(dimension_semantics=sem, vmem_limit_bytes=VMEM_LIMIT)


def _pick(n, cands):
    for c in cands:
        if n % c == 0:
            return c
    return n


def _silu(x):
    return x * jax.nn.sigmoid(x)


def _softplus(x):
    return jnp.maximum(x, 0.0) + jnp.log(1.0 + jnp.exp(-jnp.abs(x)))


def _dot_t(a, b):
    return lax.dot_general(a, b, (((1,), (1,)), ((), ())), preferred_element_type=F32)


def _hp_dot(a, b):
    return jnp.dot(a, b, precision=_HP, preferred_element_type=F32)


def _rmsnorm_kernel(x_ref, g_ref, o_ref):
    x = x_ref[...]
    ms = jnp.mean(x * x, axis=-1, keepdims=True)
    o_ref[...] = (x * lax.rsqrt(ms + EPS) * g_ref[...]).astype(o_ref.dtype)


def _rmsnorm(x, g, out_dtype):
    m, d = x.shape
    tm = _pick(m, (256, 128, 64, 32, 16, 8))
    return pl.pallas_call(
        _rmsnorm_kernel,
        out_shape=jax.ShapeDtypeStruct((m, d), out_dtype),
        grid=(m // tm,),
        in_specs=[pl.BlockSpec((tm, d), lambda i: (i, 0)), pl.BlockSpec((1, d), lambda i: (0, 0))],
        out_specs=pl.BlockSpec((tm, d), lambda i: (i, 0)),
        compiler_params=_params("parallel"),
        name="rmsnorm",
    )(x, g.reshape(1, d))


def _mm_kernel(*refs, n_a, has_res):
    a_refs = refs[:n_a]
    w_refs = refs[n_a:2 * n_a]
    o_ref = refs[-1]
    acc = jnp.dot(a_refs[0][...], w_refs[0][0], preferred_element_type=F32)
    for a_ref, w_ref in zip(a_refs[1:], w_refs[1:]):
        acc = acc + jnp.dot(a_ref[...], w_ref[0], preferred_element_type=F32)
    if has_res:
        acc = refs[2 * n_a][...] + acc
    o_ref[...] = acc


def _matmul(a_list, w, l, res=None):
    m = a_list[0].shape[0]
    n = w.shape[2]
    k_total = sum(a.shape[1] for a in a_list)
    assert k_total == w.shape[1]
    tm = _pick(m, (1024, 512, 256, 128, 64, 32, 16)) if k_total <= 4096 else _pick(m, (512, 256, 128, 64, 32, 16))
    tn = _pick(n, (512, 256, 128)) if k_total <= 4096 else _pick(n, (256, 128))
    in_specs = [pl.BlockSpec((tm, a.shape[1]), lambda i, j: (i, 0)) for a in a_list]
    row = 0
    for a in a_list:
        in_specs.append(pl.BlockSpec((pl.Element(1), pl.Element(a.shape[1]), pl.Element(tn)),
                                     lambda i, j, row=row: (l, row, j * tn)))
        row += a.shape[1]
    args = list(a_list) + [w] * len(a_list)
    if res is not None:
        in_specs.append(pl.BlockSpec((tm, tn), lambda i, j: (i, j)))
        args.append(res)
    return pl.pallas_call(
        functools.partial(_mm_kernel, n_a=len(a_list), has_res=res is not None),
        out_shape=jax.ShapeDtypeStruct((m, n), F32),
        grid=(m // tm, n // tn),
        in_specs=in_specs,
        out_specs=pl.BlockSpec((tm, tn), lambda i, j: (i, j)),
        compiler_params=_params("parallel", "parallel"),
        name="matmul",
    )(*args)


CAST_ROWS = 512


def _cast_rows(dst_ref, c0, c1, load):
    k = dst_ref.shape[0]
    step = CAST_ROWS if k % CAST_ROWS == 0 else k

    def body(c, carry):
        rows = pl.ds(pl.multiple_of(c * step, step), step)
        dst_ref[rows, c0:c1] = load(rows).astype(BF16)
        return carry

    lax.fori_loop(0, k // step, body, 0)


def _wmm_kernel(*refs, shift, n_plain, has_a2, has_prev, heads_out):
    refs = list(refs)
    a_ref = refs.pop(0)
    a2_ref = refs.pop(0) if has_a2 else None
    wa_ref = refs.pop(0)
    wb_ref = refs.pop(0)
    if has_prev:
        refs.pop(0)
    o_ref = refs.pop(0)
    o2_ref = refs.pop(0) if has_a2 else None
    wbf_ref = refs.pop(0)
    j = pl.program_id(0)
    i = pl.program_id(1)
    tn = wa_ref.shape[1]

    def plain():
        _cast_rows(wbf_ref, 0, tn, lambda rows: wa_ref[rows, :])

    def shifted():
        _cast_rows(wbf_ref, 0, tn,
                   lambda rows: jnp.concatenate([wa_ref[rows, :], wb_ref[rows, :]], axis=1)[:, shift:shift + tn])

    @pl.when(i == 0)
    def _():
        if shift == 0:
            plain()
        else:
            pl.when(j < n_plain)(plain)
            pl.when(j >= n_plain)(shifted)

    acc = jnp.dot(a_ref[...], wbf_ref[...], preferred_element_type=F32)
    if heads_out:
        for hh in range(tn // HEAD_DIM):
            o_ref[hh] = acc[:, hh * HEAD_DIM:(hh + 1) * HEAD_DIM]
    else:
        o_ref[...] = acc
    if has_a2:
        @pl.when(i == pl.num_programs(1) - 1)
        def _():
            o2_ref[...] = jnp.dot(a2_ref[...], wbf_ref[...], preferred_element_type=F32)


def _wmatmul(a, w, l, c0, ncols, tn, plain_cols=0, shift=0, a2=None, heads=None):
    m, k = a.shape
    assert c0 % tn == 0 and ncols % tn == 0 and plain_cols % tn == 0 and tn % HEAD_DIM == 0 and shift < HEAD_DIM
    tm = _pick(m if heads is None else heads[1], (1024, 512, 256, 128, 64, 32, 16))
    jb = c0 // tn
    n_plain = max(0, min(ncols, plain_cols - c0) // tn) if shift else ncols // tn
    lanes_per = tn // HEAD_DIM
    last_blk = -(-w.shape[2] // HEAD_DIM) - 1
    in_specs = [pl.BlockSpec((tm, k), lambda j, i: (i, 0))]
    args = [a]
    if a2 is not None:
        in_specs.append(pl.BlockSpec(a2.shape, lambda j, i: (0, 0)))
        args.append(a2)
    in_specs += [pl.BlockSpec((None, k, tn), lambda j, i: (l, 0, jb + j)),
                 pl.BlockSpec((None, k, HEAD_DIM),
                              lambda j, i: (l, 0, jnp.minimum((jb + j + 1) * lanes_per, last_blk)))]
    args += [w, w]
    aliases = {}
    if heads is None:
        out_shape = [jax.ShapeDtypeStruct((m, ncols), F32)]
        out_specs = [pl.BlockSpec((tm, tn), lambda j, i: (i, j))]
    else:
        bsz, t, prev = heads
        assert t % tm == 0
        tpb = t // tm
        out_shape = [jax.ShapeDtypeStruct((w.shape[0], bsz, ncols // HEAD_DIM, t, HEAD_DIM), F32)]
        out_specs = [pl.BlockSpec((None, None, lanes_per, tm, HEAD_DIM),
                                  lambda j, i: (l, i // tpb, j, i % tpb, 0))]
        if prev is not None:
            in_specs.append(pl.BlockSpec(memory_space=pl.ANY))
            aliases = {len(args): 0}
            args.append(prev)
    if a2 is not None:
        out_shape.append(jax.ShapeDtypeStruct((a2.shape[0], ncols), F32))
        out_specs.append(pl.BlockSpec((a2.shape[0], tn), lambda j, i: (0, j)))
    res = pl.pallas_call(
        functools.partial(_wmm_kernel, shift=shift, n_plain=n_plain, has_a2=a2 is not None,
                          has_prev=heads is not None and heads[2] is not None, heads_out=heads is not None),
        out_shape=out_shape,
        grid=(ncols // tn, m // tm),
        in_specs=in_specs,
        out_specs=out_specs,
        scratch_shapes=[pltpu.VMEM((k, tn), BF16)],
        input_output_aliases=aliases,
        compiler_params=_params("arbitrary", "arbitrary"),
        name="wmatmul",
    )(*args)
    return res[0] if a2 is None else tuple(res)


def _gates_kernel(ps_ref, pf_ref, alog_ref, dtb_ref, bf_ref, gb_ref, lf_ref, c_ref, carry_ref, *, nvh, nch):
    t = pl.program_id(1)
    ps = ps_ref[...]
    tt = ps.shape[0]
    b = ps[:, 0:nvh]
    a = ps[:, nvh:2 * nvh]
    f = pf_ref[:, 2 * nvh:2 * nvh + nch]
    gb_ref[:, 0:nvh] = -jnp.exp(alog_ref[...]) * _softplus(a + dtb_ref[...])
    gb_ref[:, nvh:2 * nvh] = jax.nn.sigmoid(b)
    lf = -_softplus(-(f + bf_ref[...]))
    lf_ref[...] = lf

    @pl.when(t == 0)
    def _():
        carry_ref[...] = jnp.zeros_like(carry_ref)

    ii = lax.broadcasted_iota(jnp.int32, (tt, tt), 0)
    jj = lax.broadcasted_iota(jnp.int32, (tt, tt), 1)
    tri = (ii >= jj).astype(F32)
    c = _hp_dot(tri, lf) + carry_ref[...]
    c_ref[...] = c
    carry_ref[...] = c[tt - 1:tt, :]


def _gates(ps, pf, a_log, dt_bias, b_f, bsz, t):
    nvh = a_log.shape[0]
    nch = b_f.shape[0]
    tt = _pick(t, (256, 128, 64, 32, 16, 8))
    nt = t // tt
    pw = ps.shape[1]
    row = lambda b, i: (b * nt + i, 0)
    fix = lambda b, i: (0, 0)
    return pl.pallas_call(
        functools.partial(_gates_kernel, nvh=nvh, nch=nch),
        out_shape=(jax.ShapeDtypeStruct((bsz * t, 2 * nvh), F32),
                   jax.ShapeDtypeStruct((bsz * t, nch), F32),
                   jax.ShapeDtypeStruct((bsz * t, nch), F32)),
        grid=(bsz, nt),
        in_specs=[pl.BlockSpec((tt, pw), row), pl.BlockSpec((tt, pw), row), pl.BlockSpec((1, nvh), fix),
                  pl.BlockSpec((1, nvh), fix), pl.BlockSpec((1, nch), fix)],
        out_specs=(pl.BlockSpec((tt, 2 * nvh), row), pl.BlockSpec((tt, nch), row), pl.BlockSpec((tt, nch), row)),
        scratch_shapes=[pltpu.VMEM((1, nch), F32)],
        compiler_params=_params("parallel", "arbitrary"),
        name="gates",
    )(ps, pf, a_log.reshape(1, nvh), dt_bias.reshape(1, nvh), b_f.reshape(1, nch))


def _gdn_prep_kernel(x_ref, w_ref, prev_ref, qk_ref, v_ref, cnew_ref, buf_ref, *, tt, kw, width):
    t = pl.program_id(1)
    halo = width - 1
    base = SUBLANES
    lo = base - halo

    @pl.when(t == 0)
    def _():
        buf_ref[lo:base, :] = prev_ref[0]

    @pl.when(t > 0)
    def _():
        buf_ref[lo:base, :] = buf_ref[lo + tt:base + tt, :]

    buf_ref[base:base + tt, :] = x_ref[...]
    chans = x_ref.shape[1]
    for c0 in range(0, chans, HEAD_DIM):
        cs = slice(c0, c0 + HEAD_DIM)
        y = buf_ref[lo:lo + tt, cs] * w_ref[0:1, cs]
        for i in range(1, width):
            y = y + buf_ref[lo + i:lo + i + tt, cs] * w_ref[i:i + 1, cs]
        y = _silu(y)
        if c0 < 2 * kw:
            y = y * lax.rsqrt(jnp.sum(y * y, axis=-1, keepdims=True) + EPS)
            if c0 < kw:
                y = y * HEAD_DIM ** -0.5
            qk_ref[:, cs] = y
        else:
            v_ref[:, c0 - 2 * kw:c0 - 2 * kw + HEAD_DIM] = y

    @pl.when(t == pl.num_programs(1) - 1)
    def _():
        cnew_ref[0] = buf_ref[lo + tt:base + tt, :]


def _gdn_prep(pm, conv_w, prev, bsz, t, kw, vw):
    width, chans = conv_w.shape
    tt = _pick(t, (256, 128, 64))
    nt = t // tt
    row = lambda b, i: (b * nt + i, 0)
    return pl.pallas_call(
        functools.partial(_gdn_prep_kernel, tt=tt, kw=kw, width=width),
        out_shape=(jax.ShapeDtypeStruct((bsz * t, 2 * kw), F32),
                   jax.ShapeDtypeStruct((bsz * t, vw), F32),
                   jax.ShapeDtypeStruct((bsz, width - 1, chans), F32)),
        grid=(bsz, nt),
        in_specs=[pl.BlockSpec((tt, chans), row),
                  pl.BlockSpec((width, chans), lambda b, i: (0, 0)),
                  pl.BlockSpec((1, width - 1, chans), lambda b, i: (b, 0, 0))],
        out_specs=(pl.BlockSpec((tt, 2 * kw), row), pl.BlockSpec((tt, vw), row),
                   pl.BlockSpec((1, width - 1, chans), lambda b, i: (b, 0, 0))),
        scratch_shapes=[pltpu.VMEM((SUBLANES + tt, chans), F32)],
        compiler_params=_params("parallel", "arbitrary"),
        name="gdn_prep",
    )(pm, conv_w, prev)


def _bdot(a, b):
    return jnp.dot(a.astype(BF16), b.astype(BF16), preferred_element_type=F32)


def _gdn_chunk_kernel(qk_ref, v_ref, z_ref, gb_ref, grow_ref, s0_ref, nw_ref, o_ref, sout_ref, s_ref,
                      *, c, nkh, nvh, kw):
    ci = pl.program_id(1)

    @pl.when(ci == 0)
    def _():
        s_ref[...] = s0_ref[0]

    rep = nvh // nkh
    ii = lax.broadcasted_iota(jnp.int32, (c, c), 0)
    jj = lax.broadcasted_iota(jnp.int32, (c, c), 1)
    lower = ii >= jj
    strict = ii > jj
    same_blk = {}
    size = SUBLANES
    while size <= c:
        sh = int(math.log2(size))
        same_blk[size] = (ii >> sh) == (jj >> sh)
        size *= 2
    gb = gb_ref[...]
    grow = grow_ref[0]
    nw = nw_ref[...]
    heads = range(nvh)
    hsl = [slice(h * HEAD_DIM, (h + 1) * HEAD_DIM) for h in heads]
    q = [qk_ref[:, kh * HEAD_DIM:(kh + 1) * HEAD_DIM] for kh in range(nkh)]
    k = [qk_ref[:, kw + kh * HEAD_DIM:kw + (kh + 1) * HEAD_DIM] for kh in range(nkh)]
    v = [v_ref[:, hsl[h]] for h in heads]
    s_old = [s_ref[h] for h in heads]
    k16 = [x.astype(BF16) for x in k]
    kk = [_dot_t(k16[i], k16[i]) for i in range(nkh)]
    qk = [_dot_t(q[i].astype(BF16), k16[i]) for i in range(nkh)]
    beta = [gb[:, nvh + h:nvh + h + 1] for h in heads]
    gc_col = [jnp.sum(jnp.where(lower, grow[h:h + 1, :], 0.0), axis=1, keepdims=True) for h in heads]
    gc_row = [jnp.sum(jnp.where(ii <= jj, gb[:, h:h + 1], 0.0), axis=0, keepdims=True) for h in heads]
    decay = [jnp.exp(jnp.where(lower, gc_col[h] - gc_row[h], -1e30)) for h in heads]
    a = [jnp.where(strict, kk[h // rep] * beta[h] * decay[h], 0.0) for h in heads]
    d = [jnp.where(same_blk[SUBLANES], a[h], 0.0) for h in heads]
    d2 = [_bdot(d[h], d[h]) for h in heads]
    d4 = [_bdot(d2[h], d2[h]) for h in heads]
    x = [d2[h] - d[h] - _bdot(d2[h], d[h]) for h in heads]
    x = [x[h] + d4[h] + _bdot(d4[h], x[h]) for h in heads]
    size = SUBLANES
    while size < c:
        r = [jnp.where(same_blk[2 * size], a[h], 0.0) - jnp.where(same_blk[size], a[h], 0.0) for h in heads]
        y = [r[h] + _bdot(x[h], r[h]) for h in heads]
        x = [x[h] - y[h] - _bdot(y[h], x[h]) for h in heads]
        size *= 2
    egc = [jnp.exp(gc_col[h]) for h in heads]
    rhs = [jnp.concatenate([v[h] * beta[h], k[h // rep] * (beta[h] * egc[h])], axis=1) for h in heads]
    uw = [rhs[h] + _bdot(x[h], rhs[h]) for h in heads]
    s16 = [s_old[h].astype(BF16) for h in heads]
    v_new = [uw[h][:, :HEAD_DIM] - jnp.dot(uw[h][:, HEAD_DIM:].astype(BF16), s16[h], preferred_element_type=F32)
             for h in heads]
    vn16 = [v_new[h].astype(BF16) for h in heads]
    o = [jnp.dot((q[h // rep] * egc[h]).astype(BF16), s16[h], preferred_element_type=F32)
         + jnp.dot((qk[h // rep] * decay[h]).astype(BF16), vn16[h], preferred_element_type=F32) for h in heads]
    g_last = [gc_col[h][c - 1:c, :] for h in heads]
    kd = [(k[h // rep] * jnp.exp(g_last[h] - gc_col[h])).astype(BF16) for h in heads]
    s_new = [s_old[h] * jnp.exp(g_last[h])
             + lax.dot_general(kd[h], vn16[h], (((0,), (0,)), ((), ())), preferred_element_type=F32) for h in heads]
    on = [o[h] * lax.rsqrt(jnp.mean(o[h] * o[h], axis=-1, keepdims=True) + EPS) * nw for h in heads]
    for h in heads:
        s_ref[h] = s_new[h]
        o_ref[:, hsl[h]] = (on[h] * _silu(z_ref[:, hsl[h]])).astype(o_ref.dtype)

    @pl.when(ci == pl.num_programs(1) - 1)
    def _():
        sout_ref[0] = s_ref[...]


def _gdn_chunks(qk, v, pm, z_blk, gb, grow, s0, norm_w, bsz, t, nkh, nvh):
    c = GDN_CHUNK
    n = t // c
    kw = nkh * HEAD_DIM
    vw = nvh * HEAD_DIM
    row = lambda b, i: (b * n + i, 0)
    return pl.pallas_call(
        functools.partial(_gdn_chunk_kernel, c=c, nkh=nkh, nvh=nvh, kw=kw),
        out_shape=(jax.ShapeDtypeStruct((bsz * t, vw), BF16),
                   jax.ShapeDtypeStruct((bsz, nvh, HEAD_DIM, HEAD_DIM), F32)),
        grid=(bsz, n),
        in_specs=[pl.BlockSpec((c, 2 * kw), row),
                  pl.BlockSpec((c, vw), row),
                  pl.BlockSpec((c, vw), lambda b, i: (b * n + i, z_blk)),
                  pl.BlockSpec((c, 2 * nvh), row),
                  pl.BlockSpec((1, nvh, c), lambda b, i: (b * n + i, 0, 0)),
                  pl.BlockSpec((1, nvh, HEAD_DIM, HEAD_DIM), lambda b, i: (b, 0, 0, 0)),
                  pl.BlockSpec((1, HEAD_DIM), lambda b, i: (0, 0))],
        out_specs=(pl.BlockSpec((c, vw), row),
                   pl.BlockSpec((1, nvh, HEAD_DIM, HEAD_DIM), lambda b, i: (b, 0, 0, 0))),
        scratch_shapes=[pltpu.VMEM((nvh, HEAD_DIM, HEAD_DIM), F32)],
        compiler_params=_params("parallel", "arbitrary"),
        name="gdn_chunks",
    )(qk, v, pm, gb, grow, s0, norm_w.reshape(1, HEAD_DIM))


def _conf_kernel(val0_ref, val1_ref, gate0_ref, gate1_ref, w_ref, b_ref, g_ref, beta_ref, prev_ref, o_ref, cnew_ref,
                 buf_ref, y_ref, *, tt, width, base):
    t = pl.program_id(1)
    halo = width - 1
    lo = base - halo
    half = val0_ref.shape[1]
    ch = 2 * half

    @pl.when(t == 0)
    def _():
        buf_ref[lo:base, :] = prev_ref[0]
        buf_ref[base + tt:base + tt + SUBLANES, :] = jnp.zeros((SUBLANES, ch), F32)

    @pl.when(t > 0)
    def _():
        buf_ref[lo:base, :] = buf_ref[lo + tt:base + tt, :]

    buf_ref[base:base + tt, 0:half] = val0_ref[...] * jax.nn.sigmoid(gate0_ref[...])
    buf_ref[base:base + tt, half:2 * half] = val1_ref[...] * jax.nn.sigmoid(gate1_ref[...])
    cchunk = 2 * HEAD_DIM if ch % (2 * HEAD_DIM) == 0 else ch
    for c0 in range(0, ch, cchunk):
        cs = slice(c0, c0 + cchunk)
        y = None
        for r in range(SUBLANES):
            z = None
            for i in range(width):
                if (lo + i) % SUBLANES != r:
                    continue
                a0 = (lo + i) - r
                term = buf_ref[a0:a0 + tt + SUBLANES, cs] * w_ref[i:i + 1, cs]
                z = term if z is None else z + term
            if z is not None:
                zr = z[r:r + tt, :]
                y = zr if y is None else y + zr
        y_ref[:, cs] = y
    y = y_ref[...] + b_ref[...]
    yc = y - jnp.mean(y, axis=-1, keepdims=True)
    yn = yc * lax.rsqrt(jnp.mean(yc * yc, axis=-1, keepdims=True) + EPS)
    o_ref[...] = _silu(yn * g_ref[...] + beta_ref[...]).astype(o_ref.dtype)

    @pl.when(t == pl.num_programs(1) - 1)
    def _():
        cnew_ref[0] = buf_ref[lo + tt:base + tt, :]


def _conformer(pm, val_blk, dw_w, dw_b, ln_g, ln_b, prev, bsz, t):
    width, ch = dw_w.shape
    tt = _pick(t, (128, 64))
    nt = t // tt
    base = -(-(width - 1) // SUBLANES) * SUBLANES
    fix = lambda b, i: (0, 0)
    half = ch // 2
    part = lambda p: pl.BlockSpec((tt, half), lambda b, i: (b * nt + i, val_blk + p))
    return pl.pallas_call(
        functools.partial(_conf_kernel, tt=tt, width=width, base=base),
        out_shape=(jax.ShapeDtypeStruct((bsz * t, ch), BF16),
                   jax.ShapeDtypeStruct((bsz, width - 1, ch), F32)),
        grid=(bsz, nt),
        in_specs=[part(0), part(1), part(2), part(3),
                  pl.BlockSpec((width, ch), fix), pl.BlockSpec((1, ch), fix), pl.BlockSpec((1, ch), fix),
                  pl.BlockSpec((1, ch), fix),
                  pl.BlockSpec((1, width - 1, ch), lambda b, i: (b, 0, 0))],
        out_specs=(pl.BlockSpec((tt, ch), lambda b, i: (b * nt + i, 0)),
                   pl.BlockSpec((1, width - 1, ch), lambda b, i: (b, 0, 0))),
        scratch_shapes=[pltpu.VMEM((base + tt + SUBLANES, ch), F32), pltpu.VMEM((tt, ch), F32)],
        compiler_params=_params("parallel", "arbitrary"),
        name="conformer",
    )(pm, pm, pm, pm, dw_w, dw_b.reshape(1, ch), ln_g.reshape(1, ch), ln_b.reshape(1, ch), prev)


def _fox_prompt_kernel(q_ref, k_ref, v_ref, ck_ref, o_ref, *, tq):
    nq = q_ref.shape[0] // tq
    tiles = [slice(i * tq, (i + 1) * tq) for i in range(nq)]
    k16 = [k_ref[ts, :].astype(BF16) for ts in tiles]
    v16 = [v_ref[ts, :].astype(BF16) for ts in tiles]
    ii = lax.broadcasted_iota(jnp.int32, (tq, tq), 0)
    jj = lax.broadcasted_iota(jnp.int32, (tq, tq), 1)
    for qi in range(nq):
        q = (q_ref[tiles[qi], :] * HEAD_DIM ** -0.5).astype(BF16)
        m = l = acc = None
        for ki in range(qi + 1):
            s = _dot_t(q, k16[ki]) - ck_ref[0, ki]
            if ki == qi:
                s = jnp.where(jj <= ii, s, -1e30)
            s_max = jnp.max(s, axis=-1, keepdims=True)
            if ki == 0:
                m = s_max
                p = jnp.exp(s - m)
                l = jnp.sum(p, axis=-1, keepdims=True)
                acc = jnp.dot(p.astype(BF16), v16[ki], preferred_element_type=F32)
            else:
                m_new = jnp.maximum(m, s_max)
                alpha = jnp.exp(m - m_new)
                p = jnp.exp(s - m_new)
                l = alpha * l + jnp.sum(p, axis=-1, keepdims=True)
                acc = alpha * acc + jnp.dot(p.astype(BF16), v16[ki], preferred_element_type=F32)
                m = m_new
        o_ref[tiles[qi], :] = (acc / l).astype(o_ref.dtype)


def _fox_prompt(pm, q_blk, kh, vh, l, c_row, bsz, t, nch, tq):
    nq = t // tq
    kv_spec = pl.BlockSpec((None, None, None, t, HEAD_DIM), lambda b, h: (l, b, h, 0, 0))
    return pl.pallas_call(
        functools.partial(_fox_prompt_kernel, tq=tq),
        out_shape=jax.ShapeDtypeStruct((bsz * t, nch * HEAD_DIM), BF16),
        grid=(bsz, nch),
        in_specs=[pl.BlockSpec((t, HEAD_DIM), lambda b, h: (b, q_blk + h)),
                  kv_spec, kv_spec,
                  pl.BlockSpec((1, nq, 1, tq), lambda b, h: (b * nch + h, 0, 0, 0))],
        out_specs=pl.BlockSpec((t, HEAD_DIM), lambda b, h: (b, h)),
        compiler_params=_params("parallel", "parallel"),
        name="fox_prompt",
    )(pm, kh, vh, c_row)


def _ffn_up_kernel(a_ref, a2_ref, wg_ref, wv_ref, cwg_ref, cwv_ref, sg_ref, sv_ref, act_ref, ng_ref, nv_ref,
                   u2g_ref, u2v_ref, wcat_ref, *buf_refs, tm, tn, tpb, width):
    i = pl.program_id(1)
    halo = width - 1
    base = SUBLANES
    lo = base - halo
    n_sub = len(buf_refs)
    sub = tm // n_sub
    last = buf_refs[n_sub - 1]

    @pl.when(i == 0)
    def _():
        _cast_rows(wcat_ref, 0, tn, lambda rows: wg_ref[rows, :])
        _cast_rows(wcat_ref, tn, 2 * tn, lambda rows: wv_ref[rows, :])

    first = (i % tpb) == 0

    @pl.when(first)
    def _():
        buf_refs[0][lo:base, 0:tn] = sg_ref[0]
        buf_refs[0][lo:base, tn:2 * tn] = sv_ref[0]

    @pl.when(jnp.logical_not(first))
    def _():
        buf_refs[0][lo:base, :] = last[lo + sub:base + sub, :]

    def conv_rows(r):
        buf = buf_refs[r]
        yg = buf[lo:lo + sub, 0:tn] * cwg_ref[0:1, :]
        yv = buf[lo:lo + sub, tn:2 * tn] * cwv_ref[0:1, :]
        for t in range(1, width):
            yg = yg + buf[lo + t:lo + t + sub, 0:tn] * cwg_ref[t:t + 1, :]
            yv = yv + buf[lo + t:lo + t + sub, tn:2 * tn] * cwv_ref[t:t + 1, :]
        act_ref[r * sub:(r + 1) * sub, :] = (_silu(yg) * yv).astype(act_ref.dtype)

    for r in range(n_sub):
        buf_refs[r][base:base + sub, :] = jnp.dot(a_ref[r * sub:(r + 1) * sub, :], wcat_ref[...],
                                                  preferred_element_type=F32)
        if r + 1 < n_sub:
            buf_refs[r + 1][lo:base, :] = buf_refs[r][lo + sub:base + sub, :]
        if r > 0:
            conv_rows(r - 1)
    conv_rows(n_sub - 1)
    ng_ref[0] = last[lo + sub:base + sub, 0:tn]
    nv_ref[0] = last[lo + sub:base + sub, tn:2 * tn]

    @pl.when(i == pl.num_programs(1) - 1)
    def _():
        up2 = jnp.dot(a2_ref[...], wcat_ref[...], preferred_element_type=F32)
        u2g_ref[...] = up2[:, 0:tn]
        u2v_ref[...] = up2[:, tn:2 * tn]


def _ffn_up(h, h2, w_up, l, conv_w, state, bsz, t):
    m, d = h.shape
    r2 = h2.shape[0]
    f2 = w_up.shape[2]
    f = f2 // 2
    width = conv_w.shape[0]
    tm = _pick(t, (1024, 512, 256, 128, 64))
    tn = _pick(f, (256, 128))
    nj = f // tn
    tpb = t // tm
    n_sub = max(1, tm // FFN_SUB_ROWS)
    return pl.pallas_call(
        functools.partial(_ffn_up_kernel, tm=tm, tn=tn, tpb=tpb, width=width),
        out_shape=(jax.ShapeDtypeStruct((m, f), BF16),
                   jax.ShapeDtypeStruct((bsz, width - 1, f), F32),
                   jax.ShapeDtypeStruct((bsz, width - 1, f), F32),
                   jax.ShapeDtypeStruct((r2, f), F32),
                   jax.ShapeDtypeStruct((r2, f), F32)),
        grid=(nj, m // tm),
        in_specs=[pl.BlockSpec((tm, d), lambda j, i: (i, 0)),
                  pl.BlockSpec((r2, d), lambda j, i: (0, 0)),
                  pl.BlockSpec((None, d, tn), lambda j, i: (l, 0, j)),
                  pl.BlockSpec((None, d, tn), lambda j, i: (l, 0, j + nj)),
                  pl.BlockSpec((width, tn), lambda j, i: (0, j)),
                  pl.BlockSpec((width, tn), lambda j, i: (0, j + nj)),
                  pl.BlockSpec((1, width - 1, tn), lambda j, i: (i // tpb, 0, j)),
                  pl.BlockSpec((1, width - 1, tn), lambda j, i: (i // tpb, 0, j + nj))],
        out_specs=(pl.BlockSpec((tm, tn), lambda j, i: (i, j)),
                   pl.BlockSpec((1, width - 1, tn), lambda j, i: (i // tpb, 0, j)),
                   pl.BlockSpec((1, width - 1, tn), lambda j, i: (i // tpb, 0, j)),
                   pl.BlockSpec((r2, tn), lambda j, i: (0, j)),
                   pl.BlockSpec((r2, tn), lambda j, i: (0, j))),
        scratch_shapes=[pltpu.VMEM((d, 2 * tn), BF16)]
        + [pltpu.VMEM((SUBLANES + tm // n_sub, 2 * tn), F32) for _ in range(n_sub)],
        compiler_params=_params("arbitrary", "arbitrary"),
        name="ffn_up",
    )(h, h2, w_up, w_up, conv_w, conv_w, state, state)


def _col_bcast(row):
    n = row.shape[1]
    return jnp.transpose(jnp.broadcast_to(row, (n, n)))


def _sample_mix_kernel(pm_ref, ps_ref, pf_ref, gprev_ref, sprev_ref, cprev_ref, gw_ref, alog_ref, dtb_ref, nw_ref,
                       cw_ref, cb_ref, lg_ref, lb_ref, bf_ref,
                       oa_ref, ob_ref, snew_ref, gnew_ref, cnew_ref, lf_ref,
                       *, nkh, nvh, nch, z_off, glu_off, bch):
    kw = nkh * HEAD_DIM
    vw = nvh * HEAD_DIM
    aqkv = 2 * kw + vw
    rep = nvh // nkh
    gwidth = gw_ref.shape[0]
    x = pm_ref[0, :, 0:aqkv]
    gprev = gprev_ref[0]
    y = jnp.sum(gprev * gw_ref[0:gwidth - 1, :], axis=0, keepdims=True) + x * gw_ref[gwidth - 1:gwidth, :]
    gnew_ref[0, 0:gwidth - 2, :] = gprev[1:gwidth - 1, :]
    gnew_ref[0, gwidth - 2:gwidth - 1, :] = x
    y = _silu(y)
    ps = ps_ref[0]
    beta_all = jax.nn.sigmoid(ps[:, 0:nvh])
    g_all = -jnp.exp(alog_ref[...]) * _softplus(ps[:, nvh:2 * nvh] + dtb_ref[...])
    lf_ref[0] = -_softplus(-(pf_ref[0, :, 2 * nvh:2 * nvh + nch] + bf_ref[...]))
    nw = nw_ref[...]
    for kh in range(nkh):
        q = y[:, kh * HEAD_DIM:(kh + 1) * HEAD_DIM]
        k = y[:, kw + kh * HEAD_DIM:kw + (kh + 1) * HEAD_DIM]
        q = q * lax.rsqrt(jnp.sum(q * q, axis=-1, keepdims=True) + EPS) * HEAD_DIM ** -0.5
        k = k * lax.rsqrt(jnp.sum(k * k, axis=-1, keepdims=True) + EPS)
        q_cols = _col_bcast(q)
        k_cols = _col_bcast(k)
        qk = jnp.sum(q * k, axis=-1, keepdims=True)
        for r in range(rep):
            h = kh * rep + r
            v = y[:, 2 * kw + h * HEAD_DIM:2 * kw + (h + 1) * HEAD_DIM]
            beta = beta_all[:, h:h + 1]
            eg = jnp.exp(g_all[:, h:h + 1])
            s = sprev_ref[0, h]
            k_s = jnp.sum(k_cols * s, axis=0, keepdims=True)
            q_s = jnp.sum(q_cols * s, axis=0, keepdims=True)
            v_new = beta * v - (beta * eg) * k_s
            o = eg * q_s + qk * v_new
            snew_ref[0, h] = s * eg + k_cols * v_new
            on = o * lax.rsqrt(jnp.mean(o * o, axis=-1, keepdims=True) + EPS) * nw
            z = pm_ref[0, :, z_off + h * HEAD_DIM:z_off + (h + 1) * HEAD_DIM]
            oa_ref[0, :, h * HEAD_DIM:(h + 1) * HEAD_DIM] = (on * _silu(z)).astype(oa_ref.dtype)
    cwidth = cw_ref.shape[0]
    val = pm_ref[0, :, glu_off:glu_off + bch]
    gate = pm_ref[0, :, glu_off + bch:glu_off + 2 * bch]
    u = val * jax.nn.sigmoid(gate)
    cprev = cprev_ref[0]
    yc = (jnp.sum(cprev * cw_ref[0:cwidth - 1, :], axis=0, keepdims=True) + u * cw_ref[cwidth - 1:cwidth, :]
          + cb_ref[...])
    cnew_ref[0, 0:cwidth - 2, :] = cprev[1:cwidth - 1, :]
    cnew_ref[0, cwidth - 2:cwidth - 1, :] = u
    yc = yc - jnp.mean(yc, axis=-1, keepdims=True)
    yn = yc * lax.rsqrt(jnp.mean(yc * yc, axis=-1, keepdims=True) + EPS)
    ob_ref[0] = _silu(yn * lg_ref[...] + lb_ref[...]).astype(ob_ref.dtype)


def _sample_mix(pm, ps, pf, gprev, sprev, cprev, gw, a_log, dt_bias, nw, cw, cb, lg, lb, b_f, nkh, z_off, glu_off):
    bsz = gprev.shape[0]
    nvh = a_log.shape[0]
    nch = b_f.shape[0]
    bch = cw.shape[1]
    vw = nvh * HEAD_DIM
    pmw = pm.shape[1]
    psw = ps.shape[1]
    pm3 = pm[:bsz].reshape(bsz, 1, pmw)
    ps3 = ps[:bsz].reshape(bsz, 1, psw)
    pf3 = pf[:bsz].reshape(bsz, 1, psw)
    per_b = lambda *blk: pl.BlockSpec((1,) + blk, lambda b: (b,) + (0,) * len(blk))
    fixed = lambda a: pl.BlockSpec(a.shape, lambda b: (0,) * a.ndim)
    consts = [gw, a_log.reshape(1, nvh), dt_bias.reshape(1, nvh), nw.reshape(1, HEAD_DIM), cw, cb.reshape(1, bch),
              lg.reshape(1, bch), lb.reshape(1, bch), b_f.reshape(1, nch)]
    return pl.pallas_call(
        functools.partial(_sample_mix_kernel, nkh=nkh, nvh=nvh, nch=nch, z_off=z_off, glu_off=glu_off, bch=bch),
        out_shape=(jax.ShapeDtypeStruct((bsz, 1, vw), BF16),
                   jax.ShapeDtypeStruct((bsz, 1, bch), BF16),
                   jax.ShapeDtypeStruct(sprev.shape, F32),
                   jax.ShapeDtypeStruct(gprev.shape, F32),
                   jax.ShapeDtypeStruct(cprev.shape, F32),
                   jax.ShapeDtypeStruct((bsz, 1, nch), F32)),
        grid=(bsz,),
        in_specs=[per_b(1, pmw), per_b(1, psw), per_b(1, psw), per_b(*gprev.shape[1:]), per_b(*sprev.shape[1:]),
                  per_b(*cprev.shape[1:])] + [fixed(a) for a in consts],
        out_specs=(per_b(1, vw), per_b(1, bch), per_b(*sprev.shape[1:]), per_b(*gprev.shape[1:]),
                   per_b(*cprev.shape[1:]), per_b(1, nch)),
        compiler_params=_params("parallel"),
        name="sample_mix",
    )(pm3, ps3, pf3, gprev, sprev, cprev, *consts)


def _fox_sample_kernel(pt_ref, q_ref, kn_ref, vn_ref, lfn_ref, *refs, nch, hp, group):
    kp_refs = refs[0:group]
    vp_refs = refs[group:2 * group]
    lfp_refs = refs[2 * group:3 * group]
    o_ref, qt_ref, m_ref, l_ref, r_ref, acc_ref = refs[3 * group:]
    pi = pl.program_id(1)
    page = kp_refs[0].shape[1]
    scale = HEAD_DIM ** -0.5

    @pl.when(pi == 0)
    def _():
        q = q_ref[0]
        qt_ref[...] = jnp.transpose(q).astype(BF16)
        m_ref[...] = jnp.sum(jnp.transpose(q * kn_ref[0]), axis=0, keepdims=True) * scale
        l_ref[...] = jnp.ones_like(l_ref)
        r_ref[...] = lfn_ref[0]
        acc_ref[...] = vn_ref[0]

    jj = lax.broadcasted_iota(jnp.int32, (page, page), 0)
    mm = lax.broadcasted_iota(jnp.int32, (page, page), 1)
    later = (mm > jj).astype(BF16)
    lanes = lax.broadcasted_iota(jnp.int32, (page, HEAD_DIM), 1)
    slots = range(group)
    lf = [lfp_refs[g][...] for g in slots]
    lf_hi = [x.astype(BF16) for x in lf]
    lf_lo = [(lf[g] - lf_hi[g].astype(F32)).astype(BF16) for g in slots]
    inner = [jnp.dot(later, lf_hi[g], preferred_element_type=F32)
             + jnp.dot(later, lf_lo[g], preferred_element_type=F32) for g in slots]
    total = [jnp.sum(x, axis=0, keepdims=True) for x in lf]
    r_after = [r_ref[...]]
    for g in slots:
        r_after.append(r_after[g] + total[g])
    qt = qt_ref[...]
    s = []
    for g in slots:
        sg = jnp.zeros((page, HEAD_DIM), F32)
        for h in range(nch):
            sg = jnp.where(lanes == h, jnp.dot(kp_refs[g][h].astype(BF16), qt, preferred_element_type=F32), sg)
        s.append(sg * scale + (r_after[g] + inner[g]))
    m_old = m_ref[...]
    m_new = m_old
    for g in slots:
        m_new = jnp.maximum(m_new, jnp.max(s[g], axis=0, keepdims=True))
    alpha = jnp.exp(m_old - m_new)
    p = [jnp.exp(s[g] - m_new) for g in slots]
    l_new = alpha * l_ref[...]
    for g in slots:
        l_new = l_new + jnp.sum(p[g], axis=0, keepdims=True)
    l_ref[...] = l_new
    m_ref[...] = m_new
    r_ref[...] = r_after[group]
    p_t = [jnp.transpose(p[g])[0:hp, :].astype(BF16) for g in slots]
    rows = lax.broadcasted_iota(jnp.int32, (hp, HEAD_DIM), 0)
    upd = jnp.zeros((hp, HEAD_DIM), F32)
    for h in range(nch):
        oh = jnp.dot(p_t[0], vp_refs[0][h].astype(BF16), preferred_element_type=F32)
        for g in range(1, group):
            oh = oh + jnp.dot(p_t[g], vp_refs[g][h].astype(BF16), preferred_element_type=F32)
        upd = jnp.where(rows == h, oh, upd)
    acc_ref[...] = acc_ref[...] * _col_bcast(alpha)[0:hp, :] + upd

    @pl.when(pi == pl.num_programs(1) - 1)
    def _():
        o_ref[0] = (acc_ref[...] / _col_bcast(l_ref[...])[0:hp, :]).astype(o_ref.dtype)


def _fox_sample(q, kn, vn, lfn, cache_k, cache_v, lf_pad, l, page_table, nch):
    bsz = q.shape[0]
    n_pages = page_table.shape[1]
    page = cache_k.shape[3]
    hp = vn.shape[1]
    assert page == HEAD_DIM and cache_k.shape[4] == HEAD_DIM and cache_k.shape[2] == nch
    pt = page_table.reshape(-1)
    group = _pick(n_pages, (FOX_PAGE_GROUP, 2, 1))
    per_b = lambda *blk: pl.BlockSpec((1,) + blk, lambda b, p, pt: (b,) + (0,) * len(blk))

    def paged(g, *blk):
        return pl.BlockSpec((None, None) + blk,
                            lambda b, p, pt: (l, pt[b * n_pages + n_pages - 1 - (p * group + g)]) + (0,) * len(blk))

    grid_spec = pltpu.PrefetchScalarGridSpec(
        num_scalar_prefetch=1,
        grid=(bsz, n_pages // group),
        in_specs=([per_b(HEAD_DIM, HEAD_DIM), per_b(HEAD_DIM, HEAD_DIM), per_b(hp, HEAD_DIM), per_b(1, HEAD_DIM)]
                  + [paged(g, nch, page, HEAD_DIM) for g in range(group)]
                  + [paged(g, nch, page, HEAD_DIM) for g in range(group)]
                  + [paged(g, page, HEAD_DIM) for g in range(group)]),
        out_specs=per_b(hp, HEAD_DIM),
        scratch_shapes=[pltpu.VMEM((HEAD_DIM, HEAD_DIM), BF16), pltpu.VMEM((1, HEAD_DIM), F32),
                        pltpu.VMEM((1, HEAD_DIM), F32), pltpu.VMEM((1, HEAD_DIM), F32),
                        pltpu.VMEM((hp, HEAD_DIM), F32)],
    )
    return pl.pallas_call(
        functools.partial(_fox_sample_kernel, nch=nch, hp=hp, group=group),
        out_shape=jax.ShapeDtypeStruct((bsz, hp, HEAD_DIM), BF16),
        grid_spec=grid_spec,
        compiler_params=_params("parallel", "arbitrary"),
        name="fox_sample",
    )(pt, q, kn, vn, lfn, *([cache_k] * group + [cache_v] * group + [lf_pad] * group))


def _ffn_sample_kernel(upg_ref, upv_ref, sg_ref, sv_ref, cwg_ref, cwv_ref, act_ref, *, width):
    ys = []
    for up_ref, s_ref, cw_ref in ((upg_ref, sg_ref, cwg_ref), (upv_ref, sv_ref, cwv_ref)):
        y = up_ref[...] * cw_ref[width - 1:width, :]
        for i in range(width - 1):
            y = y + s_ref[i] * cw_ref[i:i + 1, :]
        ys.append(y)
    act_ref[...] = (_silu(ys[0]) * ys[1]).astype(act_ref.dtype)


def _ffn_sample(up, state_t, conv_w):
    r, f2 = up.shape
    f = f2 // 2
    width = conv_w.shape[0]
    tn = _pick(f, (1024, 512, 256, 128))
    nj = f // tn
    return pl.pallas_call(
        functools.partial(_ffn_sample_kernel, width=width),
        out_shape=jax.ShapeDtypeStruct((r, f), BF16),
        grid=(nj,),
        in_specs=[pl.BlockSpec((r, tn), lambda j: (0, j)), pl.BlockSpec((r, tn), lambda j: (0, j + nj)),
                  pl.BlockSpec((width - 1, r, tn), lambda j: (0, 0, j)),
                  pl.BlockSpec((width - 1, r, tn), lambda j: (0, 0, j + nj)),
                  pl.BlockSpec((width, tn), lambda j: (0, j)), pl.BlockSpec((width, tn), lambda j: (0, j + nj))],
        out_specs=pl.BlockSpec((r, tn), lambda j: (0, j)),
        compiler_params=_params("parallel"),
        name="ffn_sample",
    )(up, up, state_t, state_t, conv_w, conv_w)


def _in_proj(h, h2, w_in, l, o_b, shift, glu_w, cw, tn, bsz, t, k_prev, v_prev):
    main = _wmatmul(h, w_in, l, 0, o_b + glu_w + cw, tn, o_b, shift, a2=h2)
    k = _wmatmul(h, w_in, l, o_b + glu_w + cw, cw, tn, o_b, shift, a2=h2, heads=(bsz, t, k_prev))
    v = _wmatmul(h, w_in, l, o_b + glu_w + 2 * cw, cw, tn, o_b, shift, a2=h2, heads=(bsz, t, v_prev))
    ps = _wmatmul(h, w_in, l, o_b, HEAD_DIM, HEAD_DIM, a2=h2)
    pf = _wmatmul(h, w_in, l, o_b + glu_w + 3 * cw, HEAD_DIM, HEAD_DIM, a2=h2)
    return main, k, v, ps, pf


def kernel(x_prompt, x_sample, cache_k, cache_v, cache_logf, page_table, state_gdn, state_gdn_conv, state_conf_conv, state_ffn_conv, norm_mix, w_in, gdn_conv_w, gdn_a_log, gdn_dt_bias, gdn_norm_w, conf_dw_w, conf_dw_b, conf_ln_g, conf_ln_b, fox_b_f, w_out, norm_ffn, ffn_conv_w, w_up, w_down, norm_final):
    bp, t, d = x_prompt.shape
    bs, ts, _ = x_sample.shape
    assert ts == 1, "the sample step handles one new token per sequence"
    depth = w_in.shape[0]
    nvh = gdn_a_log.shape[1]
    aqkv = gdn_conv_w.shape[2]
    vw = nvh * HEAD_DIM
    kw = (aqkv - vw) // 2
    nkh = kw // HEAD_DIM
    nch = fox_b_f.shape[1]
    cw = nch * HEAD_DIM
    bch = conf_dw_w.shape[2]
    f = w_down.shape[1]
    o_b = aqkv + vw
    shift = 2 * nvh
    glu_w = 2 * bch
    tn_in = _pick(o_b, (512, 256, 128))
    assert (t % GDN_CHUNK == 0 and cw == vw and aqkv % vw == 0 and o_b % (bch // 2) == 0
            and glu_w % tn_in == 0 and cw % tn_in == 0)
    z_blk = aqkv // vw
    glu_off = o_b
    q_off = o_b + glu_w
    tq = _pick(t, (512, 256, 128))
    nq = t // tq
    n_chunks = t // GDN_CHUNK
    hp = -(-nch // SUBLANES) * SUBLANES

    xp = x_prompt.reshape(bp * t, d)
    rs = -(-bs // ROW_PAD) * ROW_PAD
    xs = jnp.zeros((rs, d), F32).at[:bs].set(x_sample.reshape(bs, d))
    lf_pad = jnp.pad(cache_logf, ((0, 0), (0, 0), (0, 0), (0, HEAD_DIM - nch)))
    cache_kh = cache_k.transpose(0, 1, 3, 2, 4)
    cache_vh = cache_v.transpose(0, 1, 3, 2, 4)
    k_all = v_all = None
    w_o = w_out.astype(BF16)
    w_d = w_down.astype(BF16)
    outs_p, outs_s = [], []
    for l in range(depth):

        h = _rmsnorm(xp, norm_mix[l], BF16)
        hs = _rmsnorm(xs, norm_mix[l], BF16)
        (pm, pm_s), (k_all, kc_s), (v_all, vc_s), (ps, ps_s), (pf, pf_s) = _in_proj(
            h, hs, w_in, l, o_b, shift, glu_w, cw, tn_in, bp, t, k_all, v_all)

        gb, logf, ccum = _gates(ps, pf, gdn_a_log[l], gdn_dt_bias[l], fox_b_f[l], bp, t)
        qk, vact, gconv_new = _gdn_prep(pm, gdn_conv_w[l], jnp.zeros((bp, gdn_conv_w.shape[1] - 1, aqkv), F32),
                                        bp, t, kw, vw)
        grow = gb[:, :nvh].reshape(bp * n_chunks, GDN_CHUNK, nvh).transpose(0, 2, 1)
        o_a, s_new = _gdn_chunks(qk, vact, pm, z_blk, gb, grow, jnp.zeros((bp, nvh, HEAD_DIM, HEAD_DIM), F32),
                                 gdn_norm_w[l], bp, t, nkh, nvh)
        o_bm, cconv_new = _conformer(pm, glu_off // (bch // 2), conf_dw_w[l], conf_dw_b[l], conf_ln_g[l],
                                     conf_ln_b[l], jnp.zeros((bp, conf_dw_w.shape[1] - 1, bch), F32), bp, t)
        c_t = ccum.reshape(bp, t, nch).transpose(0, 2, 1)
        o_c = _fox_prompt(pm, q_off // HEAD_DIM, k_all, v_all, l, c_t.reshape(bp * nch, nq, 1, tq), bp, t, nch, tq)
        xp = _matmul([o_a, o_bm, o_c], w_o, l, res=xp)
        outs_p.append((logf.reshape(bp, t, nch), s_new, gconv_new, cconv_new))

        o_a, o_bm, s_new, gconv_new, cconv_new, logf = _sample_mix(
            pm_s, ps_s, pf_s, state_gdn_conv[l], state_gdn[l], state_conf_conv[l], gdn_conv_w[l], gdn_a_log[l],
            gdn_dt_bias[l], gdn_norm_w[l], conf_dw_w[l], conf_dw_b[l], conf_ln_g[l], conf_ln_b[l], fox_b_f[l],
            nkh, aqkv, glu_off)
        head_rows = lambda a, rows: jnp.zeros((bs, rows, HEAD_DIM), F32).at[:, :nch].set(a.reshape(bs, nch, HEAD_DIM))
        lfn = jnp.zeros((bs, 1, HEAD_DIM), F32).at[:, :, :nch].set(logf)
        o_c = _fox_sample(head_rows(pm_s[:bs, q_off:q_off + cw], HEAD_DIM), head_rows(kc_s[:bs], HEAD_DIM),
                          head_rows(vc_s[:bs], hp), lfn, cache_kh, cache_vh, lf_pad, l, page_table, nch)
        o_c = o_c[:, :nch].reshape(bs, cw)
        pad_rows = lambda a: jnp.zeros((rs, a.shape[-1]), a.dtype).at[:bs].set(a.reshape(bs, -1))
        xs = _matmul([pad_rows(o_a), pad_rows(o_bm), pad_rows(o_c)], w_o, l, res=xs)

        h = _rmsnorm(xp, norm_ffn[l], BF16)
        hs = _rmsnorm(xs, norm_ffn[l], BF16)
        act, fnew_g, fnew_v, up_g, up_v = _ffn_up(h, hs, w_up, l, ffn_conv_w[l],
                                                  jnp.zeros((bp, ffn_conv_w.shape[1] - 1, 2 * f), F32), bp, t)
        xp = _matmul([act], w_d, l, res=xp)
        up = jnp.concatenate([up_g, up_v], axis=1)
        st = jnp.zeros((ffn_conv_w.shape[1] - 1, rs, 2 * f), F32).at[:, :bs].set(state_ffn_conv[l].transpose(1, 0, 2))
        act = _ffn_sample(up, st, ffn_conv_w[l])
        xs = _matmul([act], w_d, l, res=xs)
        outs_p[-1] += (jnp.concatenate([fnew_g, fnew_v], axis=-1),)
        fconv_new = jnp.concatenate([state_ffn_conv[l][:, 1:], up[:bs, None, :]], axis=1)
        outs_s.append((kc_s[:bs].reshape(bs, 1, nch, HEAD_DIM), vc_s[:bs].reshape(bs, 1, nch, HEAD_DIM), logf,
                       s_new, gconv_new, cconv_new, fconv_new))

    y_prompt = _rmsnorm(xp, norm_final, F32).reshape(bp, t, d)
    y_sample = _rmsnorm(xs, norm_final, F32)[:bs].reshape(bs, 1, d)
    stack = lambda outs, i: jnp.stack([o[i] for o in outs], axis=0)
    k_rows_p = k_all.transpose(0, 1, 3, 2, 4)
    v_rows_p = v_all.transpose(0, 1, 3, 2, 4)
    return ((y_prompt, y_sample, k_rows_p, v_rows_p) + tuple(stack(outs_p, i) for i in range(5))
            + tuple(stack(outs_s, i) for i in range(7)))
```

```python
import functools
import math

import jax
import jax.numpy as jnp
from jax import lax
from jax.experimental import pallas as pl
from jax.experimental.pallas import tpu as pltpu

EPS = 1e-6
HEAD_DIM = 128
GDN_CHUNK = 64
SUBLANES = 8
ROW_PAD = 16
VMEM_LIMIT = 56 * 2**20
FOX_PAGE_GROUP = 8
FFN_SUB_ROWS = 512

F32 = jnp.float32
BF16 = jnp.bfloat16
_HP = lax.Precision.HIGHEST


def _params(*sem):
    return pltpu.CompilerParams(dimension_semantics=sem, vmem_limit_bytes=VMEM_LIMIT)


def _pick(n, cands):
    for c in cands:
        if n % c == 0:
            return c
    return n


def _silu(x):
    return x * jax.nn.sigmoid(x)


def _softplus(x):
    return jnp.maximum(x, 0.0) + jnp.log(1.0 + jnp.exp(-jnp.abs(x)))


def _dot_t(a, b):
    return lax.dot_general(a, b, (((1,), (1,)), ((), ())), preferred_element_type=F32)


def _hp_dot(a, b):
    return jnp.dot(a, b, precision=_HP, preferred_element_type=F32)


def _rmsnorm_kernel(x_ref, g_ref, o_ref):
    x = x_ref[...]
    ms = jnp.mean(x * x, axis=-1, keepdims=True)
    o_ref[...] = (x * lax.rsqrt(ms + EPS) * g_ref[...]).astype(o_ref.dtype)


def _rmsnorm(x, g, out_dtype):
    m, d = x.shape
    tm = _pick(m, (256, 128, 64, 32, 16, 8))
    return pl.pallas_call(
        _rmsnorm_kernel,
        out_shape=jax.ShapeDtypeStruct((m, d), out_dtype),
        grid=(m // tm,),
        in_specs=[pl.BlockSpec((tm, d), lambda i: (i, 0)), pl.BlockSpec((1, d), lambda i: (0, 0))],
        out_specs=pl.BlockSpec((tm, d), lambda i: (i, 0)),
        compiler_params=_params("parallel"),
        name="rmsnorm",
    )(x, g.reshape(1, d))


def _mm_kernel(*refs, n_a, has_res):
    a_refs = refs[:n_a]
    w_refs = refs[n_a:2 * n_a]
    o_ref = refs[-1]
    acc = jnp.dot(a_refs[0][...], w_refs[0][0], preferred_element_type=F32)
    for a_ref, w_ref in zip(a_refs[1:], w_refs[1:]):
        acc = acc + jnp.dot(a_ref[...], w_ref[0], preferred_element_type=F32)
    if has_res:
        acc = refs[2 * n_a][...] + acc
    o_ref[...] = acc


def _matmul(a_list, w, l, res=None):
    m = a_list[0].shape[0]
    n = w.shape[2]
    k_total = sum(a.shape[1] for a in a_list)
    assert k_total == w.shape[1]
    tm = _pick(m, (1024, 512, 256, 128, 64, 32, 16)) if k_total <= 4096 else _pick(m, (512, 256, 128, 64, 32, 16))
    tn = _pick(n, (512, 256, 128))
    in_specs = [pl.BlockSpec((tm, a.shape[1]), lambda i, j: (i, 0)) for a in a_list]
    row = 0
    for a in a_list:
        in_specs.append(pl.BlockSpec((pl.Element(1), pl.Element(a.shape[1]), pl.Element(tn)),
                                     lambda i, j, row=row: (l, row, j * tn)))
        row += a.shape[1]
    args = list(a_list) + [w] * len(a_list)
    if res is not None:
        in_specs.append(pl.BlockSpec((tm, tn), lambda i, j: (i, j)))
        args.append(res)
    return pl.pallas_call(
        functools.partial(_mm_kernel, n_a=len(a_list), has_res=res is not None),
        out_shape=jax.ShapeDtypeStruct((m, n), F32),
        grid=(m // tm, n // tn),
        in_specs=in_specs,
        out_specs=pl.BlockSpec((tm, tn), lambda i, j: (i, j)),
        compiler_params=_params("parallel", "parallel"),
        name="matmul",
    )(*args)


CAST_ROWS = 512


def _cast_rows(dst_ref, c0, c1, load):
    k = dst_ref.shape[0]
    step = CAST_ROWS if k % CAST_ROWS == 0 else k

    def body(c, carry):
        rows = pl.ds(pl.multiple_of(c * step, step), step)
        dst_ref[rows, c0:c1] = load(rows).astype(BF16)
        return carry

    lax.fori_loop(0, k // step, body, 0)


def _wmm_kernel(*refs, shift, n_plain, has_a2, has_prev, heads_out):
    refs = list(refs)
    a_ref = refs.pop(0)
    a2_ref = refs.pop(0) if has_a2 else None
    wa_ref = refs.pop(0)
    wb_ref = refs.pop(0)
    if has_prev:
        refs.pop(0)
    o_ref = refs.pop(0)
    o2_ref = refs.pop(0) if has_a2 else None
    wbf_ref = refs.pop(0)
    j = pl.program_id(0)
    i = pl.program_id(1)
    tn = wa_ref.shape[1]

    def plain():
        _cast_rows(wbf_ref, 0, tn, lambda rows: wa_ref[rows, :])

    def shifted():
        _cast_rows(wbf_ref, 0, tn,
                   lambda rows: jnp.concatenate([wa_ref[rows, :], wb_ref[rows, :]], axis=1)[:, shift:shift + tn])

    @pl.when(i == 0)
    def _():
        if shift == 0:
            plain()
        else:
            pl.when(j < n_plain)(plain)
            pl.when(j >= n_plain)(shifted)

    acc = jnp.dot(a_ref[...], wbf_ref[...], preferred_element_type=F32)
    if heads_out:
        for hh in range(tn // HEAD_DIM):
            o_ref[hh] = acc[:, hh * HEAD_DIM:(hh + 1) * HEAD_DIM]
    else:
        o_ref[...] = acc
    if has_a2:
        @pl.when(i == pl.num_programs(1) - 1)
        def _():
            o2_ref[...] = jnp.dot(a2_ref[...], wbf_ref[...], preferred_element_type=F32)


def _wmatmul(a, w, l, c0, ncols, tn, plain_cols=0, shift=0, a2=None, heads=None):
    m, k = a.shape
    assert c0 % tn == 0 and ncols % tn == 0 and plain_cols % tn == 0 and tn % HEAD_DIM == 0 and shift < HEAD_DIM
    tm = _pick(m if heads is None else heads[1], (1024, 512, 256, 128, 64, 32, 16))
    jb = c0 // tn
    n_plain = max(0, min(ncols, plain_cols - c0) // tn) if shift else ncols // tn
    lanes_per = tn // HEAD_DIM
    last_blk = -(-w.shape[2] // HEAD_DIM) - 1
    in_specs = [pl.BlockSpec((tm, k), lambda j, i: (i, 0))]
    args = [a]
    if a2 is not None:
        in_specs.append(pl.BlockSpec(a2.shape, lambda j, i: (0, 0)))
        args.append(a2)
    in_specs += [pl.BlockSpec((None, k, tn), lambda j, i: (l, 0, jb + j)),
                 pl.BlockSpec((None, k, HEAD_DIM),
                              lambda j, i: (l, 0, jnp.minimum((jb + j + 1) * lanes_per, last_blk)))]
    args += [w, w]
    aliases = {}
    if heads is None:
        out_shape = [jax.ShapeDtypeStruct((m, ncols), F32)]
        out_specs = [pl.BlockSpec((tm, tn), lambda j, i: (i, j))]
    else:
        bsz, t, prev = heads
        assert t % tm == 0
        tpb = t // tm
        out_shape = [jax.ShapeDtypeStruct((w.shape[0], bsz, ncols // HEAD_DIM, t, HEAD_DIM), F32)]
        out_specs = [pl.BlockSpec((None, None, lanes_per, tm, HEAD_DIM),
                                  lambda j, i: (l, i // tpb, j, i % tpb, 0))]
        if prev is not None:
            in_specs.append(pl.BlockSpec(memory_space=pl.ANY))
            aliases = {len(args): 0}
            args.append(prev)
    if a2 is not None:
        out_shape.append(jax.ShapeDtypeStruct((a2.shape[0], ncols), F32))
        out_specs.append(pl.BlockSpec((a2.shape[0], tn), lambda j, i: (0, j)))
    res = pl.pallas_call(
        functools.partial(_wmm_kernel, shift=shift, n_plain=n_plain, has_a2=a2 is not None,
                          has_prev=heads is not None and heads[2] is not None, heads_out=heads is not None),
        out_shape=out_shape,
        grid=(ncols // tn, m // tm),
        in_specs=in_specs,
        out_specs=out_specs,
        scratch_shapes=[pltpu.VMEM((k, tn), BF16)],
        input_output_aliases=aliases,
        compiler_params=_params("arbitrary", "arbitrary"),
        name="wmatmul",
    )(*args)
    return res[0] if a2 is None else tuple(res)


def _gates_kernel(ps_ref, pf_ref, alog_ref, dtb_ref, bf_ref, gb_ref, lf_ref, c_ref, carry_ref, *, nvh, nch):
    t = pl.program_id(1)
    ps = ps_ref[...]
    tt = ps.shape[0]
    b = ps[:, 0:nvh]
    a = ps[:, nvh:2 * nvh]
    f = pf_ref[:, 2 * nvh:2 * nvh + nch]
    gb_ref[:, 0:nvh] = -jnp.exp(alog_ref[...]) * _softplus(a + dtb_ref[...])
    gb_ref[:, nvh:2 * nvh] = jax.nn.sigmoid(b)
    lf = -_softplus(-(f + bf_ref[...]))
    lf_ref[...] = lf

    @pl.when(t == 0)
    def _():
        carry_ref[...] = jnp.zeros_like(carry_ref)

    ii = lax.broadcasted_iota(jnp.int32, (tt, tt), 0)
    jj = lax.broadcasted_iota(jnp.int32, (tt, tt), 1)
    tri = (ii >= jj).astype(F32)
    c = _hp_dot(tri, lf) + carry_ref[...]
    c_ref[...] = c
    carry_ref[...] = c[tt - 1:tt, :]


def _gates(ps, pf, a_log, dt_bias, b_f, bsz, t):
    nvh = a_log.shape[0]
    nch = b_f.shape[0]
    tt = _pick(t, (256, 128, 64, 32, 16, 8))
    nt = t // tt
    pw = ps.shape[1]
    row = lambda b, i: (b * nt + i, 0)
    fix = lambda b, i: (0, 0)
    return pl.pallas_call(
        functools.partial(_gates_kernel, nvh=nvh, nch=nch),
        out_shape=(jax.ShapeDtypeStruct((bsz * t, 2 * nvh), F32),
                   jax.ShapeDtypeStruct((bsz * t, nch), F32),
                   jax.ShapeDtypeStruct((bsz * t, nch), F32)),
        grid=(bsz, nt),
        in_specs=[pl.BlockSpec((tt, pw), row), pl.BlockSpec((tt, pw), row), pl.BlockSpec((1, nvh), fix),
                  pl.BlockSpec((1, nvh), fix), pl.BlockSpec((1, nch), fix)],
        out_specs=(pl.BlockSpec((tt, 2 * nvh), row), pl.BlockSpec((tt, nch), row), pl.BlockSpec((tt, nch), row)),
        scratch_shapes=[pltpu.VMEM((1, nch), F32)],
        compiler_params=_params("parallel", "arbitrary"),
        name="gates",
    )(ps, pf, a_log.reshape(1, nvh), dt_bias.reshape(1, nvh), b_f.reshape(1, nch))


def _gdn_prep_kernel(x_ref, w_ref, prev_ref, qk_ref, v_ref, cnew_ref, buf_ref, *, tt, kw, width):
    t = pl.program_id(1)
    halo = width - 1
    base = SUBLANES
    lo = base - halo

    @pl.when(t == 0)
    def _():
        buf_ref[lo:base, :] = prev_ref[0]

    @pl.when(t > 0)
    def _():
        buf_ref[lo:base, :] = buf_ref[lo + tt:base + tt, :]

    buf_ref[base:base + tt, :] = x_ref[...]
    chans = x_ref.shape[1]
    for c0 in range(0, chans, HEAD_DIM):
        cs = slice(c0, c0 + HEAD_DIM)
        y = buf_ref[lo:lo + tt, cs] * w_ref[0:1, cs]
        for i in range(1, width):
            y = y + buf_ref[lo + i:lo + i + tt, cs] * w_ref[i:i + 1, cs]
        y = _silu(y)
        if c0 < 2 * kw:
            y = y * lax.rsqrt(jnp.sum(y * y, axis=-1, keepdims=True) + EPS)
            if c0 < kw:
                y = y * HEAD_DIM ** -0.5
            qk_ref[:, cs] = y
        else:
            v_ref[:, c0 - 2 * kw:c0 - 2 * kw + HEAD_DIM] = y

    @pl.when(t == pl.num_programs(1) - 1)
    def _():
        cnew_ref[0] = buf_ref[lo + tt:base + tt, :]


def _gdn_prep(pm, conv_w, prev, bsz, t, kw, vw):
    width, chans = conv_w.shape
    tt = _pick(t, (256, 128, 64))
    nt = t // tt
    row = lambda b, i: (b * nt + i, 0)
    return pl.pallas_call(
        functools.partial(_gdn_prep_kernel, tt=tt, kw=kw, width=width),
        out_shape=(jax.ShapeDtypeStruct((bsz * t, 2 * kw), F32),
                   jax.ShapeDtypeStruct((bsz * t, vw), F32),
                   jax.ShapeDtypeStruct((bsz, width - 1, chans), F32)),
        grid=(bsz, nt),
        in_specs=[pl.BlockSpec((tt, chans), row),
                  pl.BlockSpec((width, chans), lambda b, i: (0, 0)),
                  pl.BlockSpec((1, width - 1, chans), lambda b, i: (b, 0, 0))],
        out_specs=(pl.BlockSpec((tt, 2 * kw), row), pl.BlockSpec((tt, vw), row),
                   pl.BlockSpec((1, width - 1, chans), lambda b, i: (b, 0, 0))),
        scratch_shapes=[pltpu.VMEM((SUBLANES + tt, chans), F32)],
        compiler_params=_params("parallel", "arbitrary"),
        name="gdn_prep",
    )(pm, conv_w, prev)


def _bdot(a, b):
    return jnp.dot(a.astype(BF16), b.astype(BF16), preferred_element_type=F32)


def _gdn_chunk_kernel(qk_ref, v_ref, z_ref, gb_ref, grow_ref, s0_ref, nw_ref, o_ref, sout_ref, s_ref,
                      *, c, nkh, nvh, kw):
    ci = pl.program_id(1)

    @pl.when(ci == 0)
    def _():
        s_ref[...] = s0_ref[0]

    rep = nvh // nkh
    ii = lax.broadcasted_iota(jnp.int32, (c, c), 0)
    jj = lax.broadcasted_iota(jnp.int32, (c, c), 1)
    lower = ii >= jj
    strict = ii > jj
    same_blk = {}
    size = SUBLANES
    while size <= c:
        sh = int(math.log2(size))
        same_blk[size] = (ii >> sh) == (jj >> sh)
        size *= 2
    gb = gb_ref[...]
    grow = grow_ref[0]
    nw = nw_ref[...]
    heads = range(nvh)
    hsl = [slice(h * HEAD_DIM, (h + 1) * HEAD_DIM) for h in heads]
    q = [qk_ref[:, kh * HEAD_DIM:(kh + 1) * HEAD_DIM] for kh in range(nkh)]
    k = [qk_ref[:, kw + kh * HEAD_DIM:kw + (kh + 1) * HEAD_DIM] for kh in range(nkh)]
    v = [v_ref[:, hsl[h]] for h in heads]
    s_old = [s_ref[h] for h in heads]
    k16 = [x.astype(BF16) for x in k]
    kk = [_dot_t(k16[i], k16[i]) for i in range(nkh)]
    qk = [_dot_t(q[i].astype(BF16), k16[i]) for i in range(nkh)]
    beta = [gb[:, nvh + h:nvh + h + 1] for h in heads]
    gc_col = [jnp.sum(jnp.where(lower, grow[h:h + 1, :], 0.0), axis=1, keepdims=True) for h in heads]
    gc_row = [jnp.sum(jnp.where(ii <= jj, gb[:, h:h + 1], 0.0), axis=0, keepdims=True) for h in heads]
    decay = [jnp.exp(jnp.where(lower, gc_col[h] - gc_row[h], -1e30)) for h in heads]
    a = [jnp.where(strict, kk[h // rep] * beta[h] * decay[h], 0.0) for h in heads]
    d = [jnp.where(same_blk[SUBLANES], a[h], 0.0) for h in heads]
    d2 = [_bdot(d[h], d[h]) for h in heads]
    d4 = [_bdot(d2[h], d2[h]) for h in heads]
    x = [d2[h] - d[h] - _bdot(d2[h], d[h]) for h in heads]
    x = [x[h] + d4[h] + _bdot(d4[h], x[h]) for h in heads]
    size = SUBLANES
    while size < c:
        r = [jnp.where(same_blk[2 * size], a[h], 0.0) - jnp.where(same_blk[size], a[h], 0.0) for h in heads]
        y = [r[h] + _bdot(x[h], r[h]) for h in heads]
        x = [x[h] - y[h] - _bdot(y[h], x[h]) for h in heads]
        size *= 2
    egc = [jnp.exp(gc_col[h]) for h in heads]
    rhs = [jnp.concatenate([v[h] * beta[h], k[h // rep] * (beta[h] * egc[h])], axis=1) for h in heads]
    uw = [rhs[h] + _bdot(x[h], rhs[h]) for h in heads]
    s16 = [s_old[h].astype(BF16) for h in heads]
    v_new = [uw[h][:, :HEAD_DIM] - jnp.dot(uw[h][:, HEAD_DIM:].astype(BF16), s16[h], preferred_element_type=F32)
             for h in heads]
    vn16 = [v_new[h].astype(BF16) for h in heads]
    o = [jnp.dot((q[h // rep] * egc[h]).astype(BF16), s16[h], preferred_element_type=F32)
         + jnp.dot((qk[h // rep] * decay[h]).astype(BF16), vn16[h], preferred_element_type=F32) for h in heads]
    g_last = [gc_col[h][c - 1:c, :] for h in heads]
    kd = [(k[h // rep] * jnp.exp(g_last[h] - gc_col[h])).astype(BF16) for h in heads]
    s_new = [s_old[h] * jnp.exp(g_last[h])
             + lax.dot_general(kd[h], vn16[h], (((0,), (0,)), ((), ())), preferred_element_type=F32) for h in heads]
    on = [o[h] * lax.rsqrt(jnp.mean(o[h] * o[h], axis=-1, keepdims=True) + EPS) * nw for h in heads]
    for h in heads:
        s_ref[h] = s_new[h]
        o_ref[:, hsl[h]] = (on[h] * _silu(z_ref[:, hsl[h]])).astype(o_ref.dtype)

    @pl.when(ci == pl.num_programs(1) - 1)
    def _():
        sout_ref[0] = s_ref[...]


def _gdn_chunks(qk, v, pm, z_blk, gb, grow, s0, norm_w, bsz, t, nkh, nvh):
    c = GDN_CHUNK
    n = t // c
    kw = nkh * HEAD_DIM
    vw = nvh * HEAD_DIM
    row = lambda b, i: (b * n + i, 0)
    return pl.pallas_call(
        functools.partial(_gdn_chunk_kernel, c=c, nkh=nkh, nvh=nvh, kw=kw),
        out_shape=(jax.ShapeDtypeStruct((bsz * t, vw), BF16),
                   jax.ShapeDtypeStruct((bsz, nvh, HEAD_DIM, HEAD_DIM), F32)),
        grid=(bsz, n),
        in_specs=[pl.BlockSpec((c, 2 * kw), row),
                  pl.BlockSpec((c, vw), row),
                  pl.BlockSpec((c, vw), lambda b, i: (b * n + i, z_blk)),
                  pl.BlockSpec((c, 2 * nvh), row),
                  pl.BlockSpec((1, nvh, c), lambda b, i: (b * n + i, 0, 0)),
                  pl.BlockSpec((1, nvh, HEAD_DIM, HEAD_DIM), lambda b, i: (b, 0, 0, 0)),
                  pl.BlockSpec((1, HEAD_DIM), lambda b, i: (0, 0))],
        out_specs=(pl.BlockSpec((c, vw), row),
                   pl.BlockSpec((1, nvh, HEAD_DIM, HEAD_DIM), lambda b, i: (b, 0, 0, 0))),
        scratch_shapes=[pltpu.VMEM((nvh, HEAD_DIM, HEAD_DIM), F32)],
        compiler_params=_params("parallel", "arbitrary"),
        name="gdn_chunks",
    )(qk, v, pm, gb, grow, s0, norm_w.reshape(1, HEAD_DIM))


def _conf_kernel(val0_ref, val1_ref, gate0_ref, gate1_ref, w_ref, b_ref, g_ref, beta_ref, prev_ref, o_ref, cnew_ref,
                 buf_ref, y_ref, *, tt, width, base):
    t = pl.program_id(1)
    halo = width - 1
    lo = base - halo
    half = val0_ref.shape[1]
    ch = 2 * half

    @pl.when(t == 0)
    def _():
        buf_ref[lo:base, :] = prev_ref[0]
        buf_ref[base + tt:base + tt + SUBLANES, :] = jnp.zeros((SUBLANES, ch), F32)

    @pl.when(t > 0)
    def _():
        buf_ref[lo:base, :] = buf_ref[lo + tt:base + tt, :]

    buf_ref[base:base + tt, 0:half] = val0_ref[...] * jax.nn.sigmoid(gate0_ref[...])
    buf_ref[base:base + tt, half:2 * half] = val1_ref[...] * jax.nn.sigmoid(gate1_ref[...])
    cchunk = 2 * HEAD_DIM if ch % (2 * HEAD_DIM) == 0 else ch
    for c0 in range(0, ch, cchunk):
        cs = slice(c0, c0 + cchunk)
        y = None
        for r in range(SUBLANES):
            z = None
            for i in range(width):
                if (lo + i) % SUBLANES != r:
                    continue
                a0 = (lo + i) - r
                term = buf_ref[a0:a0 + tt + SUBLANES, cs] * w_ref[i:i + 1, cs]
                z = term if z is None else z + term
            if z is not None:
                zr = z[r:r + tt, :]
                y = zr if y is None else y + zr
        y_ref[:, cs] = y
    y = y_ref[...] + b_ref[...]
    yc = y - jnp.mean(y, axis=-1, keepdims=True)
    yn = yc * lax.rsqrt(jnp.mean(yc * yc, axis=-1, keepdims=True) + EPS)
    o_ref[...] = _silu(yn * g_ref[...] + beta_ref[...]).astype(o_ref.dtype)

    @pl.when(t == pl.num_programs(1) - 1)
    def _():
        cnew_ref[0] = buf_ref[lo + tt:base + tt, :]


def _conformer(pm, val_blk, dw_w, dw_b, ln_g, ln_b, prev, bsz, t):
    width, ch = dw_w.shape
    tt = _pick(t, (128, 64))
    nt = t // tt
    base = -(-(width - 1) // SUBLANES) * SUBLANES
    fix = lambda b, i: (0, 0)
    half = ch // 2
    part = lambda p: pl.BlockSpec((tt, half), lambda b, i: (b * nt + i, val_blk + p))
    return pl.pallas_call(
        functools.partial(_conf_kernel, tt=tt, width=width, base=base),
        out_shape=(jax.ShapeDtypeStruct((bsz * t, ch), BF16),
                   jax.ShapeDtypeStruct((bsz, width - 1, ch), F32)),
        grid=(bsz, nt),
        in_specs=[part(0), part(1), part(2), part(3),
                  pl.BlockSpec((width, ch), fix), pl.BlockSpec((1, ch), fix), pl.BlockSpec((1, ch), fix),
                  pl.BlockSpec((1, ch), fix),
                  pl.BlockSpec((1, width - 1, ch), lambda b, i: (b, 0, 0))],
        out_specs=(pl.BlockSpec((tt, ch), lambda b, i: (b * nt + i, 0)),
                   pl.BlockSpec((1, width - 1, ch), lambda b, i: (b, 0, 0))),
        scratch_shapes=[pltpu.VMEM((base + tt + SUBLANES, ch), F32), pltpu.VMEM((tt, ch), F32)],
        compiler_params=_params("parallel", "arbitrary"),
        name="conformer",
    )(pm, pm, pm, pm, dw_w, dw_b.reshape(1, ch), ln_g.reshape(1, ch), ln_b.reshape(1, ch), prev)


def _fox_prompt_kernel(q_ref, k_ref, v_ref, ck_ref, o_ref, *, tq):
    nq = q_ref.shape[0] // tq
    tiles = [slice(i * tq, (i + 1) * tq) for i in range(nq)]
    k16 = [k_ref[ts, :].astype(BF16) for ts in tiles]
    v16 = [v_ref[ts, :].astype(BF16) for ts in tiles]
    ii = lax.broadcasted_iota(jnp.int32, (tq, tq), 0)
    jj = lax.broadcasted_iota(jnp.int32, (tq, tq), 1)
    for qi in range(nq):
        q = (q_ref[tiles[qi], :] * HEAD_DIM ** -0.5).astype(BF16)
        m = l = acc = None
        for ki in range(qi + 1):
            s = _dot_t(q, k16[ki]) - ck_ref[0, ki]
            if ki == qi:
                s = jnp.where(jj <= ii, s, -1e30)
            s_max = jnp.max(s, axis=-1, keepdims=True)
            if ki == 0:
                m = s_max
                p = jnp.exp(s - m)
                l = jnp.sum(p, axis=-1, keepdims=True)
                acc = jnp.dot(p.astype(BF16), v16[ki], preferred_element_type=F32)
            else:
                m_new = jnp.maximum(m, s_max)
                alpha = jnp.exp(m - m_new)
                p = jnp.exp(s - m_new)
                l = alpha * l + jnp.sum(p, axis=-1, keepdims=True)
                acc = alpha * acc + jnp.dot(p.astype(BF16), v16[ki], preferred_element_type=F32)
                m = m_new
        o_ref[tiles[qi], :] = (acc / l).astype(o_ref.dtype)


def _fox_prompt(pm, q_blk, kh, vh, l, c_row, bsz, t, nch, tq):
    nq = t // tq
    kv_spec = pl.BlockSpec((None, None, None, t, HEAD_DIM), lambda b, h: (l, b, h, 0, 0))
    return pl.pallas_call(
        functools.partial(_fox_prompt_kernel, tq=tq),
        out_shape=jax.ShapeDtypeStruct((bsz * t, nch * HEAD_DIM), BF16),
        grid=(bsz, nch),
        in_specs=[pl.BlockSpec((t, HEAD_DIM), lambda b, h: (b, q_blk + h)),
                  kv_spec, kv_spec,
                  pl.BlockSpec((1, nq, 1, tq), lambda b, h: (b * nch + h, 0, 0, 0))],
        out_specs=pl.BlockSpec((t, HEAD_DIM), lambda b, h: (b, h)),
        compiler_params=_params("parallel", "parallel"),
        name="fox_prompt",
    )(pm, kh, vh, c_row)


def _ffn_up_kernel(a_ref, a2_ref, wg_ref, wv_ref, cwg_ref, cwv_ref, sg_ref, sv_ref, act_ref, ng_ref, nv_ref,
                   u2g_ref, u2v_ref, wcat_ref, *buf_refs, tm, tn, tpb, width):
    i = pl.program_id(1)
    halo = width - 1
    base = SUBLANES
    lo = base - halo
    n_sub = len(buf_refs)
    sub = tm // n_sub
    last = buf_refs[n_sub - 1]

    @pl.when(i == 0)
    def _():
        _cast_rows(wcat_ref, 0, tn, lambda rows: wg_ref[rows, :])
        _cast_rows(wcat_ref, tn, 2 * tn, lambda rows: wv_ref[rows, :])

    first = (i % tpb) == 0

    @pl.when(first)
    def _():
        buf_refs[0][lo:base, 0:tn] = sg_ref[0]
        buf_refs[0][lo:base, tn:2 * tn] = sv_ref[0]

    @pl.when(jnp.logical_not(first))
    def _():
        buf_refs[0][lo:base, :] = last[lo + sub:base + sub, :]

    def conv_rows(r):
        buf = buf_refs[r]
        yg = buf[lo:lo + sub, 0:tn] * cwg_ref[0:1, :]
        yv = buf[lo:lo + sub, tn:2 * tn] * cwv_ref[0:1, :]
        for t in range(1, width):
            yg = yg + buf[lo + t:lo + t + sub, 0:tn] * cwg_ref[t:t + 1, :]
            yv = yv + buf[lo + t:lo + t + sub, tn:2 * tn] * cwv_ref[t:t + 1, :]
        act_ref[r * sub:(r + 1) * sub, :] = (_silu(yg) * yv).astype(act_ref.dtype)

    for r in range(n_sub):
        buf_refs[r][base:base + sub, :] = jnp.dot(a_ref[r * sub:(r + 1) * sub, :], wcat_ref[...],
                                                  preferred_element_type=F32)
        if r + 1 < n_sub:
            buf_refs[r + 1][lo:base, :] = buf_refs[r][lo + sub:base + sub, :]
        if r > 0:
            conv_rows(r - 1)
    conv_rows(n_sub - 1)
    ng_ref[0] = last[lo + sub:base + sub, 0:tn]
    nv_ref[0] = last[lo + sub:base + sub, tn:2 * tn]

    @pl.when(i == pl.num_programs(1) - 1)
    def _():
        up2 = jnp.dot(a2_ref[...], wcat_ref[...], preferred_element_type=F32)
        u2g_ref[...] = up2[:, 0:tn]
        u2v_ref[...] = up2[:, tn:2 * tn]


def _ffn_up(h, h2, w_up, l, conv_w, state, bsz, t):
    m, d = h.shape
    r2 = h2.shape[0]
    f2 = w_up.shape[2]
    f = f2 // 2
    width = conv_w.shape[0]
    tm = _pick(t, (1024, 512, 256, 128, 64))
    tn = _pick(f, (256, 128))
    nj = f // tn
    tpb = t // tm
    n_sub = max(1, tm // FFN_SUB_ROWS)
    return pl.pallas_call(
        functools.partial(_ffn_up_kernel, tm=tm, tn=tn, tpb=tpb, width=width),
        out_shape=(jax.ShapeDtypeStruct((m, f), BF16),
                   jax.ShapeDtypeStruct((bsz, width - 1, f), F32),
                   jax.ShapeDtypeStruct((bsz, width - 1, f), F32),
                   jax.ShapeDtypeStruct((r2, f), F32),
                   jax.ShapeDtypeStruct((r2, f), F32)),
        grid=(nj, m // tm),
        in_specs=[pl.BlockSpec((tm, d), lambda j, i: (i, 0)),
                  pl.BlockSpec((r2, d), lambda j, i: (0, 0)),
                  pl.BlockSpec((None, d, tn), lambda j, i: (l, 0, j)),
                  pl.BlockSpec((None, d, tn), lambda j, i: (l, 0, j + nj)),
                  pl.BlockSpec((width, tn), lambda j, i: (0, j)),
                  pl.BlockSpec((width, tn), lambda j, i: (0, j + nj)),
                  pl.BlockSpec((1, width - 1, tn), lambda j, i: (i // tpb, 0, j)),
                  pl.BlockSpec((1, width - 1, tn), lambda j, i: (i // tpb, 0, j + nj))],
        out_specs=(pl.BlockSpec((tm, tn), lambda j, i: (i, j)),
                   pl.BlockSpec((1, width - 1, tn), lambda j, i: (i // tpb, 0, j)),
                   pl.BlockSpec((1, width - 1, tn), lambda j, i: (i // tpb, 0, j)),
                   pl.BlockSpec((r2, tn), lambda j, i: (0, j)),
                   pl.BlockSpec((r2, tn), lambda j, i: (0, j))),
        scratch_shapes=[pltpu.VMEM((d, 2 * tn), BF16)]
        + [pltpu.VMEM((SUBLANES + tm // n_sub, 2 * tn), F32) for _ in range(n_sub)],
        compiler_params=_params("arbitrary", "arbitrary"),
        name="ffn_up",
    )(h, h2, w_up, w_up, conv_w, conv_w, state, state)


def _col_bcast(row):
    n = row.shape[1]
    return jnp.transpose(jnp.broadcast_to(row, (n, n)))


def _sample_mix_kernel(pm_ref, ps_ref, pf_ref, gprev_ref, sprev_ref, cprev_ref, gw_ref, alog_ref, dtb_ref, nw_ref,
                       cw_ref, cb_ref, lg_ref, lb_ref, bf_ref,
                       oa_ref, ob_ref, snew_ref, gnew_ref, cnew_ref, lf_ref,
                       *, nkh, nvh, nch, z_off, glu_off, bch):
    kw = nkh * HEAD_DIM
    vw = nvh * HEAD_DIM
    aqkv = 2 * kw + vw
    rep = nvh // nkh
    gwidth = gw_ref.shape[0]
    x = pm_ref[0, :, 0:aqkv]
    gprev = gprev_ref[0]
    y = jnp.sum(gprev * gw_ref[0:gwidth - 1, :], axis=0, keepdims=True) + x * gw_ref[gwidth - 1:gwidth, :]
    gnew_ref[0, 0:gwidth - 2, :] = gprev[1:gwidth - 1, :]
    gnew_ref[0, gwidth - 2:gwidth - 1, :] = x
    y = _silu(y)
    ps = ps_ref[0]
    beta_all = jax.nn.sigmoid(ps[:, 0:nvh])
    g_all = -jnp.exp(alog_ref[...]) * _softplus(ps[:, nvh:2 * nvh] + dtb_ref[...])
    lf_ref[0] = -_softplus(-(pf_ref[0, :, 2 * nvh:2 * nvh + nch] + bf_ref[...]))
    nw = nw_ref[...]
    for kh in range(nkh):
        q = y[:, kh * HEAD_DIM:(kh + 1) * HEAD_DIM]
        k = y[:, kw + kh * HEAD_DIM:kw + (kh + 1) * HEAD_DIM]
        q = q * lax.rsqrt(jnp.sum(q * q, axis=-1, keepdims=True) + EPS) * HEAD_DIM ** -0.5
        k = k * lax.rsqrt(jnp.sum(k * k, axis=-1, keepdims=True) + EPS)
        q_cols = _col_bcast(q)
        k_cols = _col_bcast(k)
        qk = jnp.sum(q * k, axis=-1, keepdims=True)
        for r in range(rep):
            h = kh * rep + r
            v = y[:, 2 * kw + h * HEAD_DIM:2 * kw + (h + 1) * HEAD_DIM]
            beta = beta_all[:, h:h + 1]
            eg = jnp.exp(g_all[:, h:h + 1])
            s = sprev_ref[0, h]
            k_s = jnp.sum(k_cols * s, axis=0, keepdims=True)
            q_s = jnp.sum(q_cols * s, axis=0, keepdims=True)
            v_new = beta * v - (beta * eg) * k_s
            o = eg * q_s + qk * v_new
            snew_ref[0, h] = s * eg + k_cols * v_new
            on = o * lax.rsqrt(jnp.mean(o * o, axis=-1, keepdims=True) + EPS) * nw
            z = pm_ref[0, :, z_off + h * HEAD_DIM:z_off + (h + 1) * HEAD_DIM]
            oa_ref[0, :, h * HEAD_DIM:(h + 1) * HEAD_DIM] = (on * _silu(z)).astype(oa_ref.dtype)
    cwidth = cw_ref.shape[0]
    val = pm_ref[0, :, glu_off:glu_off + bch]
    gate = pm_ref[0, :, glu_off + bch:glu_off + 2 * bch]
    u = val * jax.nn.sigmoid(gate)
    cprev = cprev_ref[0]
    yc = (jnp.sum(cprev * cw_ref[0:cwidth - 1, :], axis=0, keepdims=True) + u * cw_ref[cwidth - 1:cwidth, :]
          + cb_ref[...])
    cnew_ref[0, 0:cwidth - 2, :] = cprev[1:cwidth - 1, :]
    cnew_ref[0, cwidth - 2:cwidth - 1, :] = u
    yc = yc - jnp.mean(yc, axis=-1, keepdims=True)
    yn = yc * lax.rsqrt(jnp.mean(yc * yc, axis=-1, keepdims=True) + EPS)
    ob_ref[0] = _silu(yn * lg_ref[...] + lb_ref[...]).astype(ob_ref.dtype)


def _sample_mix(pm, ps, pf, gprev, sprev, cprev, gw, a_log, dt_bias, nw, cw, cb, lg, lb, b_f, nkh, z_off, glu_off):
    bsz = gprev.shape[0]
    nvh = a_log.shape[0]
    nch = b_f.shape[0]
    bch = cw.shape[1]
    vw = nvh * HEAD_DIM
    pmw = pm.shape[1]
    psw = ps.shape[1]
    pm3 = pm[:bsz].reshape(bsz, 1, pmw)
    ps3 = ps[:bsz].reshape(bsz, 1, psw)
    pf3 = pf[:bsz].reshape(bsz, 1, psw)
    per_b = lambda *blk: pl.BlockSpec((1,) + blk, lambda b: (b,) + (0,) * len(blk))
    fixed = lambda a: pl.BlockSpec(a.shape, lambda b: (0,) * a.ndim)
    consts = [gw, a_log.reshape(1, nvh), dt_bias.reshape(1, nvh), nw.reshape(1, HEAD_DIM), cw, cb.reshape(1, bch),
              lg.reshape(1, bch), lb.reshape(1, bch), b_f.reshape(1, nch)]
    return pl.pallas_call(
        functools.partial(_sample_mix_kernel, nkh=nkh, nvh=nvh, nch=nch, z_off=z_off, glu_off=glu_off, bch=bch),
        out_shape=(jax.ShapeDtypeStruct((bsz, 1, vw), BF16),
                   jax.ShapeDtypeStruct((bsz, 1, bch), BF16),
                   jax.ShapeDtypeStruct(sprev.shape, F32),
                   jax.ShapeDtypeStruct(gprev.shape, F32),
                   jax.ShapeDtypeStruct(cprev.shape, F32),
                   jax.ShapeDtypeStruct((bsz, 1, nch), F32)),
        grid=(bsz,),
        in_specs=[per_b(1, pmw), per_b(1, psw), per_b(1, psw), per_b(*gprev.shape[1:]), per_b(*sprev.shape[1:]),
                  per_b(*cprev.shape[1:])] + [fixed(a) for a in consts],
        out_specs=(per_b(1, vw), per_b(1, bch), per_b(*sprev.shape[1:]), per_b(*gprev.shape[1:]),
                   per_b(*cprev.shape[1:]), per_b(1, nch)),
        compiler_params=_params("parallel"),
        name="sample_mix",
    )(pm3, ps3, pf3, gprev, sprev, cprev, *consts)


def _fox_sample_kernel(pt_ref, q_ref, kn_ref, vn_ref, lfn_ref, *refs, nch, hp, group):
    kp_refs = refs[0:group]
    vp_refs = refs[group:2 * group]
    lfp_refs = refs[2 * group:3 * group]
    o_ref, qt_ref, m_ref, l_ref, r_ref, acc_ref = refs[3 * group:]
    pi = pl.program_id(1)
    page = kp_refs[0].shape[1]
    scale = HEAD_DIM ** -0.5

    @pl.when(pi == 0)
    def _():
        q = q_ref[0]
        qt_ref[...] = jnp.transpose(q).astype(BF16)
        m_ref[...] = jnp.sum(jnp.transpose(q * kn_ref[0]), axis=0, keepdims=True) * scale
        l_ref[...] = jnp.ones_like(l_ref)
        r_ref[...] = lfn_ref[0]
        acc_ref[...] = vn_ref[0]

    jj = lax.broadcasted_iota(jnp.int32, (page, page), 0)
    mm = lax.broadcasted_iota(jnp.int32, (page, page), 1)
    later = (mm > jj).astype(BF16)
    lanes = lax.broadcasted_iota(jnp.int32, (page, HEAD_DIM), 1)
    slots = range(group)
    lf = [lfp_refs[g][...] for g in slots]
    lf_hi = [x.astype(BF16) for x in lf]
    lf_lo = [(lf[g] - lf_hi[g].astype(F32)).astype(BF16) for g in slots]
    inner = [jnp.dot(later, lf_hi[g], preferred_element_type=F32)
             + jnp.dot(later, lf_lo[g], preferred_element_type=F32) for g in slots]
    total = [jnp.sum(x, axis=0, keepdims=True) for x in lf]
    r_after = [r_ref[...]]
    for g in slots:
        r_after.append(r_after[g] + total[g])
    qt = qt_ref[...]
    s = []
    for g in slots:
        sg = jnp.zeros((page, HEAD_DIM), F32)
        for h in range(nch):
            sg = jnp.where(lanes == h, jnp.dot(kp_refs[g][h].astype(BF16), qt, preferred_element_type=F32), sg)
        s.append(sg * scale + (r_after[g] + inner[g]))
    m_old = m_ref[...]
    m_new = m_old
    for g in slots:
        m_new = jnp.maximum(m_new, jnp.max(s[g], axis=0, keepdims=True))
    alpha = jnp.exp(m_old - m_new)
    p = [jnp.exp(s[g] - m_new) for g in slots]
    l_new = alpha * l_ref[...]
    for g in slots:
        l_new = l_new + jnp.sum(p[g], axis=0, keepdims=True)
    l_ref[...] = l_new
    m_ref[...] = m_new
    r_ref[...] = r_after[group]
    p_t = [jnp.transpose(p[g])[0:hp, :].astype(BF16) for g in slots]
    rows = lax.broadcasted_iota(jnp.int32, (hp, HEAD_DIM), 0)
    upd = jnp.zeros((hp, HEAD_DIM), F32)
    for h in range(nch):
        oh = jnp.dot(p_t[0], vp_refs[0][h].astype(BF16), preferred_element_type=F32)
        for g in range(1, group):
            oh = oh + jnp.dot(p_t[g], vp_refs[g][h].astype(BF16), preferred_element_type=F32)
        upd = jnp.where(rows == h, oh, upd)
    acc_ref[...] = acc_ref[...] * _col_bcast(alpha)[0:hp, :] + upd

    @pl.when(pi == pl.num_programs(1) - 1)
    def _():
        o_ref[0] = (acc_ref[...] / _col_bcast(l_ref[...])[0:hp, :]).astype(o_ref.dtype)


def _fox_sample(q, kn, vn, lfn, cache_k, cache_v, lf_pad, l, page_table, nch):
    bsz = q.shape[0]
    n_pages = page_table.shape[1]
    page = cache_k.shape[3]
    hp = vn.shape[1]
    assert page == HEAD_DIM and cache_k.shape[4] == HEAD_DIM and cache_k.shape[2] == nch
    pt = page_table.reshape(-1)
    group = _pick(n_pages, (FOX_PAGE_GROUP, 2, 1))
    per_b = lambda *blk: pl.BlockSpec((1,) + blk, lambda b, p, pt: (b,) + (0,) * len(blk))

    def paged(g, *blk):
        return pl.BlockSpec((None, None) + blk,
                            lambda b, p, pt: (l, pt[b * n_pages + n_pages - 1 - (p * group + g)]) + (0,) * len(blk))

    grid_spec = pltpu.PrefetchScalarGridSpec(
        num_scalar_prefetch=1,
        grid=(bsz, n_pages // group),
        in_specs=([per_b(HEAD_DIM, HEAD_DIM), per_b(HEAD_DIM, HEAD_DIM), per_b(hp, HEAD_DIM), per_b(1, HEAD_DIM)]
                  + [paged(g, nch, page, HEAD_DIM) for g in range(group)]
                  + [paged(g, nch, page, HEAD_DIM) for g in range(group)]
                  + [paged(g, page, HEAD_DIM) for g in range(group)]),
        out_specs=per_b(hp, HEAD_DIM),
        scratch_shapes=[pltpu.VMEM((HEAD_DIM, HEAD_DIM), BF16), pltpu.VMEM((1, HEAD_DIM), F32),
                        pltpu.VMEM((1, HEAD_DIM), F32), pltpu.VMEM((1, HEAD_DIM), F32),
                        pltpu.VMEM((hp, HEAD_DIM), F32)],
    )
    return pl.pallas_call(
        functools.partial(_fox_sample_kernel, nch=nch, hp=hp, group=group),
        out_shape=jax.ShapeDtypeStruct((bsz, hp, HEAD_DIM), BF16),
        grid_spec=grid_spec,
        compiler_params=_params("parallel", "arbitrary"),
        name="fox_sample",
    )(pt, q, kn, vn, lfn, *([cache_k] * group + [cache_v] * group + [lf_pad] * group))


def _ffn_sample_kernel(upg_ref, upv_ref, sg_ref, sv_ref, cwg_ref, cwv_ref, act_ref, *, width):
    ys = []
    for up_ref, s_ref, cw_ref in ((upg_ref, sg_ref, cwg_ref), (upv_ref, sv_ref, cwv_ref)):
        y = up_ref[...] * cw_ref[width - 1:width, :]
        for i in range(width - 1):
            y = y + s_ref[i] * cw_ref[i:i + 1, :]
        ys.append(y)
    act_ref[...] = (_silu(ys[0]) * ys[1]).astype(act_ref.dtype)


def _ffn_sample(up, state_t, conv_w):
    r, f2 = up.shape
    f = f2 // 2
    width = conv_w.shape[0]
    tn = _pick(f, (1024, 512, 256, 128))
    nj = f // tn
    return pl.pallas_call(
        functools.partial(_ffn_sample_kernel, width=width),
        out_shape=jax.ShapeDtypeStruct((r, f), BF16),
        grid=(nj,),
        in_specs=[pl.BlockSpec((r, tn), lambda j: (0, j)), pl.BlockSpec((r, tn), lambda j: (0, j + nj)),
                  pl.BlockSpec((width - 1, r, tn), lambda j: (0, 0, j)),
                  pl.BlockSpec((width - 1, r, tn), lambda j: (0, 0, j + nj)),
                  pl.BlockSpec((width, tn), lambda j: (0, j)), pl.BlockSpec((width, tn), lambda j: (0, j + nj))],
        out_specs=pl.BlockSpec((r, tn), lambda j: (0, j)),
        compiler_params=_params("parallel"),
        name="ffn_sample",
    )(up, up, state_t, state_t, conv_w, conv_w)


def _in_proj(h, h2, w_in, l, o_b, shift, glu_w, cw, tn, bsz, t, k_prev, v_prev):
    main = _wmatmul(h, w_in, l, 0, o_b + glu_w + cw, tn, o_b, shift, a2=h2)
    k = _wmatmul(h, w_in, l, o_b + glu_w + cw, cw, tn, o_b, shift, a2=h2, heads=(bsz, t, k_prev))
    v = _wmatmul(h, w_in, l, o_b + glu_w + 2 * cw, cw, tn, o_b, shift, a2=h2, heads=(bsz, t, v_prev))
    ps = _wmatmul(h, w_in, l, o_b, HEAD_DIM, HEAD_DIM, a2=h2)
    pf = _wmatmul(h, w_in, l, o_b + glu_w + 3 * cw, HEAD_DIM, HEAD_DIM, a2=h2)
    return main, k, v, ps, pf


def kernel(x_prompt, x_sample, cache_k, cache_v, cache_logf, page_table, state_gdn, state_gdn_conv, state_conf_conv, state_ffn_conv, norm_mix, w_in, gdn_conv_w, gdn_a_log, gdn_dt_bias, gdn_norm_w, conf_dw_w, conf_dw_b, conf_ln_g, conf_ln_b, fox_b_f, w_out, norm_ffn, ffn_conv_w, w_up, w_down, norm_final):
    bp, t, d = x_prompt.shape
    bs, ts, _ = x_sample.shape
    assert ts == 1, "the sample step handles one new token per sequence"
    depth = w_in.shape[0]
    nvh = gdn_a_log.shape[1]
    aqkv = gdn_conv_w.shape[2]
    vw = nvh * HEAD_DIM
    kw = (aqkv - vw) // 2
    nkh = kw // HEAD_DIM
    nch = fox_b_f.shape[1]
    cw = nch * HEAD_DIM
    bch = conf_dw_w.shape[2]
    f = w_down.shape[1]
    o_b = aqkv + vw
    shift = 2 * nvh
    glu_w = 2 * bch
    tn_in = _pick(o_b, (512, 256, 128))
    assert (t % GDN_CHUNK == 0 and cw == vw and aqkv % vw == 0 and o_b % (bch // 2) == 0
            and glu_w % tn_in == 0 and cw % tn_in == 0)
    z_blk = aqkv // vw
    glu_off = o_b
    q_off = o_b + glu_w
    tq = _pick(t, (512, 256, 128))
    nq = t // tq
    n_chunks = t // GDN_CHUNK
    hp = -(-nch // SUBLANES) * SUBLANES

    xp = x_prompt.reshape(bp * t, d)
    rs = -(-bs // ROW_PAD) * ROW_PAD
    xs = jnp.zeros((rs, d), F32).at[:bs].set(x_sample.reshape(bs, d))
    lf_pad = jnp.pad(cache_logf, ((0, 0), (0, 0), (0, 0), (0, HEAD_DIM - nch)))
    cache_kh = cache_k.transpose(0, 1, 3, 2, 4)
    cache_vh = cache_v.transpose(0, 1, 3, 2, 4)
    k_all = v_all = None
    w_o = w_out.astype(BF16)
    w_d = w_down.astype(BF16)
    outs_p, outs_s = [], []
    for l in range(depth):

        h = _rmsnorm(xp, norm_mix[l], BF16)
        hs = _rmsnorm(xs, norm_mix[l], BF16)
        (pm, pm_s), (k_all, kc_s), (v_all, vc_s), (ps, ps_s), (pf, pf_s) = _in_proj(
            h, hs, w_in, l, o_b, shift, glu_w, cw, tn_in, bp, t, k_all, v_all)

        gb, logf, ccum = _gates(ps, pf, gdn_a_log[l], gdn_dt_bias[l], fox_b_f[l], bp, t)
        qk, vact, gconv_new = _gdn_prep(pm, gdn_conv_w[l], jnp.zeros((bp, gdn_conv_w.shape[1] - 1, aqkv), F32),
                                        bp, t, kw, vw)
        grow = gb[:, :nvh].reshape(bp * n_chunks, GDN_CHUNK, nvh).transpose(0, 2, 1)
        o_a, s_new = _gdn_chunks(qk, vact, pm, z_blk, gb, grow, jnp.zeros((bp, nvh, HEAD_DIM, HEAD_DIM), F32),
                                 gdn_norm_w[l], bp, t, nkh, nvh)
        o_bm, cconv_new = _conformer(pm, glu_off // (bch // 2), conf_dw_w[l], conf_dw_b[l], conf_ln_g[l],
                                     conf_ln_b[l], jnp.zeros((bp, conf_dw_w.shape[1] - 1, bch), F32), bp, t)
        c_t = ccum.reshape(bp, t, nch).transpose(0, 2, 1)
        o_c = _fox_prompt(pm, q_off // HEAD_DIM, k_all, v_all, l, c_t.reshape(bp * nch, nq, 1, tq), bp, t, nch, tq)
        xp = _matmul([o_a, o_bm, o_c], w_o, l, res=xp)
        outs_p.append((logf.reshape(bp, t, nch), s_new, gconv_new, cconv_new))

        o_a, o_bm, s_new, gconv_new, cconv_new, logf = _sample_mix(
            pm_s, ps_s, pf_s, state_gdn_conv[l], state_gdn[l], state_conf_conv[l], gdn_conv_w[l], gdn_a_log[l],
            gdn_dt_bias[l], gdn_norm_w[l], conf_dw_w[l], conf_dw_b[l], conf_ln_g[l], conf_ln_b[l], fox_b_f[l],
            nkh, aqkv, glu_off)
        head_rows = lambda a, rows: jnp.zeros((bs, rows, HEAD_DIM), F32).at[:, :nch].set(a.reshape(bs, nch, HEAD_DIM))
        lfn = jnp.zeros((bs, 1, HEAD_DIM), F32).at[:, :, :nch].set(logf)
        o_c = _fox_sample(head_rows(pm_s[:bs, q_off:q_off + cw], HEAD_DIM), head_rows(kc_s[:bs], HEAD_DIM),
                          head_rows(vc_s[:bs], hp), lfn, cache_kh, cache_vh, lf_pad, l, page_table, nch)
        o_c = o_c[:, :nch].reshape(bs, cw)
        pad_rows = lambda a: jnp.zeros((rs, a.shape[-1]), a.dtype).at[:bs].set(a.reshape(bs, -1))
        xs = _matmul([pad_rows(o_a), pad_rows(o_bm), pad_rows(o_c)], w_o, l, res=xs)

        h = _rmsnorm(xp, norm_ffn[l], BF16)
        hs = _rmsnorm(xs, norm_ffn[l], BF16)
        act, fnew_g, fnew_v, up_g, up_v = _ffn_up(h, hs, w_up, l, ffn_conv_w[l],
                                                  jnp.zeros((bp, ffn_conv_w.shape[1] - 1, 2 * f), F32), bp, t)
        xp = _matmul([act], w_d, l, res=xp)
        up = jnp.concatenate([up_g, up_v], axis=1)
        st = jnp.zeros((ffn_conv_w.shape[1] - 1, rs, 2 * f), F32).at[:, :bs].set(state_ffn_conv[l].transpose(1, 0, 2))
        act = _ffn_sample(up, st, ffn_conv_w[l])
        xs = _matmul([act], w_d, l, res=xs)
        outs_p[-1] += (jnp.concatenate([fnew_g, fnew_v], axis=-1),)
        fconv_new = jnp.concatenate([state_ffn_conv[l][:, 1:], up[:bs, None, :]], axis=1)
        outs_s.append((kc_s[:bs].reshape(bs, 1, nch, HEAD_DIM), vc_s[:bs].reshape(bs, 1, nch, HEAD_DIM), logf,
                       s_new, gconv_new, cconv_new, fconv_new))

    y_prompt = _rmsnorm(xp, norm_final, F32).reshape(bp, t, d)
    y_sample = _rmsnorm(xs, norm_final, F32)[:bs].reshape(bs, 1, d)
    stack = lambda outs, i: jnp.stack([o[i] for o in outs], axis=0)
    k_rows_p = k_all.transpose(0, 1, 3, 2, 4)
    v_rows_p = v_all.transpose(0, 1, 3, 2, 4)
    return ((y_prompt, y_sample, k_rows_p, v_rows_p) + tuple(stack(outs_p, i) for i in range(5))
            + tuple(stack(outs_s, i) for i in range(7)))
```

```python
import functools
import math

import jax
import jax.numpy as jnp
from jax import lax
from jax.experimental import pallas as pl
from jax.experimental.pallas import tpu as pltpu

EPS = 1e-6
HEAD_DIM = 128
GDN_CHUNK = 64
SUBLANES = 8
ROW_PAD = 16
VMEM_LIMIT = 56 * 2**20
FOX_PAGE_GROUP = 8
FFN_SUB_ROWS = 512

F32 = jnp.float32
BF16 = jnp.bfloat16
_HP = lax.Precision.HIGHEST


def _params(*sem):
    return pltpu.CompilerParams(dimension_semantics=sem, vmem_limit_bytes=VMEM_LIMIT)


def _pick(n, cands):
    for c in cands:
        if n % c == 0:
            return c
    return n


def _silu(x):
    return x * jax.nn.sigmoid(x)


def _softplus(x):
    return jnp.maximum(x, 0.0) + jnp.log(1.0 + jnp.exp(-jnp.abs(x)))


def _dot_t(a, b):
    return lax.dot_general(a, b, (((1,), (1,)), ((), ())), preferred_element_type=F32)


def _hp_dot(a, b):
    return jnp.dot(a, b, precision=_HP, preferred_element_type=F32)


def _rmsnorm_kernel(x_ref, g_ref, o_ref):
    x = x_ref[...]
    ms = jnp.mean(x * x, axis=-1, keepdims=True)
    o_ref[...] = (x * lax.rsqrt(ms + EPS) * g_ref[...]).astype(o_ref.dtype)


def _rmsnorm(x, g, out_dtype):
    m, d = x.shape
    tm = _pick(m, (256, 128, 64, 32, 16, 8))
    return pl.pallas_call(
        _rmsnorm_kernel,
        out_shape=jax.ShapeDtypeStruct((m, d), out_dtype),
        grid=(m // tm,),
        in_specs=[pl.BlockSpec((tm, d), lambda i: (i, 0)), pl.BlockSpec((1, d), lambda i: (0, 0))],
        out_specs=pl.BlockSpec((tm, d), lambda i: (i, 0)),
        compiler_params=_params("parallel"),
        name="rmsnorm",
    )(x, g.reshape(1, d))


def _mm_kernel(*refs, n_a, has_res):
    a_refs = refs[:n_a]
    w_refs = refs[n_a:2 * n_a]
    o_ref = refs[-1]
    acc = jnp.dot(a_refs[0][...], w_refs[0][0], preferred_element_type=F32)
    for a_ref, w_ref in zip(a_refs[1:], w_refs[1:]):
        acc = acc + jnp.dot(a_ref[...], w_ref[0], preferred_element_type=F32)
    if has_res:
        acc = refs[2 * n_a][...] + acc
    o_ref[...] = acc


def _matmul(a_list, w, l, res=None):
    m = a_list[0].shape[0]
    n = w.shape[2]
    k_total = sum(a.shape[1] for a in a_list)
    assert k_total == w.shape[1]
    tm = _pick(m, (1024, 512, 256, 128, 64, 32, 16)) if k_total <= 4096 else _pick(m, (512, 256, 128, 64, 32, 16))
    tn = _pick(n, (512, 256, 128))
    in_specs = [pl.BlockSpec((tm, a.shape[1]), lambda i, j: (i, 0)) for a in a_list]
    row = 0
    for a in a_list:
        in_specs.append(pl.BlockSpec((pl.Element(1), pl.Element(a.shape[1]), pl.Element(tn)),
                                     lambda i, j, row=row: (l, row, j * tn)))
        row += a.shape[1]
    args = list(a_list) + [w] * len(a_list)
    if res is not None:
        in_specs.append(pl.BlockSpec((tm, tn), lambda i, j: (i, j)))
        args.append(res)
    return pl.pallas_call(
        functools.partial(_mm_kernel, n_a=len(a_list), has_res=res is not None),
        out_shape=jax.ShapeDtypeStruct((m, n), F32),
        grid=(m // tm, n // tn),
        in_specs=in_specs,
        out_specs=pl.BlockSpec((tm, tn), lambda i, j: (i, j)),
        compiler_params=_params("parallel", "parallel"),
        name="matmul",
    )(*args)


CAST_ROWS = 512


def _cast_rows(dst_ref, c0, c1, load):
    k = dst_ref.shape[0]
    step = CAST_ROWS if k % CAST_ROWS == 0 else k

    def body(c, carry):
        rows = pl.ds(pl.multiple_of(c * step, step), step)
        dst_ref[rows, c0:c1] = load(rows).astype(BF16)
        return carry

    lax.fori_loop(0, k // step, body, 0)


def _wmm_kernel(*refs, shift, n_plain, has_a2, has_prev, heads_out, pair):
    refs = list(refs)
    a_ref = refs.pop(0)
    a2_ref = refs.pop(0) if has_a2 else None
    wa_ref = refs.pop(0)
    wb_ref = refs.pop(0)
    if has_prev:
        refs.pop(0)
    o_ref = refs.pop(0)
    o2_ref = refs.pop(0) if has_a2 else None
    wbf_ref = refs.pop(0)
    j = pl.program_id(0)
    i = pl.program_id(1)
    tn = wa_ref.shape[1]

    def plain():
        _cast_rows(wbf_ref, 0, tn, lambda rows: wa_ref[rows, :])

    def shifted():
        _cast_rows(wbf_ref, 0, tn,
                   lambda rows: jnp.concatenate([wa_ref[rows, :], wb_ref[rows, :]], axis=1)[:, shift:shift + tn])

    @pl.when(i == 0)
    def _():
        if pair:
            plain()
            _cast_rows(wbf_ref, tn, tn + wb_ref.shape[1], lambda rows: wb_ref[rows, :])
        elif shift == 0:
            plain()
        else:
            pl.when(j < n_plain)(plain)
            pl.when(j >= n_plain)(shifted)

    acc = jnp.dot(a_ref[...], wbf_ref[...], preferred_element_type=F32)
    if heads_out:
        for hh in range(tn // HEAD_DIM):
            o_ref[hh] = acc[:, hh * HEAD_DIM:(hh + 1) * HEAD_DIM]
    else:
        o_ref[...] = acc
    if has_a2:
        @pl.when(i == pl.num_programs(1) - 1)
        def _():
            o2_ref[...] = jnp.dot(a2_ref[...], wbf_ref[...], preferred_element_type=F32)


def _wmatmul_pair(a, w, l, blk_a, blk_b, tn, tm, a2):
    m, k = a.shape
    wide = tn + HEAD_DIM
    in_specs = [pl.BlockSpec((tm, k), lambda j, i: (i, 0))]
    args = [a]
    out_shape = [jax.ShapeDtypeStruct((m, wide), F32)]
    out_specs = [pl.BlockSpec((tm, wide), lambda j, i: (i, 0))]
    if a2 is not None:
        in_specs.append(pl.BlockSpec(a2.shape, lambda j, i: (0, 0)))
        args.append(a2)
        out_shape.append(jax.ShapeDtypeStruct((a2.shape[0], wide), F32))
        out_specs.append(pl.BlockSpec((a2.shape[0], wide), lambda j, i: (0, 0)))
    in_specs += [pl.BlockSpec((None, k, tn), lambda j, i: (l, 0, blk_a)),
                 pl.BlockSpec((None, k, HEAD_DIM), lambda j, i: (l, 0, blk_b))]
    args += [w, w]
    res = pl.pallas_call(
        functools.partial(_wmm_kernel, shift=0, n_plain=1, has_a2=a2 is not None, pair=True, has_prev=False,
                          heads_out=False),
        out_shape=out_shape,
        grid=(1, m // tm),
        in_specs=in_specs,
        out_specs=out_specs,
        scratch_shapes=[pltpu.VMEM((k, wide), BF16)],
        compiler_params=_params("arbitrary", "arbitrary"),
        name="wmatmul_pair",
    )(*args)
    return res[0] if a2 is None else tuple(res)


def _wmatmul(a, w, l, c0, ncols, tn, plain_cols=0, shift=0, a2=None, heads=None, pair_col=None):
    m, k = a.shape
    assert c0 % tn == 0 and ncols % tn == 0 and plain_cols % tn == 0 and tn % HEAD_DIM == 0 and shift < HEAD_DIM
    tm = _pick(m if heads is None else heads[1], (1024, 512, 256, 128, 64, 32, 16))
    jb = c0 // tn
    n_plain = max(0, min(ncols, plain_cols - c0) // tn) if shift else ncols // tn
    lanes_per = tn // HEAD_DIM
    last_blk = -(-w.shape[2] // HEAD_DIM) - 1
    pair = pair_col is not None
    if pair:
        assert ncols == tn and shift == 0 and heads is None and pair_col % HEAD_DIM == 0
        return _wmatmul_pair(a, w, l, jb, pair_col // HEAD_DIM, tn, tm, a2)
    in_specs = [pl.BlockSpec((tm, k), lambda j, i: (i, 0))]
    args = [a]
    if a2 is not None:
        in_specs.append(pl.BlockSpec(a2.shape, lambda j, i: (0, 0)))
        args.append(a2)
    in_specs += [pl.BlockSpec((None, k, tn), lambda j, i: (l, 0, jb + j)),
                 pl.BlockSpec((None, k, HEAD_DIM),
                              lambda j, i: (l, 0, jnp.minimum((jb + j + 1) * lanes_per, last_blk)))]
    args += [w, w]
    aliases = {}
    if heads is None:
        out_shape = [jax.ShapeDtypeStruct((m, ncols), F32)]
        out_specs = [pl.BlockSpec((tm, tn), lambda j, i: (i, j))]
    else:
        bsz, t, prev = heads
        assert t % tm == 0
        tpb = t // tm
        out_shape = [jax.ShapeDtypeStruct((w.shape[0], bsz, ncols // HEAD_DIM, t, HEAD_DIM), F32)]
        out_specs = [pl.BlockSpec((None, None, lanes_per, tm, HEAD_DIM),
                                  lambda j, i: (l, i // tpb, j, i % tpb, 0))]
        if prev is not None:
            in_specs.append(pl.BlockSpec(memory_space=pl.ANY))
            aliases = {len(args): 0}
            args.append(prev)
    if a2 is not None:
        out_shape.append(jax.ShapeDtypeStruct((a2.shape[0], ncols), F32))
        out_specs.append(pl.BlockSpec((a2.shape[0], tn), lambda j, i: (0, j)))
    res = pl.pallas_call(
        functools.partial(_wmm_kernel, shift=shift, n_plain=n_plain, has_a2=a2 is not None, pair=False,
                          has_prev=heads is not None and heads[2] is not None, heads_out=heads is not None),
        out_shape=out_shape,
        grid=(ncols // tn, m // tm),
        in_specs=in_specs,
        out_specs=out_specs,
        scratch_shapes=[pltpu.VMEM((k, tn), BF16)],
        input_output_aliases=aliases,
        compiler_params=_params("arbitrary", "arbitrary"),
        name="wmatmul",
    )(*args)
    return res[0] if a2 is None else tuple(res)


def _gates_kernel(ps_ref, pf_ref, alog_ref, dtb_ref, bf_ref, gb_ref, lf_ref, c_ref, carry_ref, *, nvh, nch):
    t = pl.program_id(1)
    ps = ps_ref[...]
    tt = ps.shape[0]
    b = ps[:, 0:nvh]
    a = ps[:, nvh:2 * nvh]
    f = pf_ref[:, 2 * nvh:2 * nvh + nch]
    gb_ref[:, 0:nvh] = -jnp.exp(alog_ref[...]) * _softplus(a + dtb_ref[...])
    gb_ref[:, nvh:2 * nvh] = jax.nn.sigmoid(b)
    lf = -_softplus(-(f + bf_ref[...]))
    lf_ref[...] = lf

    @pl.when(t == 0)
    def _():
        carry_ref[...] = jnp.zeros_like(carry_ref)

    ii = lax.broadcasted_iota(jnp.int32, (tt, tt), 0)
    jj = lax.broadcasted_iota(jnp.int32, (tt, tt), 1)
    tri = (ii >= jj).astype(F32)
    c = _hp_dot(tri, lf) + carry_ref[...]
    c_ref[...] = c
    carry_ref[...] = c[tt - 1:tt, :]


def _gates(pg, a_log, dt_bias, b_f, bsz, t):
    nvh = a_log.shape[0]
    nch = b_f.shape[0]
    tt = _pick(t, (256, 128, 64, 32, 16, 8))
    nt = t // tt
    pw = HEAD_DIM
    ps = pf = pg
    row = lambda b, i: (b * nt + i, 0)
    fix = lambda b, i: (0, 0)
    return pl.pallas_call(
        functools.partial(_gates_kernel, nvh=nvh, nch=nch),
        out_shape=(jax.ShapeDtypeStruct((bsz * t, 2 * nvh), F32),
                   jax.ShapeDtypeStruct((bsz * t, nch), F32),
                   jax.ShapeDtypeStruct((bsz * t, nch), F32)),
        grid=(bsz, nt),
        in_specs=[pl.BlockSpec((tt, pw), row), pl.BlockSpec((tt, pw), lambda b, i: (b * nt + i, 1)),
                  pl.BlockSpec((1, nvh), fix), pl.BlockSpec((1, nvh), fix), pl.BlockSpec((1, nch), fix)],
        out_specs=(pl.BlockSpec((tt, 2 * nvh), row), pl.BlockSpec((tt, nch), row), pl.BlockSpec((tt, nch), row)),
        scratch_shapes=[pltpu.VMEM((1, nch), F32)],
        compiler_params=_params("parallel", "arbitrary"),
        name="gates",
    )(ps, pf, a_log.reshape(1, nvh), dt_bias.reshape(1, nvh), b_f.reshape(1, nch))


def _gdn_prep_kernel(x_ref, w_ref, prev_ref, qk_ref, v_ref, cnew_ref, buf_ref, *, tt, kw, width):
    t = pl.program_id(1)
    halo = width - 1
    base = SUBLANES
    lo = base - halo

    @pl.when(t == 0)
    def _():
        buf_ref[lo:base, :] = prev_ref[0]

    @pl.when(t > 0)
    def _():
        buf_ref[lo:base, :] = buf_ref[lo + tt:base + tt, :]

    buf_ref[base:base + tt, :] = x_ref[...]
    chans = x_ref.shape[1]
    for c0 in range(0, chans, HEAD_DIM):
        cs = slice(c0, c0 + HEAD_DIM)
        y = buf_ref[lo:lo + tt, cs] * w_ref[0:1, cs]
        for i in range(1, width):
            y = y + buf_ref[lo + i:lo + i + tt, cs] * w_ref[i:i + 1, cs]
        y = _silu(y)
        if c0 < 2 * kw:
            y = y * lax.rsqrt(jnp.sum(y * y, axis=-1, keepdims=True) + EPS)
            if c0 < kw:
                y = y * HEAD_DIM ** -0.5
            qk_ref[:, cs] = y
        else:
            v_ref[:, c0 - 2 * kw:c0 - 2 * kw + HEAD_DIM] = y

    @pl.when(t == pl.num_programs(1) - 1)
    def _():
        cnew_ref[0] = buf_ref[lo + tt:base + tt, :]


def _gdn_prep(pm, conv_w, prev, bsz, t, kw, vw):
    width, chans = conv_w.shape
    tt = _pick(t, (256, 128, 64))
    nt = t // tt
    row = lambda b, i: (b * nt + i, 0)
    return pl.pallas_call(
        functools.partial(_gdn_prep_kernel, tt=tt, kw=kw, width=width),
        out_shape=(jax.ShapeDtypeStruct((bsz * t, 2 * kw), F32),
                   jax.ShapeDtypeStruct((bsz * t, vw), F32),
                   jax.ShapeDtypeStruct((bsz, width - 1, chans), F32)),
        grid=(bsz, nt),
        in_specs=[pl.BlockSpec((tt, chans), row),
                  pl.BlockSpec((width, chans), lambda b, i: (0, 0)),
                  pl.BlockSpec((1, width - 1, chans), lambda b, i: (b, 0, 0))],
        out_specs=(pl.BlockSpec((tt, 2 * kw), row), pl.BlockSpec((tt, vw), row),
                   pl.BlockSpec((1, width - 1, chans), lambda b, i: (b, 0, 0))),
        scratch_shapes=[pltpu.VMEM((SUBLANES + tt, chans), F32)],
        compiler_params=_params("parallel", "arbitrary"),
        name="gdn_prep",
    )(pm, conv_w, prev)


def _bdot(a, b):
    return jnp.dot(a.astype(BF16), b.astype(BF16), preferred_element_type=F32)


def _gdn_chunk_kernel(qk_ref, v_ref, z_ref, gb_ref, grow_ref, s0_ref, nw_ref, o_ref, sout_ref, s_ref,
                      *, c, nkh, nvh, kw):
    ci = pl.program_id(1)

    @pl.when(ci == 0)
    def _():
        s_ref[...] = s0_ref[0]

    rep = nvh // nkh
    ii = lax.broadcasted_iota(jnp.int32, (c, c), 0)
    jj = lax.broadcasted_iota(jnp.int32, (c, c), 1)
    lower = ii >= jj
    strict = ii > jj
    same_blk = {}
    size = SUBLANES
    while size <= c:
        sh = int(math.log2(size))
        same_blk[size] = (ii >> sh) == (jj >> sh)
        size *= 2
    gb = gb_ref[...]
    grow = grow_ref[0]
    nw = nw_ref[...]
    heads = range(nvh)
    hsl = [slice(h * HEAD_DIM, (h + 1) * HEAD_DIM) for h in heads]
    q = [qk_ref[:, kh * HEAD_DIM:(kh + 1) * HEAD_DIM] for kh in range(nkh)]
    k = [qk_ref[:, kw + kh * HEAD_DIM:kw + (kh + 1) * HEAD_DIM] for kh in range(nkh)]
    v = [v_ref[:, hsl[h]] for h in heads]
    s_old = [s_ref[h] for h in heads]
    k16 = [x.astype(BF16) for x in k]
    kk = [_dot_t(k16[i], k16[i]) for i in range(nkh)]
    qk = [_dot_t(q[i].astype(BF16), k16[i]) for i in range(nkh)]
    beta = [gb[:, nvh + h:nvh + h + 1] for h in heads]
    gc_col = [jnp.sum(jnp.where(lower, grow[h:h + 1, :], 0.0), axis=1, keepdims=True) for h in heads]
    gc_row = [jnp.sum(jnp.where(ii <= jj, gb[:, h:h + 1], 0.0), axis=0, keepdims=True) for h in heads]
    decay = [jnp.exp(jnp.where(lower, gc_col[h] - gc_row[h], -1e30)) for h in heads]
    a = [jnp.where(strict, kk[h // rep] * beta[h] * decay[h], 0.0) for h in heads]
    d = [jnp.where(same_blk[SUBLANES], a[h], 0.0) for h in heads]
    d2 = [_bdot(d[h], d[h]) for h in heads]
    d4 = [_bdot(d2[h], d2[h]) for h in heads]
    x = [d2[h] - d[h] - _bdot(d2[h], d[h]) for h in heads]
    x = [x[h] + d4[h] + _bdot(d4[h], x[h]) for h in heads]
    size = SUBLANES
    while size < c:
        r = [jnp.where(same_blk[2 * size], a[h], 0.0) - jnp.where(same_blk[size], a[h], 0.0) for h in heads]
        y = [r[h] + _bdot(x[h], r[h]) for h in heads]
        x = [x[h] - y[h] - _bdot(y[h], x[h]) for h in heads]
        size *= 2
    egc = [jnp.exp(gc_col[h]) for h in heads]
    rhs = [jnp.concatenate([v[h] * beta[h], k[h // rep] * (beta[h] * egc[h])], axis=1) for h in heads]
    uw = [rhs[h] + _bdot(x[h], rhs[h]) for h in heads]
    s16 = [s_old[h].astype(BF16) for h in heads]
    v_new = [uw[h][:, :HEAD_DIM] - jnp.dot(uw[h][:, HEAD_DIM:].astype(BF16), s16[h], preferred_element_type=F32)
             for h in heads]
    vn16 = [v_new[h].astype(BF16) for h in heads]
    o = [jnp.dot((q[h // rep] * egc[h]).astype(BF16), s16[h], preferred_element_type=F32)
         + jnp.dot((qk[h // rep] * decay[h]).astype(BF16), vn16[h], preferred_element_type=F32) for h in heads]
    g_last = [gc_col[h][c - 1:c, :] for h in heads]
    kd = [(k[h // rep] * jnp.exp(g_last[h] - gc_col[h])).astype(BF16) for h in heads]
    s_new = [s_old[h] * jnp.exp(g_last[h])
             + lax.dot_general(kd[h], vn16[h], (((0,), (0,)), ((), ())), preferred_element_type=F32) for h in heads]
    on = [o[h] * lax.rsqrt(jnp.mean(o[h] * o[h], axis=-1, keepdims=True) + EPS) * nw for h in heads]
    for h in heads:
        s_ref[h] = s_new[h]
        o_ref[:, hsl[h]] = (on[h] * _silu(z_ref[:, hsl[h]])).astype(o_ref.dtype)

    @pl.when(ci == pl.num_programs(1) - 1)
    def _():
        sout_ref[0] = s_ref[...]


def _gdn_chunks(qk, v, pm, z_blk, gb, grow, s0, norm_w, bsz, t, nkh, nvh):
    c = GDN_CHUNK
    n = t // c
    kw = nkh * HEAD_DIM
    vw = nvh * HEAD_DIM
    row = lambda b, i: (b * n + i, 0)
    return pl.pallas_call(
        functools.partial(_gdn_chunk_kernel, c=c, nkh=nkh, nvh=nvh, kw=kw),
        out_shape=(jax.ShapeDtypeStruct((bsz * t, vw), BF16),
                   jax.ShapeDtypeStruct((bsz, nvh, HEAD_DIM, HEAD_DIM), F32)),
        grid=(bsz, n),
        in_specs=[pl.BlockSpec((c, 2 * kw), row),
                  pl.BlockSpec((c, vw), row),
                  pl.BlockSpec((c, vw), lambda b, i: (b * n + i, z_blk)),
                  pl.BlockSpec((c, 2 * nvh), row),
                  pl.BlockSpec((1, nvh, c), lambda b, i: (b * n + i, 0, 0)),
                  pl.BlockSpec((1, nvh, HEAD_DIM, HEAD_DIM), lambda b, i: (b, 0, 0, 0)),
                  pl.BlockSpec((1, HEAD_DIM), lambda b, i: (0, 0))],
        out_specs=(pl.BlockSpec((c, vw), row),
                   pl.BlockSpec((1, nvh, HEAD_DIM, HEAD_DIM), lambda b, i: (b, 0, 0, 0))),
        scratch_shapes=[pltpu.VMEM((nvh, HEAD_DIM, HEAD_DIM), F32)],
        compiler_params=_params("parallel", "arbitrary"),
        name="gdn_chunks",
    )(qk, v, pm, gb, grow, s0, norm_w.reshape(1, HEAD_DIM))


def _conf_kernel(val0_ref, val1_ref, gate0_ref, gate1_ref, w_ref, b_ref, g_ref, beta_ref, prev_ref, o_ref, cnew_ref,
                 buf_ref, y_ref, *, tt, width, base):
    t = pl.program_id(1)
    halo = width - 1
    lo = base - halo
    half = val0_ref.shape[1]
    ch = 2 * half

    @pl.when(t == 0)
    def _():
        buf_ref[lo:base, :] = prev_ref[0]
        buf_ref[base + tt:base + tt + SUBLANES, :] = jnp.zeros((SUBLANES, ch), F32)

    @pl.when(t > 0)
    def _():
        buf_ref[lo:base, :] = buf_ref[lo + tt:base + tt, :]

    buf_ref[base:base + tt, 0:half] = val0_ref[...] * jax.nn.sigmoid(gate0_ref[...])
    buf_ref[base:base + tt, half:2 * half] = val1_ref[...] * jax.nn.sigmoid(gate1_ref[...])
    cchunk = 2 * HEAD_DIM if ch % (2 * HEAD_DIM) == 0 else ch
    for c0 in range(0, ch, cchunk):
        cs = slice(c0, c0 + cchunk)
        y = None
        for r in range(SUBLANES):
            z = None
            for i in range(width):
                if (lo + i) % SUBLANES != r:
                    continue
                a0 = (lo + i) - r
                term = buf_ref[a0:a0 + tt + SUBLANES, cs] * w_ref[i:i + 1, cs]
                z = term if z is None else z + term
            if z is not None:
                zr = z[r:r + tt, :]
                y = zr if y is None else y + zr
        y_ref[:, cs] = y
    y = y_ref[...] + b_ref[...]
    yc = y - jnp.mean(y, axis=-1, keepdims=True)
    yn = yc * lax.rsqrt(jnp.mean(yc * yc, axis=-1, keepdims=True) + EPS)
    o_ref[...] = _silu(yn * g_ref[...] + beta_ref[...]).astype(o_ref.dtype)

    @pl.when(t == pl.num_programs(1) - 1)
    def _():
        cnew_ref[0] = buf_ref[lo + tt:base + tt, :]


def _conformer(pm, val_blk, dw_w, dw_b, ln_g, ln_b, prev, bsz, t):
    width, ch = dw_w.shape
    tt = _pick(t, (256, 128, 64))
    nt = t // tt
    base = -(-(width - 1) // SUBLANES) * SUBLANES
    fix = lambda b, i: (0, 0)
    half = ch // 2
    part = lambda p: pl.BlockSpec((tt, half), lambda b, i: (b * nt + i, val_blk + p))
    return pl.pallas_call(
        functools.partial(_conf_kernel, tt=tt, width=width, base=base),
        out_shape=(jax.ShapeDtypeStruct((bsz * t, ch), BF16),
                   jax.ShapeDtypeStruct((bsz, width - 1, ch), F32)),
        grid=(bsz, nt),
        in_specs=[part(0), part(1), part(2), part(3),
                  pl.BlockSpec((width, ch), fix), pl.BlockSpec((1, ch), fix), pl.BlockSpec((1, ch), fix),
                  pl.BlockSpec((1, ch), fix),
                  pl.BlockSpec((1, width - 1, ch), lambda b, i: (b, 0, 0))],
        out_specs=(pl.BlockSpec((tt, ch), lambda b, i: (b * nt + i, 0)),
                   pl.BlockSpec((1, width - 1, ch), lambda b, i: (b, 0, 0))),
        scratch_shapes=[pltpu.VMEM((base + tt + SUBLANES, ch), F32), pltpu.VMEM((tt, ch), F32)],
        compiler_params=_params("parallel", "arbitrary"),
        name="conformer",
    )(pm, pm, pm, pm, dw_w, dw_b.reshape(1, ch), ln_g.reshape(1, ch), ln_b.reshape(1, ch), prev)


def _fox_prompt_kernel(q_ref, k_ref, v_ref, ck_ref, o_ref, *, tq):
    nq = q_ref.shape[0] // tq
    tiles = [slice(i * tq, (i + 1) * tq) for i in range(nq)]
    k16 = [k_ref[ts, :].astype(BF16) for ts in tiles]
    v16 = [v_ref[ts, :].astype(BF16) for ts in tiles]
    ii = lax.broadcasted_iota(jnp.int32, (tq, tq), 0)
    jj = lax.broadcasted_iota(jnp.int32, (tq, tq), 1)
    for qi in range(nq):
        q = (q_ref[tiles[qi], :] * HEAD_DIM ** -0.5).astype(BF16)
        m = l = acc = None
        for ki in range(qi + 1):
            s = _dot_t(q, k16[ki]) - ck_ref[0, ki]
            if ki == qi:
                s = jnp.where(jj <= ii, s, -1e30)
            s_max = jnp.max(s, axis=-1, keepdims=True)
            if ki == 0:
                m = s_max
                p = jnp.exp(s - m)
                l = jnp.sum(p, axis=-1, keepdims=True)
                acc = jnp.dot(p.astype(BF16), v16[ki], preferred_element_type=F32)
            else:
                m_new = jnp.maximum(m, s_max)
                alpha = jnp.exp(m - m_new)
                p = jnp.exp(s - m_new)
                l = alpha * l + jnp.sum(p, axis=-1, keepdims=True)
                acc = alpha * acc + jnp.dot(p.astype(BF16), v16[ki], preferred_element_type=F32)
                m = m_new
        o_ref[tiles[qi], :] = (acc / l).astype(o_ref.dtype)


def _fox_prompt(pm, q_blk, kh, vh, l, c_row, bsz, t, nch, tq):
    nq = t // tq
    kv_spec = pl.BlockSpec((None, None, None, t, HEAD_DIM), lambda b, h: (l, b, h, 0, 0))
    return pl.pallas_call(
        functools.partial(_fox_prompt_kernel, tq=tq),
        out_shape=jax.ShapeDtypeStruct((bsz * t, nch * HEAD_DIM), BF16),
        grid=(bsz, nch),
        in_specs=[pl.BlockSpec((t, HEAD_DIM), lambda b, h: (b, q_blk + h)),
                  kv_spec, kv_spec,
                  pl.BlockSpec((1, nq, 1, tq), lambda b, h: (b * nch + h, 0, 0, 0))],
        out_specs=pl.BlockSpec((t, HEAD_DIM), lambda b, h: (b, h)),
        compiler_params=_params("parallel", "parallel"),
        name="fox_prompt",
    )(pm, kh, vh, c_row)


def _ffn_up_kernel(a_ref, a2_ref, wg_ref, wv_ref, cwg_ref, cwv_ref, sg_ref, sv_ref, act_ref, ng_ref, nv_ref,
                   u2g_ref, u2v_ref, wcat_ref, *buf_refs, tm, tn, tpb, width):
    i = pl.program_id(1)
    halo = width - 1
    base = SUBLANES
    lo = base - halo
    n_sub = len(buf_refs)
    sub = tm // n_sub
    last = buf_refs[n_sub - 1]

    @pl.when(i == 0)
    def _():
        _cast_rows(wcat_ref, 0, tn, lambda rows: wg_ref[rows, :])
        _cast_rows(wcat_ref, tn, 2 * tn, lambda rows: wv_ref[rows, :])

    first = (i % tpb) == 0

    @pl.when(first)
    def _():
        buf_refs[0][lo:base, 0:tn] = sg_ref[0]
        buf_refs[0][lo:base, tn:2 * tn] = sv_ref[0]

    @pl.when(jnp.logical_not(first))
    def _():
        buf_refs[0][lo:base, :] = last[lo + sub:base + sub, :]

    def conv_rows(r):
        buf = buf_refs[r]
        yg = buf[lo:lo + sub, 0:tn] * cwg_ref[0:1, :]
        yv = buf[lo:lo + sub, tn:2 * tn] * cwv_ref[0:1, :]
        for t in range(1, width):
            yg = yg + buf[lo + t:lo + t + sub, 0:tn] * cwg_ref[t:t + 1, :]
            yv = yv + buf[lo + t:lo + t + sub, tn:2 * tn] * cwv_ref[t:t + 1, :]
        act_ref[r * sub:(r + 1) * sub, :] = (_silu(yg) * yv).astype(act_ref.dtype)

    for r in range(n_sub):
        buf_refs[r][base:base + sub, :] = jnp.dot(a_ref[r * sub:(r + 1) * sub, :], wcat_ref[...],
                                                  preferred_element_type=F32)
        if r + 1 < n_sub:
            buf_refs[r + 1][lo:base, :] = buf_refs[r][lo + sub:base + sub, :]
        if r > 0:
            conv_rows(r - 1)
    conv_rows(n_sub - 1)
    ng_ref[0] = last[lo + sub:base + sub, 0:tn]
    nv_ref[0] = last[lo + sub:base + sub, tn:2 * tn]

    @pl.when(i == pl.num_programs(1) - 1)
    def _():
        up2 = jnp.dot(a2_ref[...], wcat_ref[...], preferred_element_type=F32)
        u2g_ref[...] = up2[:, 0:tn]
        u2v_ref[...] = up2[:, tn:2 * tn]


def _ffn_up(h, h2, w_up, l, conv_w, state, bsz, t):
    m, d = h.shape
    r2 = h2.shape[0]
    f2 = w_up.shape[2]
    f = f2 // 2
    width = conv_w.shape[0]
    tm = _pick(t, (1024, 512, 256, 128, 64))
    tn = _pick(f, (256, 128))
    nj = f // tn
    tpb = t // tm
    n_sub = max(1, tm // FFN_SUB_ROWS)
    return pl.pallas_call(
        functools.partial(_ffn_up_kernel, tm=tm, tn=tn, tpb=tpb, width=width),
        out_shape=(jax.ShapeDtypeStruct((m, f), BF16),
                   jax.ShapeDtypeStruct((bsz, width - 1, f), F32),
                   jax.ShapeDtypeStruct((bsz, width - 1, f), F32),
                   jax.ShapeDtypeStruct((r2, f), F32),
                   jax.ShapeDtypeStruct((r2, f), F32)),
        grid=(nj, m // tm),
        in_specs=[pl.BlockSpec((tm, d), lambda j, i: (i, 0)),
                  pl.BlockSpec((r2, d), lambda j, i: (0, 0)),
                  pl.BlockSpec((None, d, tn), lambda j, i: (l, 0, j)),
                  pl.BlockSpec((None, d, tn), lambda j, i: (l, 0, j + nj)),
                  pl.BlockSpec((width, tn), lambda j, i: (0, j)),
                  pl.BlockSpec((width, tn), lambda j, i: (0, j + nj)),
                  pl.BlockSpec((1, width - 1, tn), lambda j, i: (i // tpb, 0, j)),
                  pl.BlockSpec((1, width - 1, tn), lambda j, i: (i // tpb, 0, j + nj))],
        out_specs=(pl.BlockSpec((tm, tn), lambda j, i: (i, j)),
                   pl.BlockSpec((1, width - 1, tn), lambda j, i: (i // tpb, 0, j)),
                   pl.BlockSpec((1, width - 1, tn), lambda j, i: (i // tpb, 0, j)),
                   pl.BlockSpec((r2, tn), lambda j, i: (0, j)),
                   pl.BlockSpec((r2, tn), lambda j, i: (0, j))),
        scratch_shapes=[pltpu.VMEM((d, 2 * tn), BF16)]
        + [pltpu.VMEM((SUBLANES + tm // n_sub, 2 * tn), F32) for _ in range(n_sub)],
        compiler_params=_params("arbitrary", "arbitrary"),
        name="ffn_up",
    )(h, h2, w_up, w_up, conv_w, conv_w, state, state)


def _col_bcast(row):
    n = row.shape[1]
    return jnp.transpose(jnp.broadcast_to(row, (n, n)))


def _sample_mix_kernel(pm_ref, ps_ref, pf_ref, gprev_ref, sprev_ref, cprev_ref, gw_ref, alog_ref, dtb_ref, nw_ref,
                       cw_ref, cb_ref, lg_ref, lb_ref, bf_ref,
                       oa_ref, ob_ref, snew_ref, gnew_ref, cnew_ref, lf_ref,
                       *, nkh, nvh, nch, z_off, glu_off, bch):
    kw = nkh * HEAD_DIM
    vw = nvh * HEAD_DIM
    aqkv = 2 * kw + vw
    rep = nvh // nkh
    gwidth = gw_ref.shape[0]
    x = pm_ref[0, :, 0:aqkv]
    gprev = gprev_ref[0]
    y = jnp.sum(gprev * gw_ref[0:gwidth - 1, :], axis=0, keepdims=True) + x * gw_ref[gwidth - 1:gwidth, :]
    gnew_ref[0, 0:gwidth - 2, :] = gprev[1:gwidth - 1, :]
    gnew_ref[0, gwidth - 2:gwidth - 1, :] = x
    y = _silu(y)
    ps = ps_ref[0]
    beta_all = jax.nn.sigmoid(ps[:, 0:nvh])
    g_all = -jnp.exp(alog_ref[...]) * _softplus(ps[:, nvh:2 * nvh] + dtb_ref[...])
    lf_ref[0] = -_softplus(-(pf_ref[0, :, 2 * nvh:2 * nvh + nch] + bf_ref[...]))
    nw = nw_ref[...]
    for kh in range(nkh):
        q = y[:, kh * HEAD_DIM:(kh + 1) * HEAD_DIM]
        k = y[:, kw + kh * HEAD_DIM:kw + (kh + 1) * HEAD_DIM]
        q = q * lax.rsqrt(jnp.sum(q * q, axis=-1, keepdims=True) + EPS) * HEAD_DIM ** -0.5
        k = k * lax.rsqrt(jnp.sum(k * k, axis=-1, keepdims=True) + EPS)
        q_cols = _col_bcast(q)
        k_cols = _col_bcast(k)
        qk = jnp.sum(q * k, axis=-1, keepdims=True)
        for r in range(rep):
            h = kh * rep + r
            v = y[:, 2 * kw + h * HEAD_DIM:2 * kw + (h + 1) * HEAD_DIM]
            beta = beta_all[:, h:h + 1]
            eg = jnp.exp(g_all[:, h:h + 1])
            s = sprev_ref[0, h]
            k_s = jnp.sum(k_cols * s, axis=0, keepdims=True)
            q_s = jnp.sum(q_cols * s, axis=0, keepdims=True)
            v_new = beta * v - (beta * eg) * k_s
            o = eg * q_s + qk * v_new
            snew_ref[0, h] = s * eg + k_cols * v_new
            on = o * lax.rsqrt(jnp.mean(o * o, axis=-1, keepdims=True) + EPS) * nw
            z = pm_ref[0, :, z_off + h * HEAD_DIM:z_off + (h + 1) * HEAD_DIM]
            oa_ref[0, :, h * HEAD_DIM:(h + 1) * HEAD_DIM] = (on * _silu(z)).astype(oa_ref.dtype)
    cwidth = cw_ref.shape[0]
    val = pm_ref[0, :, glu_off:glu_off + bch]
    gate = pm_ref[0, :, glu_off + bch:glu_off + 2 * bch]
    u = val * jax.nn.sigmoid(gate)
    cprev = cprev_ref[0]
    yc = (jnp.sum(cprev * cw_ref[0:cwidth - 1, :], axis=0, keepdims=True) + u * cw_ref[cwidth - 1:cwidth, :]
          + cb_ref[...])
    cnew_ref[0, 0:cwidth - 2, :] = cprev[1:cwidth - 1, :]
    cnew_ref[0, cwidth - 2:cwidth - 1, :] = u
    yc = yc - jnp.mean(yc, axis=-1, keepdims=True)
    yn = yc * lax.rsqrt(jnp.mean(yc * yc, axis=-1, keepdims=True) + EPS)
    ob_ref[0] = _silu(yn * lg_ref[...] + lb_ref[...]).astype(ob_ref.dtype)


def _sample_mix(pm, pg, gprev, sprev, cprev, gw, a_log, dt_bias, nw, cw, cb, lg, lb, b_f, nkh, z_off, glu_off):
    bsz = gprev.shape[0]
    nvh = a_log.shape[0]
    nch = b_f.shape[0]
    bch = cw.shape[1]
    vw = nvh * HEAD_DIM
    pmw = pm.shape[1]
    psw = HEAD_DIM
    pm3 = pm[:bsz].reshape(bsz, 1, pmw)
    ps3 = pg[:bsz, 0:HEAD_DIM].reshape(bsz, 1, psw)
    pf3 = pg[:bsz, HEAD_DIM:2 * HEAD_DIM].reshape(bsz, 1, psw)
    per_b = lambda *blk: pl.BlockSpec((1,) + blk, lambda b: (b,) + (0,) * len(blk))
    fixed = lambda a: pl.BlockSpec(a.shape, lambda b: (0,) * a.ndim)
    consts = [gw, a_log.reshape(1, nvh), dt_bias.reshape(1, nvh), nw.reshape(1, HEAD_DIM), cw, cb.reshape(1, bch),
              lg.reshape(1, bch), lb.reshape(1, bch), b_f.reshape(1, nch)]
    return pl.pallas_call(
        functools.partial(_sample_mix_kernel, nkh=nkh, nvh=nvh, nch=nch, z_off=z_off, glu_off=glu_off, bch=bch),
        out_shape=(jax.ShapeDtypeStruct((bsz, 1, vw), BF16),
                   jax.ShapeDtypeStruct((bsz, 1, bch), BF16),
                   jax.ShapeDtypeStruct(sprev.shape, F32),
                   jax.ShapeDtypeStruct(gprev.shape, F32),
                   jax.ShapeDtypeStruct(cprev.shape, F32),
                   jax.ShapeDtypeStruct((bsz, 1, nch), F32)),
        grid=(bsz,),
        in_specs=[per_b(1, pmw), per_b(1, psw), per_b(1, psw), per_b(*gprev.shape[1:]), per_b(*sprev.shape[1:]),
                  per_b(*cprev.shape[1:])] + [fixed(a) for a in consts],
        out_specs=(per_b(1, vw), per_b(1, bch), per_b(*sprev.shape[1:]), per_b(*gprev.shape[1:]),
                   per_b(*cprev.shape[1:]), per_b(1, nch)),
        compiler_params=_params("parallel"),
        name="sample_mix",
    )(pm3, ps3, pf3, gprev, sprev, cprev, *consts)


def _fox_sample_kernel(pt_ref, q_ref, kn_ref, vn_ref, lfn_ref, *refs, nch, hp, group):
    kp_refs = refs[0:group]
    vp_refs = refs[group:2 * group]
    lfp_refs = refs[2 * group:3 * group]
    o_ref, qt_ref, m_ref, l_ref, r_ref, acc_ref = refs[3 * group:]
    pi = pl.program_id(1)
    page = kp_refs[0].shape[1]
    scale = HEAD_DIM ** -0.5

    @pl.when(pi == 0)
    def _():
        q = q_ref[0]
        qt_ref[...] = jnp.transpose(q).astype(BF16)
        m_ref[...] = jnp.sum(jnp.transpose(q * kn_ref[0]), axis=0, keepdims=True) * scale
        l_ref[...] = jnp.ones_like(l_ref)
        r_ref[...] = lfn_ref[0]
        acc_ref[...] = vn_ref[0]

    jj = lax.broadcasted_iota(jnp.int32, (page, page), 0)
    mm = lax.broadcasted_iota(jnp.int32, (page, page), 1)
    later = (mm > jj).astype(BF16)
    lanes = lax.broadcasted_iota(jnp.int32, (page, HEAD_DIM), 1)
    slots = range(group)
    lf = [lfp_refs[g][...] for g in slots]
    lf_hi = [x.astype(BF16) for x in lf]
    lf_lo = [(lf[g] - lf_hi[g].astype(F32)).astype(BF16) for g in slots]
    inner = [jnp.dot(later, lf_hi[g], preferred_element_type=F32)
             + jnp.dot(later, lf_lo[g], preferred_element_type=F32) for g in slots]
    total = [jnp.sum(x, axis=0, keepdims=True) for x in lf]
    r_after = [r_ref[...]]
    for g in slots:
        r_after.append(r_after[g] + total[g])
    qt = qt_ref[...]
    s = []
    for g in slots:
        sg = jnp.zeros((page, HEAD_DIM), F32)
        for h in range(nch):
            sg = jnp.where(lanes == h, jnp.dot(kp_refs[g][h].astype(BF16), qt, preferred_element_type=F32), sg)
        s.append(sg * scale + (r_after[g] + inner[g]))
    m_old = m_ref[...]
    m_new = m_old
    for g in slots:
        m_new = jnp.maximum(m_new, jnp.max(s[g], axis=0, keepdims=True))
    alpha = jnp.exp(m_old - m_new)
    p = [jnp.exp(s[g] - m_new) for g in slots]
    l_new = alpha * l_ref[...]
    for g in slots:
        l_new = l_new + jnp.sum(p[g], axis=0, keepdims=True)
    l_ref[...] = l_new
    m_ref[...] = m_new
    r_ref[...] = r_after[group]
    p_t = [jnp.transpose(p[g])[0:hp, :].astype(BF16) for g in slots]
    rows = lax.broadcasted_iota(jnp.int32, (hp, HEAD_DIM), 0)
    upd = jnp.zeros((hp, HEAD_DIM), F32)
    for h in range(nch):
        oh = jnp.dot(p_t[0], vp_refs[0][h].astype(BF16), preferred_element_type=F32)
        for g in range(1, group):
            oh = oh + jnp.dot(p_t[g], vp_refs[g][h].astype(BF16), preferred_element_type=F32)
        upd = jnp.where(rows == h, oh, upd)
    acc_ref[...] = acc_ref[...] * _col_bcast(alpha)[0:hp, :] + upd

    @pl.when(pi == pl.num_programs(1) - 1)
    def _():
        o_ref[0] = (acc_ref[...] / _col_bcast(l_ref[...])[0:hp, :]).astype(o_ref.dtype)


def _fox_sample(q, kn, vn, lfn, cache_k, cache_v, lf_pad, l, page_table, nch):
    bsz = q.shape[0]
    n_pages = page_table.shape[1]
    page = cache_k.shape[3]
    hp = vn.shape[1]
    assert page == HEAD_DIM and cache_k.shape[4] == HEAD_DIM and cache_k.shape[2] == nch
    pt = page_table.reshape(-1)
    group = _pick(n_pages, (FOX_PAGE_GROUP, 2, 1))
    per_b = lambda *blk: pl.BlockSpec((1,) + blk, lambda b, p, pt: (b,) + (0,) * len(blk))

    def paged(g, *blk):
        return pl.BlockSpec((None, None) + blk,
                            lambda b, p, pt: (l, pt[b * n_pages + n_pages - 1 - (p * group + g)]) + (0,) * len(blk))

    grid_spec = pltpu.PrefetchScalarGridSpec(
        num_scalar_prefetch=1,
        grid=(bsz, n_pages // group),
        in_specs=([per_b(HEAD_DIM, HEAD_DIM), per_b(HEAD_DIM, HEAD_DIM), per_b(hp, HEAD_DIM), per_b(1, HEAD_DIM)]
                  + [paged(g, nch, page, HEAD_DIM) for g in range(group)]
                  + [paged(g, nch, page, HEAD_DIM) for g in range(group)]
                  + [paged(g, page, HEAD_DIM) for g in range(group)]),
        out_specs=per_b(hp, HEAD_DIM),
        scratch_shapes=[pltpu.VMEM((HEAD_DIM, HEAD_DIM), BF16), pltpu.VMEM((1, HEAD_DIM), F32),
                        pltpu.VMEM((1, HEAD_DIM), F32), pltpu.VMEM((1, HEAD_DIM), F32),
                        pltpu.VMEM((hp, HEAD_DIM), F32)],
    )
    return pl.pallas_call(
        functools.partial(_fox_sample_kernel, nch=nch, hp=hp, group=group),
        out_shape=jax.ShapeDtypeStruct((bsz, hp, HEAD_DIM), BF16),
        grid_spec=grid_spec,
        compiler_params=_params("parallel", "arbitrary"),
        name="fox_sample",
    )(pt, q, kn, vn, lfn, *([cache_k] * group + [cache_v] * group + [lf_pad] * group))


def _ffn_sample_kernel(upg_ref, upv_ref, sg_ref, sv_ref, cwg_ref, cwv_ref, act_ref, *, width):
    ys = []
    for up_ref, s_ref, cw_ref in ((upg_ref, sg_ref, cwg_ref), (upv_ref, sv_ref, cwv_ref)):
        y = up_ref[...] * cw_ref[width - 1:width, :]
        for i in range(width - 1):
            y = y + s_ref[i] * cw_ref[i:i + 1, :]
        ys.append(y)
    act_ref[...] = (_silu(ys[0]) * ys[1]).astype(act_ref.dtype)


def _ffn_sample(up, state_t, conv_w):
    r, f2 = up.shape
    f = f2 // 2
    width = conv_w.shape[0]
    tn = _pick(f, (1024, 512, 256, 128))
    nj = f // tn
    return pl.pallas_call(
        functools.partial(_ffn_sample_kernel, width=width),
        out_shape=jax.ShapeDtypeStruct((r, f), BF16),
        grid=(nj,),
        in_specs=[pl.BlockSpec((r, tn), lambda j: (0, j)), pl.BlockSpec((r, tn), lambda j: (0, j + nj)),
                  pl.BlockSpec((width - 1, r, tn), lambda j: (0, 0, j)),
                  pl.BlockSpec((width - 1, r, tn), lambda j: (0, 0, j + nj)),
                  pl.BlockSpec((width, tn), lambda j: (0, j)), pl.BlockSpec((width, tn), lambda j: (0, j + nj))],
        out_specs=pl.BlockSpec((r, tn), lambda j: (0, j)),
        compiler_params=_params("parallel"),
        name="ffn_sample",
    )(up, up, state_t, state_t, conv_w, conv_w)


def _in_proj(h, h2, w_in, l, o_b, shift, glu_w, cw, tn, bsz, t, k_prev, v_prev):
    main = _wmatmul(h, w_in, l, 0, o_b + glu_w + cw, tn, o_b, shift, a2=h2)
    k = _wmatmul(h, w_in, l, o_b + glu_w + cw, cw, tn, o_b, shift, a2=h2, heads=(bsz, t, k_prev))
    v = _wmatmul(h, w_in, l, o_b + glu_w + 2 * cw, cw, tn, o_b, shift, a2=h2, heads=(bsz, t, v_prev))
    pg = _wmatmul(h, w_in, l, o_b, HEAD_DIM, HEAD_DIM, a2=h2, pair_col=o_b + glu_w + 3 * cw)
    return main, k, v, pg


def kernel(x_prompt, x_sample, cache_k, cache_v, cache_logf, page_table, state_gdn, state_gdn_conv, state_conf_conv, state_ffn_conv, norm_mix, w_in, gdn_conv_w, gdn_a_log, gdn_dt_bias, gdn_norm_w, conf_dw_w, conf_dw_b, conf_ln_g, conf_ln_b, fox_b_f, w_out, norm_ffn, ffn_conv_w, w_up, w_down, norm_final):
    bp, t, d = x_prompt.shape
    bs, ts, _ = x_sample.shape
    assert ts == 1, "the sample step handles one new token per sequence"
    depth = w_in.shape[0]
    nvh = gdn_a_log.shape[1]
    aqkv = gdn_conv_w.shape[2]
    vw = nvh * HEAD_DIM
    kw = (aqkv - vw) // 2
    nkh = kw // HEAD_DIM
    nch = fox_b_f.shape[1]
    cw = nch * HEAD_DIM
    bch = conf_dw_w.shape[2]
    f = w_down.shape[1]
    o_b = aqkv + vw
    shift = 2 * nvh
    glu_w = 2 * bch
    tn_in = _pick(o_b, (512, 256, 128))
    assert (t % GDN_CHUNK == 0 and cw == vw and aqkv % vw == 0 and o_b % (bch // 2) == 0
            and glu_w % tn_in == 0 and cw % tn_in == 0)
    z_blk = aqkv // vw
    glu_off = o_b
    q_off = o_b + glu_w
    tq = _pick(t, (512, 256, 128))
    nq = t // tq
    n_chunks = t // GDN_CHUNK
    hp = -(-nch // SUBLANES) * SUBLANES

    xp = x_prompt.reshape(bp * t, d)
    rs = -(-bs // ROW_PAD) * ROW_PAD
    xs = jnp.zeros((rs, d), F32).at[:bs].set(x_sample.reshape(bs, d))
    lf_pad = jnp.pad(cache_logf, ((0, 0), (0, 0), (0, 0), (0, HEAD_DIM - nch)))
    cache_kh = cache_k.transpose(0, 1, 3, 2, 4)
    cache_vh = cache_v.transpose(0, 1, 3, 2, 4)
    k_all = v_all = None
    w_o = w_out.astype(BF16)
    w_d = w_down.astype(BF16)
    outs_p, outs_s = [], []
    for l in range(depth):

        h = _rmsnorm(xp, norm_mix[l], BF16)
        hs = _rmsnorm(xs, norm_mix[l], BF16)
        (pm, pm_s), (k_all, kc_s), (v_all, vc_s), (pg, pg_s) = _in_proj(
            h, hs, w_in, l, o_b, shift, glu_w, cw, tn_in, bp, t, k_all, v_all)

        gb, logf, ccum = _gates(pg, gdn_a_log[l], gdn_dt_bias[l], fox_b_f[l], bp, t)
        qk, vact, gconv_new = _gdn_prep(pm, gdn_conv_w[l], jnp.zeros((bp, gdn_conv_w.shape[1] - 1, aqkv), F32),
                                        bp, t, kw, vw)
        grow = gb[:, :nvh].reshape(bp * n_chunks, GDN_CHUNK, nvh).transpose(0, 2, 1)
        o_a, s_new = _gdn_chunks(qk, vact, pm, z_blk, gb, grow, jnp.zeros((bp, nvh, HEAD_DIM, HEAD_DIM), F32),
                                 gdn_norm_w[l], bp, t, nkh, nvh)
        o_bm, cconv_new = _conformer(pm, glu_off // (bch // 2), conf_dw_w[l], conf_dw_b[l], conf_ln_g[l],
                                     conf_ln_b[l], jnp.zeros((bp, conf_dw_w.shape[1] - 1, bch), F32), bp, t)
        c_t = ccum.reshape(bp, t, nch).transpose(0, 2, 1)
        o_c = _fox_prompt(pm, q_off // HEAD_DIM, k_all, v_all, l, c_t.reshape(bp * nch, nq, 1, tq), bp, t, nch, tq)
        xp = _matmul([o_a, o_bm, o_c], w_o, l, res=xp)
        outs_p.append((logf.reshape(bp, t, nch), s_new, gconv_new, cconv_new))

        o_a, o_bm, s_new, gconv_new, cconv_new, logf = _sample_mix(
            pm_s, pg_s, state_gdn_conv[l], state_gdn[l], state_conf_conv[l], gdn_conv_w[l], gdn_a_log[l],
            gdn_dt_bias[l], gdn_norm_w[l], conf_dw_w[l], conf_dw_b[l], conf_ln_g[l], conf_ln_b[l], fox_b_f[l],
            nkh, aqkv, glu_off)
        head_rows = lambda a, rows: jnp.zeros((bs, rows, HEAD_DIM), F32).at[:, :nch].set(a.reshape(bs, nch, HEAD_DIM))
        lfn = jnp.zeros((bs, 1, HEAD_DIM), F32).at[:, :, :nch].set(logf)
        o_c = _fox_sample(head_rows(pm_s[:bs, q_off:q_off + cw], HEAD_DIM), head_rows(kc_s[:bs], HEAD_DIM),
                          head_rows(vc_s[:bs], hp), lfn, cache_kh, cache_vh, lf_pad, l, page_table, nch)
        o_c = o_c[:, :nch].reshape(bs, cw)
        pad_rows = lambda a: jnp.zeros((rs, a.shape[-1]), a.dtype).at[:bs].set(a.reshape(bs, -1))
        xs = _matmul([pad_rows(o_a), pad_rows(o_bm), pad_rows(o_c)], w_o, l, res=xs)

        h = _rmsnorm(xp, norm_ffn[l], BF16)
        hs = _rmsnorm(xs, norm_ffn[l], BF16)
        act, fnew_g, fnew_v, up_g, up_v = _ffn_up(h, hs, w_up, l, ffn_conv_w[l],
                                                  jnp.zeros((bp, ffn_conv_w.shape[1] - 1, 2 * f), F32), bp, t)
        xp = _matmul([act], w_d, l, res=xp)
        up = jnp.concatenate([up_g, up_v], axis=1)
        st = jnp.zeros((ffn_conv_w.shape[1] - 1, rs, 2 * f), F32).at[:, :bs].set(state_ffn_conv[l].transpose(1, 0, 2))
        act = _ffn_sample(up, st, ffn_conv_w[l])
        xs = _matmul([act], w_d, l, res=xs)
        outs_p[-1] += (jnp.concatenate([fnew_g, fnew_v], axis=-1),)
        fconv_new = jnp.concatenate([state_ffn_conv[l][:, 1:], up[:bs, None, :]], axis=1)
        outs_s.append((kc_s[:bs].reshape(bs, 1, nch, HEAD_DIM), vc_s[:bs].reshape(bs, 1, nch, HEAD_DIM), logf,
                       s_new, gconv_new, cconv_new, fconv_new))

    y_prompt = _rmsnorm(xp, norm_final, F32).reshape(bp, t, d)
    y_sample = _rmsnorm(xs, norm_final, F32)[:bs].reshape(bs, 1, d)
    stack = lambda outs, i: jnp.stack([o[i] for o in outs], axis=0)
    k_rows_p = k_all.transpose(0, 1, 3, 2, 4)
    v_rows_p = v_all.transpose(0, 1, 3, 2, 4)
    return ((y_prompt, y_sample, k_rows_p, v_rows_p) + tuple(stack(outs_p, i) for i in range(5))
            + tuple(stack(outs_s, i) for i in range(7)))
```

```python
import functools
import math

import jax
import jax.numpy as jnp
from jax import lax
from jax.experimental import pallas as pl
from jax.experimental.pallas import tpu as pltpu

EPS = 1e-6
HEAD_DIM = 128
GDN_CHUNK = 64
SUBLANES = 8
ROW_PAD = 16
VMEM_LIMIT = 56 * 2**20
FOX_PAGE_GROUP = 8
LF_POOL_ROWS = 8
FFN_SUB_ROWS = 512

F32 = jnp.float32
BF16 = jnp.bfloat16
_HP = lax.Precision.HIGHEST


def _params(*sem):
    return pltpu.CompilerParams(dimension_semantics=sem, vmem_limit_bytes=VMEM_LIMIT)


def _pick(n, cands):
    for c in cands:
        if n % c == 0:
            return c
    return n


def _silu(x):
    return x * jax.nn.sigmoid(x)


def _softplus(x):
    return jnp.maximum(x, 0.0) + jnp.log(1.0 + jnp.exp(-jnp.abs(x)))


def _dot_t(a, b):
    return lax.dot_general(a, b, (((1,), (1,)), ((), ())), preferred_element_type=F32)


def _hp_dot(a, b):
    return jnp.dot(a, b, precision=_HP, preferred_element_type=F32)


def _rmsnorm_kernel(x_ref, g_ref, o_ref):
    x = x_ref[...]
    ms = jnp.mean(x * x, axis=-1, keepdims=True)
    o_ref[...] = (x * lax.rsqrt(ms + EPS) * g_ref[...]).astype(o_ref.dtype)


def _rmsnorm(x, g, out_dtype):
    m, d = x.shape
    tm = _pick(m, (256, 128, 64, 32, 16, 8))
    return pl.pallas_call(
        _rmsnorm_kernel,
        out_shape=jax.ShapeDtypeStruct((m, d), out_dtype),
        grid=(m // tm,),
        in_specs=[pl.BlockSpec((tm, d), lambda i: (i, 0)), pl.BlockSpec((1, d), lambda i: (0, 0))],
        out_specs=pl.BlockSpec((tm, d), lambda i: (i, 0)),
        compiler_params=_params("parallel"),
        name="rmsnorm",
    )(x, g.reshape(1, d))


def _mm_kernel(*refs, n_a, has_res):
    a_refs = refs[:n_a]
    w_refs = refs[n_a:2 * n_a]
    o_ref = refs[-1]
    acc = jnp.dot(a_refs[0][...], w_refs[0][0], preferred_element_type=F32)
    for a_ref, w_ref in zip(a_refs[1:], w_refs[1:]):
        acc = acc + jnp.dot(a_ref[...], w_ref[0], preferred_element_type=F32)
    if has_res:
        acc = refs[2 * n_a][...] + acc
    o_ref[...] = acc


def _matmul(a_list, w, l, res=None):
    m = a_list[0].shape[0]
    n = w.shape[2]
    k_total = sum(a.shape[1] for a in a_list)
    assert k_total == w.shape[1]
    tm = _pick(m, (1024, 512, 256, 128, 64, 32, 16)) if k_total <= 4096 else _pick(m, (512, 256, 128, 64, 32, 16))
    tn = _pick(n, (512, 256, 128))
    in_specs = [pl.BlockSpec((tm, a.shape[1]), lambda i, j: (i, 0)) for a in a_list]
    row = 0
    for a in a_list:
        in_specs.append(pl.BlockSpec((pl.Element(1), pl.Element(a.shape[1]), pl.Element(tn)),
                                     lambda i, j, row=row: (l, row, j * tn)))
        row += a.shape[1]
    args = list(a_list) + [w] * len(a_list)
    if res is not None:
        in_specs.append(pl.BlockSpec((tm, tn), lambda i, j: (i, j)))
        args.append(res)
    return pl.pallas_call(
        functools.partial(_mm_kernel, n_a=len(a_list), has_res=res is not None),
        out_shape=jax.ShapeDtypeStruct((m, n), F32),
        grid=(m // tm, n // tn),
        in_specs=in_specs,
        out_specs=pl.BlockSpec((tm, tn), lambda i, j: (i, j)),
        compiler_params=_params("parallel", "parallel"),
        name="matmul",
    )(*args)


CAST_ROWS = 512


def _cast_rows(dst_ref, c0, c1, load):
    k = dst_ref.shape[0]
    step = CAST_ROWS if k % CAST_ROWS == 0 else k

    def body(c, carry):
        rows = pl.ds(pl.multiple_of(c * step, step), step)
        dst_ref[rows, c0:c1] = load(rows).astype(BF16)
        return carry

    lax.fori_loop(0, k // step, body, 0)


def _wmm_kernel(*refs, shift, n_plain, has_a2, has_prev, heads_out, pair):
    refs = list(refs)
    a_ref = refs.pop(0)
    a2_ref = refs.pop(0) if has_a2 else None
    wa_ref = refs.pop(0)
    wb_ref = refs.pop(0)
    if has_prev:
        refs.pop(0)
    o_ref = refs.pop(0)
    o2_ref = refs.pop(0) if has_a2 else None
    wbf_ref = refs.pop(0)
    j = pl.program_id(0)
    i = pl.program_id(1)
    tn = wa_ref.shape[1]

    def plain():
        _cast_rows(wbf_ref, 0, tn, lambda rows: wa_ref[rows, :])

    def shifted():
        _cast_rows(wbf_ref, 0, tn,
                   lambda rows: jnp.concatenate([wa_ref[rows, :], wb_ref[rows, :]], axis=1)[:, shift:shift + tn])

    @pl.when(i == 0)
    def _():
        if pair:
            plain()

            def strip(rows):
                wb = wb_ref[rows, :]
                return jnp.where(lax.broadcasted_iota(jnp.int32, wb.shape, 1) < pair, wb, 0.0)

            _cast_rows(wbf_ref, tn, tn + wb_ref.shape[1], strip)
        elif shift == 0:
            plain()
        else:
            pl.when(j < n_plain)(plain)
            pl.when(j >= n_plain)(shifted)

    acc = jnp.dot(a_ref[...], wbf_ref[...], preferred_element_type=F32)
    if heads_out:
        for hh in range(tn // HEAD_DIM):
            o_ref[hh] = acc[:, hh * HEAD_DIM:(hh + 1) * HEAD_DIM]
    else:
        o_ref[...] = acc
    if has_a2:
        @pl.when(i == pl.num_programs(1) - 1)
        def _():
            o2_ref[...] = jnp.dot(a2_ref[...], wbf_ref[...], preferred_element_type=F32)


def _wmatmul_pair(a, w, l, blk_a, blk_b, tn, tm, a2):
    m, k = a.shape
    wide = tn + HEAD_DIM
    in_specs = [pl.BlockSpec((tm, k), lambda j, i: (i, 0))]
    args = [a]
    out_shape = [jax.ShapeDtypeStruct((m, wide), F32)]
    out_specs = [pl.BlockSpec((tm, wide), lambda j, i: (i, 0))]
    if a2 is not None:
        in_specs.append(pl.BlockSpec(a2.shape, lambda j, i: (0, 0)))
        args.append(a2)
        out_shape.append(jax.ShapeDtypeStruct((a2.shape[0], wide), F32))
        out_specs.append(pl.BlockSpec((a2.shape[0], wide), lambda j, i: (0, 0)))
    in_specs += [pl.BlockSpec((None, k, tn), lambda j, i: (l, 0, blk_a)),
                 pl.BlockSpec((None, k, HEAD_DIM), lambda j, i: (l, 0, blk_b))]
    args += [w, w]
    res = pl.pallas_call(
        functools.partial(_wmm_kernel, shift=0, n_plain=1, has_a2=a2 is not None, has_prev=False, heads_out=False,
                          pair=min(HEAD_DIM, w.shape[2] - blk_b * HEAD_DIM)),
        out_shape=out_shape,
        grid=(1, m // tm),
        in_specs=in_specs,
        out_specs=out_specs,
        scratch_shapes=[pltpu.VMEM((k, wide), BF16)],
        compiler_params=_params("arbitrary", "arbitrary"),
        name="wmatmul_pair",
    )(*args)
    return res[0] if a2 is None else tuple(res)


def _wmatmul(a, w, l, c0, ncols, tn, plain_cols=0, shift=0, a2=None, heads=None, pair_col=None):
    m, k = a.shape
    assert c0 % tn == 0 and ncols % tn == 0 and plain_cols % tn == 0 and tn % HEAD_DIM == 0 and shift < HEAD_DIM
    tm = _pick(m if heads is None else heads[1], (1024, 512, 256, 128, 64, 32, 16))
    jb = c0 // tn
    n_plain = max(0, min(ncols, plain_cols - c0) // tn) if shift else ncols // tn
    lanes_per = tn // HEAD_DIM
    last_blk = -(-w.shape[2] // HEAD_DIM) - 1
    pair = pair_col is not None
    if pair:
        assert ncols == tn and shift == 0 and heads is None and pair_col % HEAD_DIM == 0
        return _wmatmul_pair(a, w, l, jb, pair_col // HEAD_DIM, tn, tm, a2)
    in_specs = [pl.BlockSpec((tm, k), lambda j, i: (i, 0))]
    args = [a]
    if a2 is not None:
        in_specs.append(pl.BlockSpec(a2.shape, lambda j, i: (0, 0)))
        args.append(a2)
    in_specs += [pl.BlockSpec((None, k, tn), lambda j, i: (l, 0, jb + j)),
                 pl.BlockSpec((None, k, HEAD_DIM),
                              lambda j, i: (l, 0, jnp.minimum((jb + j + 1) * lanes_per, last_blk)))]
    args += [w, w]
    aliases = {}
    if heads is None:
        out_shape = [jax.ShapeDtypeStruct((m, ncols), F32)]
        out_specs = [pl.BlockSpec((tm, tn), lambda j, i: (i, j))]
    else:
        bsz, t, prev = heads
        assert t % tm == 0
        tpb = t // tm
        out_shape = [jax.ShapeDtypeStruct((w.shape[0], bsz, ncols // HEAD_DIM, t, HEAD_DIM), F32)]
        out_specs = [pl.BlockSpec((None, None, lanes_per, tm, HEAD_DIM),
                                  lambda j, i: (l, i // tpb, j, i % tpb, 0))]
        if prev is not None:
            in_specs.append(pl.BlockSpec(memory_space=pl.ANY))
            aliases = {len(args): 0}
            args.append(prev)
    if a2 is not None:
        out_shape.append(jax.ShapeDtypeStruct((a2.shape[0], ncols), F32))
        out_specs.append(pl.BlockSpec((a2.shape[0], tn), lambda j, i: (0, j)))
    res = pl.pallas_call(
        functools.partial(_wmm_kernel, shift=shift, n_plain=n_plain, has_a2=a2 is not None, pair=0,
                          has_prev=heads is not None and heads[2] is not None, heads_out=heads is not None),
        out_shape=out_shape,
        grid=(ncols // tn, m // tm),
        in_specs=in_specs,
        out_specs=out_specs,
        scratch_shapes=[pltpu.VMEM((k, tn), BF16)],
        input_output_aliases=aliases,
        compiler_params=_params("arbitrary", "arbitrary"),
        name="wmatmul",
    )(*args)
    return res[0] if a2 is None else tuple(res)


def _gates_kernel(ps_ref, pf_ref, alog_ref, dtb_ref, bf_ref, gb_ref, lf_ref, c_ref, carry_ref, *, nvh, nch):
    t = pl.program_id(1)
    ps = ps_ref[...]
    tt = ps.shape[0]
    b = ps[:, 0:nvh]
    a = ps[:, nvh:2 * nvh]
    f = pf_ref[:, 2 * nvh:2 * nvh + nch]
    gb_ref[:, 0:nvh] = -jnp.exp(alog_ref[...]) * _softplus(a + dtb_ref[...])
    gb_ref[:, nvh:2 * nvh] = jax.nn.sigmoid(b)
    lf = -_softplus(-(f + bf_ref[...]))
    lf_ref[...] = lf

    @pl.when(t == 0)
    def _():
        carry_ref[...] = jnp.zeros_like(carry_ref)

    ii = lax.broadcasted_iota(jnp.int32, (tt, tt), 0)
    jj = lax.broadcasted_iota(jnp.int32, (tt, tt), 1)
    tri = (ii >= jj).astype(F32)
    c = _hp_dot(tri, lf) + carry_ref[...]
    c_ref[...] = c
    carry_ref[...] = c[tt - 1:tt, :]


def _gates(pg, a_log, dt_bias, b_f, bsz, t):
    nvh = a_log.shape[0]
    nch = b_f.shape[0]
    tt = _pick(t, (256, 128, 64, 32, 16, 8))
    nt = t // tt
    pw = HEAD_DIM
    ps = pf = pg
    row = lambda b, i: (b * nt + i, 0)
    fix = lambda b, i: (0, 0)
    return pl.pallas_call(
        functools.partial(_gates_kernel, nvh=nvh, nch=nch),
        out_shape=(jax.ShapeDtypeStruct((bsz * t, 2 * nvh), F32),
                   jax.ShapeDtypeStruct((bsz * t, nch), F32),
                   jax.ShapeDtypeStruct((bsz * t, nch), F32)),
        grid=(bsz, nt),
        in_specs=[pl.BlockSpec((tt, pw), row), pl.BlockSpec((tt, pw), lambda b, i: (b * nt + i, 1)),
                  pl.BlockSpec((1, nvh), fix), pl.BlockSpec((1, nvh), fix), pl.BlockSpec((1, nch), fix)],
        out_specs=(pl.BlockSpec((tt, 2 * nvh), row), pl.BlockSpec((tt, nch), row), pl.BlockSpec((tt, nch), row)),
        scratch_shapes=[pltpu.VMEM((1, nch), F32)],
        compiler_params=_params("parallel", "arbitrary"),
        name="gates",
    )(ps, pf, a_log.reshape(1, nvh), dt_bias.reshape(1, nvh), b_f.reshape(1, nch))


def _gdn_prep_kernel(x_ref, w_ref, prev_ref, qk_ref, v_ref, cnew_ref, buf_ref, *, tt, kw, width):
    t = pl.program_id(1)
    halo = width - 1
    base = SUBLANES
    lo = base - halo

    @pl.when(t == 0)
    def _():
        buf_ref[lo:base, :] = prev_ref[0]

    @pl.when(t > 0)
    def _():
        buf_ref[lo:base, :] = buf_ref[lo + tt:base + tt, :]

    buf_ref[base:base + tt, :] = x_ref[...]
    chans = x_ref.shape[1]
    for c0 in range(0, chans, HEAD_DIM):
        cs = slice(c0, c0 + HEAD_DIM)
        y = buf_ref[lo:lo + tt, cs] * w_ref[0:1, cs]
        for i in range(1, width):
            y = y + buf_ref[lo + i:lo + i + tt, cs] * w_ref[i:i + 1, cs]
        y = _silu(y)
        if c0 < 2 * kw:
            y = y * lax.rsqrt(jnp.sum(y * y, axis=-1, keepdims=True) + EPS)
            if c0 < kw:
                y = y * HEAD_DIM ** -0.5
            qk_ref[:, cs] = y
        else:
            v_ref[:, c0 - 2 * kw:c0 - 2 * kw + HEAD_DIM] = y

    @pl.when(t == pl.num_programs(1) - 1)
    def _():
        cnew_ref[0] = buf_ref[lo + tt:base + tt, :]


def _gdn_prep(pm, conv_w, prev, bsz, t, kw, vw):
    width, chans = conv_w.shape
    tt = _pick(t, (256, 128, 64))
    nt = t // tt
    row = lambda b, i: (b * nt + i, 0)
    return pl.pallas_call(
        functools.partial(_gdn_prep_kernel, tt=tt, kw=kw, width=width),
        out_shape=(jax.ShapeDtypeStruct((bsz * t, 2 * kw), F32),
                   jax.ShapeDtypeStruct((bsz * t, vw), F32),
                   jax.ShapeDtypeStruct((bsz, width - 1, chans), F32)),
        grid=(bsz, nt),
        in_specs=[pl.BlockSpec((tt, chans), row),
                  pl.BlockSpec((width, chans), lambda b, i: (0, 0)),
                  pl.BlockSpec((1, width - 1, chans), lambda b, i: (b, 0, 0))],
        out_specs=(pl.BlockSpec((tt, 2 * kw), row), pl.BlockSpec((tt, vw), row),
                   pl.BlockSpec((1, width - 1, chans), lambda b, i: (b, 0, 0))),
        scratch_shapes=[pltpu.VMEM((SUBLANES + tt, chans), F32)],
        compiler_params=_params("parallel", "arbitrary"),
        name="gdn_prep",
    )(pm, conv_w, prev)


def _bdot(a, b):
    return jnp.dot(a.astype(BF16), b.astype(BF16), preferred_element_type=F32)


def _gdn_chunk_kernel(qk_ref, v_ref, z_ref, gb_ref, grow_ref, s0_ref, nw_ref, o_ref, sout_ref, s_ref,
                      *, c, nkh, nvh, kw):
    ci = pl.program_id(1)

    @pl.when(ci == 0)
    def _():
        s_ref[...] = s0_ref[0]

    rep = nvh // nkh
    ii = lax.broadcasted_iota(jnp.int32, (c, c), 0)
    jj = lax.broadcasted_iota(jnp.int32, (c, c), 1)
    lower = ii >= jj
    strict = ii > jj
    same_blk = {}
    size = SUBLANES
    while size <= c:
        sh = int(math.log2(size))
        same_blk[size] = (ii >> sh) == (jj >> sh)
        size *= 2
    gb = gb_ref[...]
    grow = grow_ref[0]
    nw = nw_ref[...]
    heads = range(nvh)
    hsl = [slice(h * HEAD_DIM, (h + 1) * HEAD_DIM) for h in heads]
    q = [qk_ref[:, kh * HEAD_DIM:(kh + 1) * HEAD_DIM] for kh in range(nkh)]
    k = [qk_ref[:, kw + kh * HEAD_DIM:kw + (kh + 1) * HEAD_DIM] for kh in range(nkh)]
    v = [v_ref[:, hsl[h]] for h in heads]
    s_old = [s_ref[h] for h in heads]
    k16 = [x.astype(BF16) for x in k]
    kk = [_dot_t(k16[i], k16[i]) for i in range(nkh)]
    qk = [_dot_t(q[i].astype(BF16), k16[i]) for i in range(nkh)]
    beta = [gb[:, nvh + h:nvh + h + 1] for h in heads]
    gc_col = [jnp.sum(jnp.where(lower, grow[h:h + 1, :], 0.0), axis=1, keepdims=True) for h in heads]
    gc_row = [jnp.sum(jnp.where(ii <= jj, gb[:, h:h + 1], 0.0), axis=0, keepdims=True) for h in heads]
    decay = [jnp.exp(jnp.where(lower, gc_col[h] - gc_row[h], -1e30)) for h in heads]
    a = [jnp.where(strict, kk[h // rep] * beta[h] * decay[h], 0.0) for h in heads]
    d = [jnp.where(same_blk[SUBLANES], a[h], 0.0) for h in heads]
    d2 = [_bdot(d[h], d[h]) for h in heads]
    d4 = [_bdot(d2[h], d2[h]) for h in heads]
    x = [d2[h] - d[h] - _bdot(d2[h], d[h]) for h in heads]
    x = [x[h] + d4[h] + _bdot(d4[h], x[h]) for h in heads]
    size = SUBLANES
    while size < c:
        r = [jnp.where(same_blk[2 * size], a[h], 0.0) - jnp.where(same_blk[size], a[h], 0.0) for h in heads]
        y = [r[h] + _bdot(x[h], r[h]) for h in heads]
        x = [x[h] - y[h] - _bdot(y[h], x[h]) for h in heads]
        size *= 2
    egc = [jnp.exp(gc_col[h]) for h in heads]
    rhs = [jnp.concatenate([v[h] * beta[h], k[h // rep] * (beta[h] * egc[h])], axis=1) for h in heads]
    uw = [rhs[h] + _bdot(x[h], rhs[h]) for h in heads]
    s16 = [s_old[h].astype(BF16) for h in heads]
    v_new = [uw[h][:, :HEAD_DIM] - jnp.dot(uw[h][:, HEAD_DIM:].astype(BF16), s16[h], preferred_element_type=F32)
             for h in heads]
    vn16 = [v_new[h].astype(BF16) for h in heads]
    o = [jnp.dot((q[h // rep] * egc[h]).astype(BF16), s16[h], preferred_element_type=F32)
         + jnp.dot((qk[h // rep] * decay[h]).astype(BF16), vn16[h], preferred_element_type=F32) for h in heads]
    g_last = [gc_col[h][c - 1:c, :] for h in heads]
    kd = [(k[h // rep] * jnp.exp(g_last[h] - gc_col[h])).astype(BF16) for h in heads]
    s_new = [s_old[h] * jnp.exp(g_last[h])
             + lax.dot_general(kd[h], vn16[h], (((0,), (0,)), ((), ())), preferred_element_type=F32) for h in heads]
    on = [o[h] * lax.rsqrt(jnp.mean(o[h] * o[h], axis=-1, keepdims=True) + EPS) * nw for h in heads]
    for h in heads:
        s_ref[h] = s_new[h]
        o_ref[:, hsl[h]] = (on[h] * _silu(z_ref[:, hsl[h]])).astype(o_ref.dtype)

    @pl.when(ci == pl.num_programs(1) - 1)
    def _():
        sout_ref[0] = s_ref[...]


def _gdn_chunks(qk, v, pm, z_blk, gb, grow, s0, norm_w, bsz, t, nkh, nvh):
    c = GDN_CHUNK
    n = t // c
    kw = nkh * HEAD_DIM
    vw = nvh * HEAD_DIM
    row = lambda b, i: (b * n + i, 0)
    return pl.pallas_call(
        functools.partial(_gdn_chunk_kernel, c=c, nkh=nkh, nvh=nvh, kw=kw),
        out_shape=(jax.ShapeDtypeStruct((bsz * t, vw), BF16),
                   jax.ShapeDtypeStruct((bsz, nvh, HEAD_DIM, HEAD_DIM), F32)),
        grid=(bsz, n),
        in_specs=[pl.BlockSpec((c, 2 * kw), row),
                  pl.BlockSpec((c, vw), row),
                  pl.BlockSpec((c, vw), lambda b, i: (b * n + i, z_blk)),
                  pl.BlockSpec((c, 2 * nvh), row),
                  pl.BlockSpec((1, nvh, c), lambda b, i: (b * n + i, 0, 0)),
                  pl.BlockSpec((1, nvh, HEAD_DIM, HEAD_DIM), lambda b, i: (b, 0, 0, 0)),
                  pl.BlockSpec((1, HEAD_DIM), lambda b, i: (0, 0))],
        out_specs=(pl.BlockSpec((c, vw), row),
                   pl.BlockSpec((1, nvh, HEAD_DIM, HEAD_DIM), lambda b, i: (b, 0, 0, 0))),
        scratch_shapes=[pltpu.VMEM((nvh, HEAD_DIM, HEAD_DIM), F32)],
        compiler_params=_params("parallel", "arbitrary"),
        name="gdn_chunks",
    )(qk, v, pm, gb, grow, s0, norm_w.reshape(1, HEAD_DIM))


def _conf_kernel(val0_ref, val1_ref, gate0_ref, gate1_ref, w_ref, b_ref, g_ref, beta_ref, prev_ref, o_ref, cnew_ref,
                 buf_ref, y_ref, *, tt, width, base):
    t = pl.program_id(1)
    halo = width - 1
    lo = base - halo
    half = val0_ref.shape[1]
    ch = 2 * half

    @pl.when(t == 0)
    def _():
        buf_ref[lo:base, :] = prev_ref[0]
        buf_ref[base + tt:base + tt + SUBLANES, :] = jnp.zeros((SUBLANES, ch), F32)

    @pl.when(t > 0)
    def _():
        buf_ref[lo:base, :] = buf_ref[lo + tt:base + tt, :]

    buf_ref[base:base + tt, 0:half] = val0_ref[...] * jax.nn.sigmoid(gate0_ref[...])
    buf_ref[base:base + tt, half:2 * half] = val1_ref[...] * jax.nn.sigmoid(gate1_ref[...])
    cchunk = 2 * HEAD_DIM if ch % (2 * HEAD_DIM) == 0 else ch
    for c0 in range(0, ch, cchunk):
        cs = slice(c0, c0 + cchunk)
        y = None
        for r in range(SUBLANES):
            z = None
            for i in range(width):
                if (lo + i) % SUBLANES != r:
                    continue
                a0 = (lo + i) - r
                term = buf_ref[a0:a0 + tt + SUBLANES, cs] * w_ref[i:i + 1, cs]
                z = term if z is None else z + term
            if z is not None:
                zr = z[r:r + tt, :]
                y = zr if y is None else y + zr
        y_ref[:, cs] = y
    y = y_ref[...] + b_ref[...]
    yc = y - jnp.mean(y, axis=-1, keepdims=True)
    yn = yc * lax.rsqrt(jnp.mean(yc * yc, axis=-1, keepdims=True) + EPS)
    o_ref[...] = _silu(yn * g_ref[...] + beta_ref[...]).astype(o_ref.dtype)

    @pl.when(t == pl.num_programs(1) - 1)
    def _():
        cnew_ref[0] = buf_ref[lo + tt:base + tt, :]


def _conformer(pm, val_blk, dw_w, dw_b, ln_g, ln_b, prev, bsz, t):
    width, ch = dw_w.shape
    tt = _pick(t, (256, 128, 64))
    nt = t // tt
    base = -(-(width - 1) // SUBLANES) * SUBLANES
    fix = lambda b, i: (0, 0)
    half = ch // 2
    part = lambda p: pl.BlockSpec((tt, half), lambda b, i: (b * nt + i, val_blk + p))
    return pl.pallas_call(
        functools.partial(_conf_kernel, tt=tt, width=width, base=base),
        out_shape=(jax.ShapeDtypeStruct((bsz * t, ch), BF16),
                   jax.ShapeDtypeStruct((bsz, width - 1, ch), F32)),
        grid=(bsz, nt),
        in_specs=[part(0), part(1), part(2), part(3),
                  pl.BlockSpec((width, ch), fix), pl.BlockSpec((1, ch), fix), pl.BlockSpec((1, ch), fix),
                  pl.BlockSpec((1, ch), fix),
                  pl.BlockSpec((1, width - 1, ch), lambda b, i: (b, 0, 0))],
        out_specs=(pl.BlockSpec((tt, ch), lambda b, i: (b * nt + i, 0)),
                   pl.BlockSpec((1, width - 1, ch), lambda b, i: (b, 0, 0))),
        scratch_shapes=[pltpu.VMEM((base + tt + SUBLANES, ch), F32), pltpu.VMEM((tt, ch), F32)],
        compiler_params=_params("parallel", "arbitrary"),
        name="conformer",
    )(pm, pm, pm, pm, dw_w, dw_b.reshape(1, ch), ln_g.reshape(1, ch), ln_b.reshape(1, ch), prev)


def _fox_prompt_kernel(q_ref, k_ref, v_ref, ck_ref, o_ref, *, tq):
    nq = q_ref.shape[0] // tq
    tiles = [slice(i * tq, (i + 1) * tq) for i in range(nq)]
    k16 = [k_ref[ts, :].astype(BF16) for ts in tiles]
    v16 = [v_ref[ts, :].astype(BF16) for ts in tiles]
    ii = lax.broadcasted_iota(jnp.int32, (tq, tq), 0)
    jj = lax.broadcasted_iota(jnp.int32, (tq, tq), 1)
    for qi in range(nq):
        q = (q_ref[tiles[qi], :] * HEAD_DIM ** -0.5).astype(BF16)
        m = l = acc = None
        for ki in range(qi + 1):
            s = _dot_t(q, k16[ki]) - ck_ref[0, ki]
            if ki == qi:
                s = jnp.where(jj <= ii, s, -1e30)
            s_max = jnp.max(s, axis=-1, keepdims=True)
            if ki == 0:
                m = s_max
                p = jnp.exp(s - m)
                l = jnp.sum(p, axis=-1, keepdims=True)
                acc = jnp.dot(p.astype(BF16), v16[ki], preferred_element_type=F32)
            else:
                m_new = jnp.maximum(m, s_max)
                alpha = jnp.exp(m - m_new)
                p = jnp.exp(s - m_new)
                l = alpha * l + jnp.sum(p, axis=-1, keepdims=True)
                acc = alpha * acc + jnp.dot(p.astype(BF16), v16[ki], preferred_element_type=F32)
                m = m_new
        o_ref[tiles[qi], :] = (acc / l).astype(o_ref.dtype)


def _fox_prompt(pm, q_blk, kh, vh, l, c_row, bsz, t, nch, tq):
    nq = t // tq
    kv_spec = pl.BlockSpec((None, None, None, t, HEAD_DIM), lambda b, h: (l, b, h, 0, 0))
    return pl.pallas_call(
        functools.partial(_fox_prompt_kernel, tq=tq),
        out_shape=jax.ShapeDtypeStruct((bsz * t, nch * HEAD_DIM), BF16),
        grid=(bsz, nch),
        in_specs=[pl.BlockSpec((t, HEAD_DIM), lambda b, h: (b, q_blk + h)),
                  kv_spec, kv_spec,
                  pl.BlockSpec((1, nq, 1, tq), lambda b, h: (b * nch + h, 0, 0, 0))],
        out_specs=pl.BlockSpec((t, HEAD_DIM), lambda b, h: (b, h)),
        compiler_params=_params("parallel", "parallel"),
        name="fox_prompt",
    )(pm, kh, vh, c_row)


def _ffn_up_kernel(a_ref, a2_ref, wg_ref, wv_ref, cwg_ref, cwv_ref, sg_ref, sv_ref, act_ref, ng_ref, nv_ref,
                   u2g_ref, u2v_ref, wcat_ref, *buf_refs, tm, tn, tpb, width):
    i = pl.program_id(1)
    halo = width - 1
    base = SUBLANES
    lo = base - halo
    n_sub = len(buf_refs)
    sub = tm // n_sub
    last = buf_refs[n_sub - 1]

    @pl.when(i == 0)
    def _():
        _cast_rows(wcat_ref, 0, tn, lambda rows: wg_ref[rows, :])
        _cast_rows(wcat_ref, tn, 2 * tn, lambda rows: wv_ref[rows, :])

    first = (i % tpb) == 0

    @pl.when(first)
    def _():
        buf_refs[0][lo:base, 0:tn] = sg_ref[0]
        buf_refs[0][lo:base, tn:2 * tn] = sv_ref[0]

    @pl.when(jnp.logical_not(first))
    def _():
        buf_refs[0][lo:base, :] = last[lo + sub:base + sub, :]

    def conv_rows(r):
        buf = buf_refs[r]
        yg = buf[lo:lo + sub, 0:tn] * cwg_ref[0:1, :]
        yv = buf[lo:lo + sub, tn:2 * tn] * cwv_ref[0:1, :]
        for t in range(1, width):
            yg = yg + buf[lo + t:lo + t + sub, 0:tn] * cwg_ref[t:t + 1, :]
            yv = yv + buf[lo + t:lo + t + sub, tn:2 * tn] * cwv_ref[t:t + 1, :]
        act_ref[r * sub:(r + 1) * sub, :] = (_silu(yg) * yv).astype(act_ref.dtype)

    for r in range(n_sub):
        buf_refs[r][base:base + sub, :] = jnp.dot(a_ref[r * sub:(r + 1) * sub, :], wcat_ref[...],
                                                  preferred_element_type=F32)
        if r + 1 < n_sub:
            buf_refs[r + 1][lo:base, :] = buf_refs[r][lo + sub:base + sub, :]
        if r > 0:
            conv_rows(r - 1)
    conv_rows(n_sub - 1)
    ng_ref[0] = last[lo + sub:base + sub, 0:tn]
    nv_ref[0] = last[lo + sub:base + sub, tn:2 * tn]

    @pl.when(i == pl.num_programs(1) - 1)
    def _():
        up2 = jnp.dot(a2_ref[...], wcat_ref[...], preferred_element_type=F32)
        u2g_ref[...] = up2[:, 0:tn]
        u2v_ref[...] = up2[:, tn:2 * tn]


def _ffn_up(h, h2, w_up, l, conv_w, state, bsz, t):
    m, d = h.shape
    r2 = h2.shape[0]
    f2 = w_up.shape[2]
    f = f2 // 2
    width = conv_w.shape[0]
    tm = _pick(t, (1024, 512, 256, 128, 64))
    tn = _pick(f, (256, 128))
    nj = f // tn
    tpb = t // tm
    n_sub = max(1, tm // FFN_SUB_ROWS)
    return pl.pallas_call(
        functools.partial(_ffn_up_kernel, tm=tm, tn=tn, tpb=tpb, width=width),
        out_shape=(jax.ShapeDtypeStruct((m, f), BF16),
                   jax.ShapeDtypeStruct((bsz, width - 1, f), F32),
                   jax.ShapeDtypeStruct((bsz, width - 1, f), F32),
                   jax.ShapeDtypeStruct((r2, f), F32),
                   jax.ShapeDtypeStruct((r2, f), F32)),
        grid=(nj, m // tm),
        in_specs=[pl.BlockSpec((tm, d), lambda j, i: (i, 0)),
                  pl.BlockSpec((r2, d), lambda j, i: (0, 0)),
                  pl.BlockSpec((None, d, tn), lambda j, i: (l, 0, j)),
                  pl.BlockSpec((None, d, tn), lambda j, i: (l, 0, j + nj)),
                  pl.BlockSpec((width, tn), lambda j, i: (0, j)),
                  pl.BlockSpec((width, tn), lambda j, i: (0, j + nj)),
                  pl.BlockSpec((1, width - 1, tn), lambda j, i: (i // tpb, 0, j)),
                  pl.BlockSpec((1, width - 1, tn), lambda j, i: (i // tpb, 0, j + nj))],
        out_specs=(pl.BlockSpec((tm, tn), lambda j, i: (i, j)),
                   pl.BlockSpec((1, width - 1, tn), lambda j, i: (i // tpb, 0, j)),
                   pl.BlockSpec((1, width - 1, tn), lambda j, i: (i // tpb, 0, j)),
                   pl.BlockSpec((r2, tn), lambda j, i: (0, j)),
                   pl.BlockSpec((r2, tn), lambda j, i: (0, j))),
        scratch_shapes=[pltpu.VMEM((d, 2 * tn), BF16)]
        + [pltpu.VMEM((SUBLANES + tm // n_sub, 2 * tn), F32) for _ in range(n_sub)],
        compiler_params=_params("arbitrary", "arbitrary"),
        name="ffn_up",
    )(h, h2, w_up, w_up, conv_w, conv_w, state, state)


def _col_bcast(row):
    n = row.shape[1]
    return jnp.transpose(jnp.broadcast_to(row, (n, n)))


def _sample_mix_kernel(pm_ref, ps_ref, pf_ref, gprev_ref, sprev_ref, cprev_ref, gw_ref, alog_ref, dtb_ref, nw_ref,
                       cw_ref, cb_ref, lg_ref, lb_ref, bf_ref,
                       oa_ref, ob_ref, snew_ref, gnew_ref, cnew_ref, lf_ref,
                       *, nkh, nvh, nch, z_off, glu_off, bch):
    kw = nkh * HEAD_DIM
    vw = nvh * HEAD_DIM
    aqkv = 2 * kw + vw
    rep = nvh // nkh
    gwidth = gw_ref.shape[0]
    x = pm_ref[0, :, 0:aqkv]
    gprev = gprev_ref[0]
    y = jnp.sum(gprev * gw_ref[0:gwidth - 1, :], axis=0, keepdims=True) + x * gw_ref[gwidth - 1:gwidth, :]
    gnew_ref[0, 0:gwidth - 2, :] = gprev[1:gwidth - 1, :]
    gnew_ref[0, gwidth - 2:gwidth - 1, :] = x
    y = _silu(y)
    ps = ps_ref[0]
    beta_all = jax.nn.sigmoid(ps[:, 0:nvh])
    g_all = -jnp.exp(alog_ref[...]) * _softplus(ps[:, nvh:2 * nvh] + dtb_ref[...])
    lf_ref[0] = -_softplus(-(pf_ref[0, :, 2 * nvh:2 * nvh + nch] + bf_ref[...]))
    nw = nw_ref[...]
    for kh in range(nkh):
        q = y[:, kh * HEAD_DIM:(kh + 1) * HEAD_DIM]
        k = y[:, kw + kh * HEAD_DIM:kw + (kh + 1) * HEAD_DIM]
        q = q * lax.rsqrt(jnp.sum(q * q, axis=-1, keepdims=True) + EPS) * HEAD_DIM ** -0.5
        k = k * lax.rsqrt(jnp.sum(k * k, axis=-1, keepdims=True) + EPS)
        q_cols = _col_bcast(q)
        k_cols = _col_bcast(k)
        qk = jnp.sum(q * k, axis=-1, keepdims=True)
        for r in range(rep):
            h = kh * rep + r
            v = y[:, 2 * kw + h * HEAD_DIM:2 * kw + (h + 1) * HEAD_DIM]
            beta = beta_all[:, h:h + 1]
            eg = jnp.exp(g_all[:, h:h + 1])
            s = sprev_ref[0, h]
            k_s = jnp.sum(k_cols * s, axis=0, keepdims=True)
            q_s = jnp.sum(q_cols * s, axis=0, keepdims=True)
            v_new = beta * v - (beta * eg) * k_s
            o = eg * q_s + qk * v_new
            snew_ref[0, h] = s * eg + k_cols * v_new
            on = o * lax.rsqrt(jnp.mean(o * o, axis=-1, keepdims=True) + EPS) * nw
            z = pm_ref[0, :, z_off + h * HEAD_DIM:z_off + (h + 1) * HEAD_DIM]
            oa_ref[0, :, h * HEAD_DIM:(h + 1) * HEAD_DIM] = (on * _silu(z)).astype(oa_ref.dtype)
    cwidth = cw_ref.shape[0]
    val = pm_ref[0, :, glu_off:glu_off + bch]
    gate = pm_ref[0, :, glu_off + bch:glu_off + 2 * bch]
    u = val * jax.nn.sigmoid(gate)
    cprev = cprev_ref[0]
    yc = (jnp.sum(cprev * cw_ref[0:cwidth - 1, :], axis=0, keepdims=True) + u * cw_ref[cwidth - 1:cwidth, :]
          + cb_ref[...])
    cnew_ref[0, 0:cwidth - 2, :] = cprev[1:cwidth - 1, :]
    cnew_ref[0, cwidth - 2:cwidth - 1, :] = u
    yc = yc - jnp.mean(yc, axis=-1, keepdims=True)
    yn = yc * lax.rsqrt(jnp.mean(yc * yc, axis=-1, keepdims=True) + EPS)
    ob_ref[0] = _silu(yn * lg_ref[...] + lb_ref[...]).astype(ob_ref.dtype)


def _sample_mix(pm, pg, gprev, sprev, cprev, gw, a_log, dt_bias, nw, cw, cb, lg, lb, b_f, nkh, z_off, glu_off):
    bsz = gprev.shape[0]
    nvh = a_log.shape[0]
    nch = b_f.shape[0]
    bch = cw.shape[1]
    vw = nvh * HEAD_DIM
    pmw = pm.shape[1]
    psw = HEAD_DIM
    pm3 = pm[:bsz].reshape(bsz, 1, pmw)
    ps3 = pg[:bsz, 0:HEAD_DIM].reshape(bsz, 1, psw)
    pf3 = pg[:bsz, HEAD_DIM:2 * HEAD_DIM].reshape(bsz, 1, psw)
    per_b = lambda *blk: pl.BlockSpec((1,) + blk, lambda b: (b,) + (0,) * len(blk))
    fixed = lambda a: pl.BlockSpec(a.shape, lambda b: (0,) * a.ndim)
    consts = [gw, a_log.reshape(1, nvh), dt_bias.reshape(1, nvh), nw.reshape(1, HEAD_DIM), cw, cb.reshape(1, bch),
              lg.reshape(1, bch), lb.reshape(1, bch), b_f.reshape(1, nch)]
    return pl.pallas_call(
        functools.partial(_sample_mix_kernel, nkh=nkh, nvh=nvh, nch=nch, z_off=z_off, glu_off=glu_off, bch=bch),
        out_shape=(jax.ShapeDtypeStruct((bsz, 1, vw), BF16),
                   jax.ShapeDtypeStruct((bsz, 1, bch), BF16),
                   jax.ShapeDtypeStruct(sprev.shape, F32),
                   jax.ShapeDtypeStruct(gprev.shape, F32),
                   jax.ShapeDtypeStruct(cprev.shape, F32),
                   jax.ShapeDtypeStruct((bsz, 1, nch), F32)),
        grid=(bsz,),
        in_specs=[per_b(1, pmw), per_b(1, psw), per_b(1, psw), per_b(*gprev.shape[1:]), per_b(*sprev.shape[1:]),
                  per_b(*cprev.shape[1:])] + [fixed(a) for a in consts],
        out_specs=(per_b(1, vw), per_b(1, bch), per_b(*sprev.shape[1:]), per_b(*gprev.shape[1:]),
                   per_b(*cprev.shape[1:]), per_b(1, nch)),
        compiler_params=_params("parallel"),
        name="sample_mix",
    )(pm3, ps3, pf3, gprev, sprev, cprev, *consts)


def _fox_sample_kernel(pt_ref, q_ref, kn_ref, vn_ref, lfn_ref, *refs, nch, hp, group, n_pages):
    kp_refs = refs[0:group]
    vp_refs = refs[group:2 * group]
    lfp_refs = refs[2 * group:3 * group]
    o_ref, qt_ref, m_ref, l_ref, r_ref, acc_ref = refs[3 * group:3 * group + 6]
    lft_refs = refs[3 * group + 6:]
    pi = pl.program_id(1)
    page = kp_refs[0].shape[1]
    scale = HEAD_DIM ** -0.5

    @pl.when(pi == 0)
    def _():
        q = q_ref[0]
        qt_ref[...] = jnp.transpose(q).astype(BF16)
        m_ref[...] = jnp.sum(jnp.transpose(q * kn_ref[0]), axis=0, keepdims=True) * scale
        l_ref[...] = jnp.ones_like(l_ref)
        r_ref[...] = lfn_ref[0]
        acc_ref[...] = vn_ref[0]

    jj = lax.broadcasted_iota(jnp.int32, (page, page), 0)
    mm = lax.broadcasted_iota(jnp.int32, (page, page), 1)
    later = (mm > jj).astype(BF16)
    lanes = lax.broadcasted_iota(jnp.int32, (page, HEAD_DIM), 1)
    slots = range(group)
    b = pl.program_id(0)

    @pl.when(pi == 0)
    def _():
        for g in slots:
            lft_refs[g][...] = jnp.zeros((HEAD_DIM, page), F32)

    lf = []
    for g in slots:
        r = pt_ref[b * n_pages + n_pages - 1 - (pi * group + g)] % LF_POOL_ROWS
        for h in range(nch):
            lft_refs[g][h:h + 1, :] = lfp_refs[g][h, pl.ds(r, 1), :]
        lf.append(jnp.transpose(lft_refs[g][...]))
    lf_hi = [x.astype(BF16) for x in lf]
    lf_lo = [(lf[g] - lf_hi[g].astype(F32)).astype(BF16) for g in slots]
    inner = [jnp.dot(later, lf_hi[g], preferred_element_type=F32)
             + jnp.dot(later, lf_lo[g], preferred_element_type=F32) for g in slots]
    total = [jnp.sum(x, axis=0, keepdims=True) for x in lf]
    r_after = [r_ref[...]]
    for g in slots:
        r_after.append(r_after[g] + total[g])
    qt = qt_ref[...]
    s = []
    for g in slots:
        sg = jnp.zeros((page, HEAD_DIM), F32)
        for h in range(nch):
            sg = jnp.where(lanes == h, jnp.dot(kp_refs[g][h].astype(BF16), qt, preferred_element_type=F32), sg)
        s.append(sg * scale + (r_after[g] + inner[g]))
    m_old = m_ref[...]
    m_new = m_old
    for g in slots:
        m_new = jnp.maximum(m_new, jnp.max(s[g], axis=0, keepdims=True))
    alpha = jnp.exp(m_old - m_new)
    p = [jnp.exp(s[g] - m_new) for g in slots]
    l_new = alpha * l_ref[...]
    for g in slots:
        l_new = l_new + jnp.sum(p[g], axis=0, keepdims=True)
    l_ref[...] = l_new
    m_ref[...] = m_new
    r_ref[...] = r_after[group]
    p_t = [jnp.transpose(p[g])[0:hp, :].astype(BF16) for g in slots]
    rows = lax.broadcasted_iota(jnp.int32, (hp, HEAD_DIM), 0)
    upd = jnp.zeros((hp, HEAD_DIM), F32)
    for h in range(nch):
        oh = jnp.dot(p_t[0], vp_refs[0][h].astype(BF16), preferred_element_type=F32)
        for g in range(1, group):
            oh = oh + jnp.dot(p_t[g], vp_refs[g][h].astype(BF16), preferred_element_type=F32)
        upd = jnp.where(rows == h, oh, upd)
    acc_ref[...] = acc_ref[...] * _col_bcast(alpha)[0:hp, :] + upd

    @pl.when(pi == pl.num_programs(1) - 1)
    def _():
        o_ref[0] = (acc_ref[...] / _col_bcast(l_ref[...])[0:hp, :]).astype(o_ref.dtype)


def _fox_sample(q, kn, vn, lfn, cache_k, cache_v, lf_h, l, page_table, nch):
    bsz = q.shape[0]
    n_pages = page_table.shape[1]
    page = cache_k.shape[3]
    hp = vn.shape[1]
    assert page == HEAD_DIM and cache_k.shape[4] == HEAD_DIM and cache_k.shape[2] == nch
    pt = page_table.reshape(-1)
    group = _pick(n_pages, (FOX_PAGE_GROUP, 2, 1))
    per_b = lambda *blk: pl.BlockSpec((1,) + blk, lambda b, p, pt: (b,) + (0,) * len(blk))

    def paged(g, *blk):
        return pl.BlockSpec((None, None) + blk,
                            lambda b, p, pt: (l, pt[b * n_pages + n_pages - 1 - (p * group + g)]) + (0,) * len(blk))

    def paged_lf(g):
        return pl.BlockSpec(
            (None, nch, LF_POOL_ROWS, page),
            lambda b, p, pt: (l, 0, pt[b * n_pages + n_pages - 1 - (p * group + g)] // LF_POOL_ROWS, 0))

    grid_spec = pltpu.PrefetchScalarGridSpec(
        num_scalar_prefetch=1,
        grid=(bsz, n_pages // group),
        in_specs=([per_b(HEAD_DIM, HEAD_DIM), per_b(HEAD_DIM, HEAD_DIM), per_b(hp, HEAD_DIM), per_b(1, HEAD_DIM)]
                  + [paged(g, nch, page, HEAD_DIM) for g in range(group)]
                  + [paged(g, nch, page, HEAD_DIM) for g in range(group)]
                  + [paged_lf(g) for g in range(group)]),
        out_specs=per_b(hp, HEAD_DIM),
        scratch_shapes=[pltpu.VMEM((HEAD_DIM, HEAD_DIM), BF16), pltpu.VMEM((1, HEAD_DIM), F32),
                        pltpu.VMEM((1, HEAD_DIM), F32), pltpu.VMEM((1, HEAD_DIM), F32),
                        pltpu.VMEM((hp, HEAD_DIM), F32)]
        + [pltpu.VMEM((HEAD_DIM, page), F32) for _ in range(group)],
    )
    return pl.pallas_call(
        functools.partial(_fox_sample_kernel, nch=nch, hp=hp, group=group, n_pages=n_pages),
        out_shape=jax.ShapeDtypeStruct((bsz, hp, HEAD_DIM), BF16),
        grid_spec=grid_spec,
        compiler_params=_params("parallel", "arbitrary"),
        name="fox_sample",
    )(pt, q, kn, vn, lfn, *([cache_k] * group + [cache_v] * group + [lf_h] * group))


def _ffn_sample_kernel(upg_ref, upv_ref, sg_ref, sv_ref, cwg_ref, cwv_ref, act_ref, *, width):
    ys = []
    for up_ref, s_ref, cw_ref in ((upg_ref, sg_ref, cwg_ref), (upv_ref, sv_ref, cwv_ref)):
        y = up_ref[...] * cw_ref[width - 1:width, :]
        for i in range(width - 1):
            y = y + s_ref[i] * cw_ref[i:i + 1, :]
        ys.append(y)
    act_ref[...] = (_silu(ys[0]) * ys[1]).astype(act_ref.dtype)


def _ffn_sample(up, state_t, conv_w):
    r, f2 = up.shape
    f = f2 // 2
    width = conv_w.shape[0]
    tn = _pick(f, (1024, 512, 256, 128))
    nj = f // tn
    return pl.pallas_call(
        functools.partial(_ffn_sample_kernel, width=width),
        out_shape=jax.ShapeDtypeStruct((r, f), BF16),
        grid=(nj,),
        in_specs=[pl.BlockSpec((r, tn), lambda j: (0, j)), pl.BlockSpec((r, tn), lambda j: (0, j + nj)),
                  pl.BlockSpec((width - 1, r, tn), lambda j: (0, 0, j)),
                  pl.BlockSpec((width - 1, r, tn), lambda j: (0, 0, j + nj)),
                  pl.BlockSpec((width, tn), lambda j: (0, j)), pl.BlockSpec((width, tn), lambda j: (0, j + nj))],
        out_specs=pl.BlockSpec((r, tn), lambda j: (0, j)),
        compiler_params=_params("parallel"),
        name="ffn_sample",
    )(up, up, state_t, state_t, conv_w, conv_w)


def _in_proj(h, h2, w_in, l, o_b, shift, glu_w, cw, tn, bsz, t, k_prev, v_prev):
    main = _wmatmul(h, w_in, l, 0, o_b + glu_w + cw, tn, o_b, shift, a2=h2)
    k = _wmatmul(h, w_in, l, o_b + glu_w + cw, cw, tn, o_b, shift, a2=h2, heads=(bsz, t, k_prev))
    v = _wmatmul(h, w_in, l, o_b + glu_w + 2 * cw, cw, tn, o_b, shift, a2=h2, heads=(bsz, t, v_prev))
    pg = _wmatmul(h, w_in, l, o_b, HEAD_DIM, HEAD_DIM, a2=h2, pair_col=o_b + glu_w + 3 * cw)
    return main, k, v, pg


def kernel(x_prompt, x_sample, cache_k, cache_v, cache_logf, page_table, state_gdn, state_gdn_conv, state_conf_conv, state_ffn_conv, norm_mix, w_in, gdn_conv_w, gdn_a_log, gdn_dt_bias, gdn_norm_w, conf_dw_w, conf_dw_b, conf_ln_g, conf_ln_b, fox_b_f, w_out, norm_ffn, ffn_conv_w, w_up, w_down, norm_final):
    bp, t, d = x_prompt.shape
    bs, ts, _ = x_sample.shape
    assert ts == 1, "the sample step handles one new token per sequence"
    depth = w_in.shape[0]
    nvh = gdn_a_log.shape[1]
    aqkv = gdn_conv_w.shape[2]
    vw = nvh * HEAD_DIM
    kw = (aqkv - vw) // 2
    nkh = kw // HEAD_DIM
    nch = fox_b_f.shape[1]
    cw = nch * HEAD_DIM
    bch = conf_dw_w.shape[2]
    f = w_down.shape[1]
    o_b = aqkv + vw
    shift = 2 * nvh
    glu_w = 2 * bch
    tn_in = _pick(o_b, (512, 256, 128))
    assert (t % GDN_CHUNK == 0 and cw == vw and aqkv % vw == 0 and o_b % (bch // 2) == 0
            and glu_w % tn_in == 0 and cw % tn_in == 0)
    z_blk = aqkv // vw
    glu_off = o_b
    q_off = o_b + glu_w
    tq = _pick(t, (512, 256, 128))
    nq = t // tq
    n_chunks = t // GDN_CHUNK
    hp = -(-nch // SUBLANES) * SUBLANES

    xp = x_prompt.reshape(bp * t, d)
    rs = -(-bs // ROW_PAD) * ROW_PAD
    xs = jnp.zeros((rs, d), F32).at[:bs].set(x_sample.reshape(bs, d))
    lf_h = cache_logf.transpose(0, 3, 1, 2)
    cache_kh = cache_k.transpose(0, 1, 3, 2, 4)
    cache_vh = cache_v.transpose(0, 1, 3, 2, 4)
    k_all = v_all = None
    w_o = w_out.astype(BF16)
    w_d = w_down.astype(BF16)
    outs_p, outs_s = [], []
    for l in range(depth):

        h = _rmsnorm(xp, norm_mix[l], BF16)
        hs = _rmsnorm(xs, norm_mix[l], BF16)
        (pm, pm_s), (k_all, kc_s), (v_all, vc_s), (pg, pg_s) = _in_proj(
            h, hs, w_in, l, o_b, shift, glu_w, cw, tn_in, bp, t, k_all, v_all)

        gb, logf, ccum = _gates(pg, gdn_a_log[l], gdn_dt_bias[l], fox_b_f[l], bp, t)
        qk, vact, gconv_new = _gdn_prep(pm, gdn_conv_w[l], jnp.zeros((bp, gdn_conv_w.shape[1] - 1, aqkv), F32),
                                        bp, t, kw, vw)
        grow = gb[:, :nvh].reshape(bp * n_chunks, GDN_CHUNK, nvh).transpose(0, 2, 1)
        o_a, s_new = _gdn_chunks(qk, vact, pm, z_blk, gb, grow, jnp.zeros((bp, nvh, HEAD_DIM, HEAD_DIM), F32),
                                 gdn_norm_w[l], bp, t, nkh, nvh)
        o_bm, cconv_new = _conformer(pm, glu_off // (bch // 2), conf_dw_w[l], conf_dw_b[l], conf_ln_g[l],
                                     conf_ln_b[l], jnp.zeros((bp, conf_dw_w.shape[1] - 1, bch), F32), bp, t)
        c_t = ccum.reshape(bp, t, nch).transpose(0, 2, 1)
        o_c = _fox_prompt(pm, q_off // HEAD_DIM, k_all, v_all, l, c_t.reshape(bp * nch, nq, 1, tq), bp, t, nch, tq)
        xp = _matmul([o_a, o_bm, o_c], w_o, l, res=xp)
        outs_p.append((logf.reshape(bp, t, nch), s_new, gconv_new, cconv_new))

        o_a, o_bm, s_new, gconv_new, cconv_new, logf = _sample_mix(
            pm_s, pg_s, state_gdn_conv[l], state_gdn[l], state_conf_conv[l], gdn_conv_w[l], gdn_a_log[l],
            gdn_dt_bias[l], gdn_norm_w[l], conf_dw_w[l], conf_dw_b[l], conf_ln_g[l], conf_ln_b[l], fox_b_f[l],
            nkh, aqkv, glu_off)
        head_rows = lambda a, rows: jnp.zeros((bs, rows, HEAD_DIM), F32).at[:, :nch].set(a.reshape(bs, nch, HEAD_DIM))
        lfn = jnp.zeros((bs, 1, HEAD_DIM), F32).at[:, :, :nch].set(logf)
        o_c = _fox_sample(head_rows(pm_s[:bs, q_off:q_off + cw], HEAD_DIM), head_rows(kc_s[:bs], HEAD_DIM),
                          head_rows(vc_s[:bs], hp), lfn, cache_kh, cache_vh, lf_h, l, page_table, nch)
        o_c = o_c[:, :nch].reshape(bs, cw)
        pad_rows = lambda a: jnp.zeros((rs, a.shape[-1]), a.dtype).at[:bs].set(a.reshape(bs, -1))
        xs = _matmul([pad_rows(o_a), pad_rows(o_bm), pad_rows(o_c)], w_o, l, res=xs)

        h = _rmsnorm(xp, norm_ffn[l], BF16)
        hs = _rmsnorm(xs, norm_ffn[l], BF16)
        act, fnew_g, fnew_v, up_g, up_v = _ffn_up(h, hs, w_up, l, ffn_conv_w[l],
                                                  jnp.zeros((bp, ffn_conv_w.shape[1] - 1, 2 * f), F32), bp, t)
        xp = _matmul([act], w_d, l, res=xp)
        up = jnp.concatenate([up_g, up_v], axis=1)
        st = jnp.zeros((ffn_conv_w.shape[1] - 1, rs, 2 * f), F32).at[:, :bs].set(state_ffn_conv[l].transpose(1, 0, 2))
        act = _ffn_sample(up, st, ffn_conv_w[l])
        xs = _matmul([act], w_d, l, res=xs)
        outs_p[-1] += (jnp.concatenate([fnew_g, fnew_v], axis=-1),)
        fconv_new = jnp.concatenate([state_ffn_conv[l][:, 1:], up[:bs, None, :]], axis=1)
        outs_s.append((kc_s[:bs].reshape(bs, 1, nch, HEAD_DIM), vc_s[:bs].reshape(bs, 1, nch, HEAD_DIM), logf,
                       s_new, gconv_new, cconv_new, fconv_new))

    y_prompt = _rmsnorm(xp, norm_final, F32).reshape(bp, t, d)
    y_sample = _rmsnorm(xs, norm_final, F32)[:bs].reshape(bs, 1, d)
    stack = lambda outs, i: jnp.stack([o[i] for o in outs], axis=0)
    k_rows_p = k_all.transpose(0, 1, 3, 2, 4)
    v_rows_p = v_all.transpose(0, 1, 3, 2, 4)
    return ((y_prompt, y_sample, k_rows_p, v_rows_p) + tuple(stack(outs_p, i) for i in range(5))
            + tuple(stack(outs_s, i) for i in range(7)))
```
